```python
import math
import jax, jax.numpy as jnp
from jax import lax
import numpy as np

D_MODEL = 4096
BATCH = 8
SEQ = 2048
DEPTH = 2

CHUNK = 64
MEM_LEN = 256
EPS = 1e-6
ROPE_THETA = 10000.0
FFN_DIM = 8192

POOL_WINDOWS = (2, 4, 8, 16)
POOL_GROUPS = 4
POOL_WIDTH = 2048
POOL_GW = POOL_WIDTH // POOL_GROUPS
SG_WIDTH = 1024
SG_BLOCK = 128
SG_GROUPS = 4
SG_GW = SG_WIDTH // SG_GROUPS
SSM_HEADS = 16
SSM_HEADDIM = 64
SSM_INNER = SSM_HEADS * SSM_HEADDIM
SSM_GROUPS = 4
SSM_STATE = 128
SSM_CONV = 4
SSM_CONV_CH = SSM_INNER + 2 * SSM_GROUPS * SSM_STATE
SSM_CHUNK = CHUNK
ATT_HEADS = 8
ATT_KV_HEADS = 2
ATT_HEADDIM = 128
IDX_HEADS = 8
IDX_HEADDIM = 64
IDX_TOPK = 256
Q_BLOCK = 128
MEM_HEADS = 4
MEM_HEADDIM = 128

IN_SPLITS = (POOL_WIDTH, SG_WIDTH, SG_WIDTH, SSM_INNER, SSM_CONV_CH, SSM_HEADS,
             ATT_HEADS * ATT_HEADDIM, ATT_KV_HEADS * ATT_HEADDIM, ATT_KV_HEADS * ATT_HEADDIM,
             IDX_HEADS * IDX_HEADDIM, IDX_HEADDIM, IDX_HEADS)
IN_WIDTH = 9304
BRANCH_WIDTHS = (POOL_WIDTH, SG_WIDTH, SSM_INNER, ATT_HEADS * ATT_HEADDIM)
BRANCH_TOTAL = 5120
N_BRANCH = 4

kernel_name = 'hybrid_streaming_gated_block'


def rmsnorm(x, g):
    x32 = x.astype(jnp.float32)
    y = x32 * lax.rsqrt(jnp.mean(x32 * x32, axis=-1, keepdims=True) + EPS)
    return (y * g.astype(jnp.float32)).astype(x.dtype)


def layernorm(x, g, b):
    x32 = x.astype(jnp.float32)
    mu = jnp.mean(x32, axis=-1, keepdims=True)
    xc = x32 - mu
    y = xc * lax.rsqrt(jnp.mean(xc * xc, axis=-1, keepdims=True) + EPS)
    return (y * g.astype(jnp.float32) + b.astype(jnp.float32)).astype(x.dtype)


def rope(x, pos):
    half = x.shape[-1] // 2
    inv = ROPE_THETA ** (-jnp.arange(half, dtype=jnp.float32) / half)
    ang = pos.astype(jnp.float32)[:, None] * inv[None, :]
    shape = (1, pos.shape[0]) + (1,) * (x.ndim - 3) + (half,)
    cos = jnp.cos(ang).reshape(shape)
    sin = jnp.sin(ang).reshape(shape)
    xf = x.astype(jnp.float32)
    x1, x2 = xf[..., :half], xf[..., half:]
    return jnp.concatenate([x1 * cos - x2 * sin, x2 * cos + x1 * sin], axis=-1).astype(x.dtype)


def swiglu_ffn(h, w_in, w_out):
    gate, up = jnp.split(h @ w_in, 2, axis=-1)
    return (jax.nn.silu(gate) * up) @ w_out


def pool_mixer(a, pool_w, pool_scale):
    bsz, L, _ = a.shape
    a4 = a.reshape(bsz, L, POOL_GROUPS, POOL_GW)
    cs = jnp.cumsum(a4.astype(jnp.float32), axis=1)
    cs = jnp.pad(cs, ((0, 0), (1, 0), (0, 0), (0, 0)))
    t1 = jnp.arange(1, L + 1)[:, None]
    lo = jnp.maximum(t1 - jnp.array(POOL_WINDOWS)[None, :], 0)
    cs_lo = cs[:, lo, jnp.arange(POOL_GROUPS)[None, :], :]
    pooled = (cs[:, 1:] - cs_lo) / (t1 - lo).astype(jnp.float32)[None, :, :, None]
    mixed = (pooled - a4.astype(jnp.float32)).astype(a.dtype)
    out = jnp.einsum('blgc,gcd->blgd', mixed, pool_w) * pool_scale
    return out.reshape(bsz, L, POOL_WIDTH)


def spatial_gating_mixer(u, v, ln_g, ln_b, w_s, b_s):
    u = jax.nn.gelu(u, approximate=False)
    v = layernorm(jax.nn.gelu(v, approximate=False), ln_g, ln_b)
    bsz, L, _ = v.shape
    nb = L // SG_BLOCK
    cid = jnp.arange(SG_BLOCK) // CHUNK
    mask = cid[:, None] >= cid[None, :]
    w = jnp.where(mask[None], w_s, 0.0)
    vb = v.reshape(bsz, nb, SG_BLOCK, SG_GROUPS, SG_GW)
    sv = jnp.einsum('gij,bnjgc->bnigc', w, vb) + b_s.T[None, None, :, :, None]
    return u * sv.reshape(bsz, L, SG_WIDTH)


def ssd_scan(xs, dt, A, Bm, Cm):
    bsz, L, H, P = xs.shape
    G, N = Bm.shape[2], Bm.shape[3]
    Hg = H // G
    nc = L // SSM_CHUNK
    xd = (xs * dt[..., None]).reshape(bsz, nc, SSM_CHUNK, G, Hg, P)
    a_cs = jnp.cumsum((dt * A).reshape(bsz, nc, SSM_CHUNK, G, Hg), axis=2)
    Bc = Bm.reshape(bsz, nc, SSM_CHUNK, G, N)
    Cc = Cm.reshape(bsz, nc, SSM_CHUNK, G, N)
    tri = jnp.tril(jnp.ones((SSM_CHUNK, SSM_CHUNK), dtype=bool))
    seg = a_cs[:, :, :, None] - a_cs[:, :, None, :]
    decay = jnp.exp(jnp.where(tri[None, None, :, :, None, None], seg, -jnp.inf))
    cb = jnp.einsum('bclgn,bcsgn->bclsg', Cc, Bc)
    y_diag = jnp.einsum('bclsg,bclsgh,bcsghp->bclghp', cb, decay, xd)
    to_end = jnp.exp(a_cs[:, :, -1:] - a_cs)
    chunk_states = jnp.einsum('bclgn,bclgh,bclghp->bcghpn', Bc, to_end, xd)
    chunk_decay = jnp.exp(a_cs[:, :, -1])

    def step(state, inp):
        dec, new = inp
        return state * dec[..., None, None] + new, state

    init = jnp.zeros((bsz, G, Hg, P, N), xs.dtype)
    _, prev = lax.scan(step, init, (jnp.moveaxis(chunk_decay, 1, 0), jnp.moveaxis(chunk_states, 1, 0)))
    prev = jnp.moveaxis(prev, 0, 1)
    y_off = jnp.einsum('bclgn,bcghpn,bclgh->bclghp', Cc, prev, jnp.exp(a_cs))
    return (y_diag + y_off).reshape(bsz, L, H, P)


def mamba2_mixer(z, xbc, dt_raw, conv_w, conv_b, a_log, dt_bias, d_skip, norm_g):
    bsz, L, _ = xbc.shape
    xbc = lax.conv_general_dilated(xbc, conv_w[:, None, :], window_strides=(1,),
                                   padding=[(SSM_CONV - 1, 0)],
                                   dimension_numbers=('NWC', 'WIO', 'NWC'),
                                   feature_group_count=SSM_CONV_CH) + conv_b
    xbc = jax.nn.silu(xbc)
    xs, Bm, Cm = jnp.split(xbc, [SSM_INNER, SSM_INNER + SSM_GROUPS * SSM_STATE], axis=-1)
    xs = xs.reshape(bsz, L, SSM_HEADS, SSM_HEADDIM).astype(jnp.float32)
    Bm = Bm.reshape(bsz, L, SSM_GROUPS, SSM_STATE).astype(jnp.float32)
    Cm = Cm.reshape(bsz, L, SSM_GROUPS, SSM_STATE).astype(jnp.float32)
    dt = jax.nn.softplus(dt_raw.astype(jnp.float32) + dt_bias.astype(jnp.float32))
    A = -jnp.exp(a_log.astype(jnp.float32))
    y = ssd_scan(xs, dt, A, Bm, Cm) + xs * d_skip.astype(jnp.float32)[:, None]
    y = y.reshape(bsz, L, SSM_INNER).astype(z.dtype) * jax.nn.silu(z)
    y = rmsnorm(y.reshape(bsz, L, SSM_GROUPS, SSM_INNER // SSM_GROUPS),
                norm_g.reshape(SSM_GROUPS, SSM_INNER // SSM_GROUPS))
    return y.reshape(bsz, L, SSM_INNER)


def dsa_mixer(q, k, v, q_idx, k_idx, w_idx):
    bsz, L, _ = q.shape
    pos = jnp.arange(L)
    cid = pos // CHUNK
    grp = ATT_HEADS // ATT_KV_HEADS
    q = rope(q.reshape(bsz, L, ATT_HEADS, ATT_HEADDIM), pos).reshape(bsz, L, ATT_KV_HEADS, grp, ATT_HEADDIM)
    k = rope(k.reshape(bsz, L, ATT_KV_HEADS, ATT_HEADDIM), pos)
    v = v.reshape(bsz, L, ATT_KV_HEADS, ATT_HEADDIM)
    q_idx = rope(q_idx.reshape(bsz, L, IDX_HEADS, IDX_HEADDIM), pos)
    k_idx = rope(k_idx, pos)
    w_idx = w_idx * (IDX_HEADS ** -0.5)
    topk = min(IDX_TOPK, L // 4)
    scale = ATT_HEADDIM ** -0.5

    def block(i):
        t0 = i * Q_BLOCK
        qb = lax.dynamic_slice_in_dim(q, t0, Q_BLOCK, axis=1)
        qib = lax.dynamic_slice_in_dim(q_idx, t0, Q_BLOCK, axis=1)
        wib = lax.dynamic_slice_in_dim(w_idx, t0, Q_BLOCK, axis=1)
        cq = (t0 + jnp.arange(Q_BLOCK)) // CHUNK
        logits = jax.nn.relu(jnp.einsum('bqhd,bsd->bqhs', qib, k_idx))
        iscore = jnp.einsum('bqhs,bqh->bqs', logits, wib).astype(jnp.float32)
        adm = cid[None, None, :] <= cq[None, :, None]
        iscore = jnp.where(adm, iscore, -jnp.inf)
        _, idx = lax.top_k(iscore, topk)
        kg = jax.vmap(lambda kk, ii: kk[ii])(k, idx)
        vg = jax.vmap(lambda vv, ii: vv[ii])(v, idx)
        valid = cid[idx] <= cq[None, :, None]
        s = jnp.einsum('bqhgd,bqkhd->bqhgk', qb, kg).astype(jnp.float32) * scale
        s = jnp.where(valid[:, :, None, None, :], s, -jnp.inf)
        p = jax.nn.softmax(s, axis=-1).astype(v.dtype)
        o = jnp.einsum('bqhgk,bqkhd->bqhgd', p, vg)
        return o.reshape(bsz, Q_BLOCK, ATT_HEADS * ATT_HEADDIM)

    out = lax.map(block, jnp.arange(L // Q_BLOCK))
    return jnp.transpose(out, (1, 0, 2, 3)).reshape(bsz, L, ATT_HEADS * ATT_HEADDIM)


def gated_merge(h, branches, w_branch, w_gate):
    out = None
    row = 0
    for i, y in enumerate(branches):
        width = y.shape[-1]
        term = jax.nn.sigmoid(h @ w_gate[i]) * (y @ w_branch[row:row + width])
        row += width
        out = term if out is None else out + term
    return out


def memory_cross_attention(h, mem_n, w_q, w_kv, w_o):
    bsz, L, _ = h.shape
    q = (h @ w_q).reshape(bsz, L, MEM_HEADS, MEM_HEADDIM)
    kv = (mem_n @ w_kv).reshape(bsz, mem_n.shape[1], 2, MEM_HEADS, MEM_HEADDIM)
    k, v = kv[:, :, 0], kv[:, :, 1]
    s = jnp.einsum('blhd,bmhd->bhlm', q, k).astype(jnp.float32) * (MEM_HEADDIM ** -0.5)
    p = jax.nn.softmax(s, axis=-1).astype(h.dtype)
    o = jnp.einsum('bhlm,bmhd->blhd', p, v).reshape(bsz, L, MEM_HEADS * MEM_HEADDIM)
    return o @ w_o


def setup_inputs(seed: int = 0) -> dict:
    key = jax.random.key(seed)
    ks = jax.random.split(key, 32)
    f32 = jnp.float32
    D = D_MODEL

    def nrm(k, shape, scale):
        return scale * jax.random.normal(k, shape, f32)

    def gain(k, shape):
        return 1.0 + 0.02 * jax.random.normal(k, shape, f32)

    dt = jnp.exp(jax.random.uniform(ks[16], (DEPTH, SSM_HEADS), f32, math.log(1e-3), math.log(1e-1)))
    row_scale = jnp.concatenate([jnp.full((w,), w ** -0.5, f32) for w in BRANCH_WIDTHS])
    return {
        'x': nrm(ks[0], (BATCH, SEQ, D), 1.0),
        'mem': nrm(ks[1], (BATCH, MEM_LEN, D), 1.0),
        'g_ffn1': gain(ks[2], (DEPTH, D)),
        'w_ffn1_in': nrm(ks[3], (DEPTH, D, 2 * FFN_DIM), D ** -0.5),
        'w_ffn1_out': nrm(ks[4], (DEPTH, FFN_DIM, D), FFN_DIM ** -0.5),
        'g_mix': gain(ks[5], (DEPTH, D)),
        'w_in': nrm(ks[6], (DEPTH, D, IN_WIDTH), D ** -0.5),
        'pool_w': nrm(ks[7], (DEPTH, POOL_GROUPS, POOL_GW, POOL_GW), POOL_GW ** -0.5),
        'pool_scale': 1.0 + nrm(ks[8], (DEPTH, POOL_GROUPS, POOL_GW), 0.1),
        'sg_ln_g': gain(ks[9], (DEPTH, SG_WIDTH)),
        'sg_ln_b': nrm(ks[10], (DEPTH, SG_WIDTH), 0.02),
        'sg_w': nrm(ks[11], (DEPTH, SG_GROUPS, SG_BLOCK, SG_BLOCK), SG_BLOCK ** -0.5),
        'sg_b': 1.0 + nrm(ks[12], (DEPTH, SG_GROUPS, SG_BLOCK), 0.1),
        'ssm_conv_w': nrm(ks[13], (DEPTH, SSM_CONV, SSM_CONV_CH), SSM_CONV ** -0.5),
        'ssm_conv_b': nrm(ks[14], (DEPTH, SSM_CONV_CH), 0.02),
        'ssm_a_log': jnp.log(jax.random.uniform(ks[15], (DEPTH, SSM_HEADS), f32, 1.0, 16.0)),
        'ssm_dt_bias': dt + jnp.log(-jnp.expm1(-dt)),
        'ssm_d': 1.0 + nrm(ks[17], (DEPTH, SSM_HEADS), 0.1),
        'ssm_norm_g': gain(ks[18], (DEPTH, SSM_INNER)),
        'w_branch': jax.random.normal(ks[19], (DEPTH, BRANCH_TOTAL, D), f32) * row_scale[None, :, None],
        'w_gate': nrm(ks[20], (DEPTH, N_BRANCH, D, D), D ** -0.5),
        'w_out': nrm(ks[21], (DEPTH, D, D), D ** -0.5),
        'g_mem': gain(ks[22], (D,)),
        'g_cross': gain(ks[23], (DEPTH, D)),
        'w_mem_q': nrm(ks[24], (DEPTH, D, MEM_HEADS * MEM_HEADDIM), D ** -0.5),
        'w_mem_kv': nrm(ks[25], (DEPTH, D, 2 * MEM_HEADS * MEM_HEADDIM), D ** -0.5),
        'w_mem_o': nrm(ks[26], (DEPTH, MEM_HEADS * MEM_HEADDIM, D), (MEM_HEADS * MEM_HEADDIM) ** -0.5),
        'g_ffn2': gain(ks[27], (DEPTH, D)),
        'w_ffn2_in': nrm(ks[28], (DEPTH, D, 2 * FFN_DIM), D ** -0.5),
        'w_ffn2_out': nrm(ks[29], (DEPTH, FFN_DIM, D), FFN_DIM ** -0.5),
        'g_final': gain(ks[30], (D,)),
    }


def reference(x, mem, g_ffn1, w_ffn1_in, w_ffn1_out, g_mix, w_in, pool_w, pool_scale,
              sg_ln_g, sg_ln_b, sg_w, sg_b, ssm_conv_w, ssm_conv_b, ssm_a_log, ssm_dt_bias,
              ssm_d, ssm_norm_g, w_branch, w_gate, w_out, g_mem, g_cross, w_mem_q, w_mem_kv,
              w_mem_o, g_ffn2, w_ffn2_in, w_ffn2_out, g_final):
    mem_n = rmsnorm(mem, g_mem)
    split_points = [int(p) for p in np.cumsum(IN_SPLITS)[:-1]]
    for l in range(DEPTH):
        x = x + 0.5 * swiglu_ffn(rmsnorm(x, g_ffn1[l]), w_ffn1_in[l], w_ffn1_out[l])
        h = rmsnorm(x, g_mix[l])
        (c_pool, c_u, c_v, c_z, c_xbc, c_dt, c_q, c_k, c_val, c_qi, c_ki, c_wi) = jnp.split(
            h @ w_in[l], split_points, axis=-1)
        y_a = pool_mixer(c_pool, pool_w[l], pool_scale[l])
        y_b = spatial_gating_mixer(c_u, c_v, sg_ln_g[l], sg_ln_b[l], sg_w[l], sg_b[l])
        y_c = mamba2_mixer(c_z, c_xbc, c_dt, ssm_conv_w[l], ssm_conv_b[l], ssm_a_log[l],
                           ssm_dt_bias[l], ssm_d[l], ssm_norm_g[l])
        y_d = dsa_mixer(c_q, c_k, c_val, c_qi, c_ki, c_wi)
        x = x + gated_merge(h, (y_a, y_b, y_c, y_d), w_branch[l], w_gate[l]) @ w_out[l]
        x = x + memory_cross_attention(rmsnorm(x, g_cross[l]), mem_n, w_mem_q[l], w_mem_kv[l], w_mem_o[l])
        x = x + 0.5 * swiglu_ffn(rmsnorm(x, g_ffn2[l]), w_ffn2_in[l], w_ffn2_out[l])
    return rmsnorm(x, g_final)
```

```python
import functools

import jax
import jax.numpy as jnp
import numpy as np
from jax import lax
from jax.experimental import pallas as pl
from jax.experimental.pallas import tpu as pltpu

F32 = jnp.float32
BF16 = jnp.bfloat16

D_MODEL = 4096
FFN_DIM = 8192
CHUNK = 64
EPS = 1e-6
ROPE_THETA = 10000.0

POOL_WINDOWS = (2, 4, 8, 16)
POOL_GROUPS = 4
POOL_WIDTH = 2048
POOL_GW = POOL_WIDTH // POOL_GROUPS
POOL_PAD = 16

SG_WIDTH = 1024
SG_BLOCK = 128
SG_GROUPS = 4
SG_GW = SG_WIDTH // SG_GROUPS

SSM_HEADS = 16
SSM_HEADDIM = 64
SSM_INNER = SSM_HEADS * SSM_HEADDIM
SSM_GROUPS = 4
SSM_STATE = 128
SSM_CONV = 4
SSM_HG = SSM_HEADS // SSM_GROUPS
SSM_GW = SSM_INNER // SSM_GROUPS

ATT_HEADS = 8
ATT_KV_HEADS = 2
ATT_HEADDIM = 128
ATT_GRP = ATT_HEADS // ATT_KV_HEADS
IDX_HEADS = 8
IDX_HEADDIM = 64
IDX_TOPK = 256
Q_BLOCK = 128

MEM_HEADS = 4
MEM_HEADDIM = 128

C_POOL, C_U, C_V, C_Z, C_XBC = 0, 2048, 3072, 4096, 5120
C_DT, C_Q, C_K, C_VAL, C_QI, C_KI, C_WI = 7168, 7184, 8208, 8464, 8720, 9232, 9296
SEG_A = 7168
SEG_Q = 2048
SEG_S = 256
S_DT_LANE = 0
S_WI_LANE = 16

LANE = 128
VMEM_LIMIT = 56 * 1024 * 1024
NEG_BIG = -1e30
INT_MIN = -2147483648


def _cparams(sem):
    return pltpu.CompilerParams(dimension_semantics=sem, vmem_limit_bytes=VMEM_LIMIT)


def _sigmoid(x):
    return 1.0 / (1.0 + jnp.exp(-x))


def _silu(x):
    return x * _sigmoid(x)


def _gelu(x):
    return 0.5 * x * (1.0 + lax.erf(x * np.float32(1.0 / np.sqrt(2.0))))


def _softplus(x):
    return jnp.maximum(x, 0.0) + jnp.log1p(jnp.exp(-jnp.abs(x)))


def _rmsnorm_kernel(x_ref, g_ref, o_ref):
    x = x_ref[...]
    ms = jnp.mean(x * x, axis=-1, keepdims=True)
    o_ref[...] = (x * lax.rsqrt(ms + EPS) * g_ref[...]).astype(o_ref.dtype)


def _rmsnorm(x, g, out_dtype):
    m, d = x.shape
    tm = min(512, m)
    return pl.pallas_call(
        _rmsnorm_kernel,
        grid=(m // tm,),
        in_specs=[pl.BlockSpec((tm, d), lambda i: (i, 0)),
                  pl.BlockSpec((1, d), lambda i: (0, 0))],
        out_specs=pl.BlockSpec((tm, d), lambda i: (i, 0)),
        out_shape=jax.ShapeDtypeStruct((m, d), out_dtype),
        compiler_params=_cparams(("parallel",)),
        name="rmsnorm",
    )(x, g.reshape(1, d))


def _mm_kernel(x_ref, w_ref, o_ref):
    o_ref[...] = jnp.dot(x_ref[...], w_ref[...], preferred_element_type=F32).astype(o_ref.dtype)


def _mm_res_kernel(x_ref, w_ref, r_ref, o_ref, *, alpha):
    acc = jnp.dot(x_ref[...], w_ref[...], preferred_element_type=F32)
    o_ref[...] = r_ref[...] + alpha * acc


def _mm_res_acc_kernel(x_ref, w_ref, r_ref, o_ref, acc_ref, *, alpha, nk):
    k = pl.program_id(2)
    part = jnp.dot(x_ref[...], w_ref[...], preferred_element_type=F32)

    @pl.when(k == 0)
    def _():
        acc_ref[...] = part

    @pl.when(k > 0)
    def _():
        acc_ref[...] += part

    @pl.when(k == nk - 1)
    def _():
        o_ref[...] = r_ref[...] + alpha * acc_ref[...]


def _w_spec(w, widx, kblk, tn, col_blk0, kgrid):
    lead = (None,) * len(widx)
    if kgrid:
        return pl.BlockSpec(lead + (kblk, tn), lambda i, j, k: (*widx, k, j + col_blk0))
    return pl.BlockSpec(lead + (kblk, tn), lambda i, j: (*widx, 0, j + col_blk0))


def _matmul(x, w, widx, n, *, col0=0, tm=1024, tn=512, out_dtype=BF16):
    m, kdim = x.shape
    tm, tn = min(tm, m), min(tn, n)
    return pl.pallas_call(
        _mm_kernel,
        grid=(m // tm, n // tn),
        in_specs=[pl.BlockSpec((tm, kdim), lambda i, j: (i, 0)),
                  _w_spec(w, widx, kdim, tn, col0 // tn, False)],
        out_specs=pl.BlockSpec((tm, tn), lambda i, j: (i, j)),
        out_shape=jax.ShapeDtypeStruct((m, n), out_dtype),
        compiler_params=_cparams(("parallel", "arbitrary")),
        name="matmul",
    )(x, w)


def _matmul_res(x, w, widx, res, alpha, *, tm=1024, tn=512, tk=None):
    m, kdim = x.shape
    n = res.shape[1]
    tm, tn = min(tm, m), min(tn, n)
    if tk is None or tk >= kdim:
        return pl.pallas_call(
            functools.partial(_mm_res_kernel, alpha=alpha),
            grid=(m // tm, n // tn),
            in_specs=[pl.BlockSpec((tm, kdim), lambda i, j: (i, 0)),
                      _w_spec(w, widx, kdim, tn, 0, False),
                      pl.BlockSpec((tm, tn), lambda i, j: (i, j))],
            out_specs=pl.BlockSpec((tm, tn), lambda i, j: (i, j)),
            out_shape=jax.ShapeDtypeStruct((m, n), F32),
            compiler_params=_cparams(("parallel", "arbitrary")),
            name="matmul_res",
        )(x, w, res)
    nk = kdim // tk
    return pl.pallas_call(
        functools.partial(_mm_res_acc_kernel, alpha=alpha, nk=nk),
        grid=(m // tm, n // tn, nk),
        in_specs=[pl.BlockSpec((tm, tk), lambda i, j, k: (i, k)),
                  _w_spec(w, widx, tk, tn, 0, True),
                  pl.BlockSpec((tm, tn), lambda i, j, k: (i, j))],
        out_specs=pl.BlockSpec((tm, tn), lambda i, j, k: (i, j)),
        out_shape=jax.ShapeDtypeStruct((m, n), F32),
        scratch_shapes=[pltpu.VMEM((tm, tn), F32)],
        compiler_params=_cparams(("parallel", "arbitrary", "arbitrary")),
        name="matmul_res_acc",
    )(x, w, res)


def _swiglu_kernel(x_ref, wg_ref, wu_ref, o_ref):
    x = x_ref[...]
    g = jnp.dot(x, wg_ref[...], preferred_element_type=F32)
    u = jnp.dot(x, wu_ref[...], preferred_element_type=F32)
    o_ref[...] = (_silu(g) * u).astype(o_ref.dtype)


def _swiglu_in(x, w, l, *, tm=1024, tn=512):
    m, kdim = x.shape
    f = w.shape[-1] // 2
    tm = min(tm, m)
    nb = f // tn
    return pl.pallas_call(
        _swiglu_kernel,
        grid=(m // tm, nb),
        in_specs=[pl.BlockSpec((tm, kdim), lambda i, j: (i, 0)),
                  pl.BlockSpec((None, kdim, tn), lambda i, j: (l, 0, j)),
                  pl.BlockSpec((None, kdim, tn), lambda i, j: (l, 0, j + nb))],
        out_specs=pl.BlockSpec((tm, tn), lambda i, j: (i, j)),
        out_shape=jax.ShapeDtypeStruct((m, f), BF16),
        compiler_params=_cparams(("parallel", "arbitrary")),
        name="swiglu_in",
    )(x, w, w)


def _ffn(x, g, w_in, w_out, l):
    h = _rmsnorm(x, g, BF16)
    act = _swiglu_in(h, w_in, l)
    return _matmul_res(act, w_out, (l,), x, 0.5, tm=1024, tn=1024, tk=2048)


def _pool_kernel(a_ref, w_ref, s_ref, o_ref, pad_ref, *, seq, rows):
    g = pl.program_id(1)
    pad_ref[0:POOL_PAD, :] = jnp.zeros((POOL_PAD, POOL_GW), F32)
    pad_ref[POOL_PAD:POOL_PAD + seq, :] = a_ref[...].astype(F32)
    w = w_ref[...]
    scale = s_ref[...]
    for gi, win in enumerate(POOL_WINDOWS):

        @pl.when(g == gi)
        def _(win=win):
            for c in range(seq // rows):
                r0 = POOL_PAD + c * rows
                cur = pad_ref[r0:r0 + rows, :]
                tot = cur
                for k in range(1, win):
                    tot = tot + pad_ref[r0 - k:r0 - k + rows, :]
                t1 = lax.broadcasted_iota(jnp.int32, (rows, POOL_GW), 0) + (c * rows + 1)
                cnt = jnp.minimum(t1, win).astype(F32)
                mixed = (tot / cnt - cur).astype(BF16)
                y = jnp.dot(mixed, w, preferred_element_type=F32) * scale
                o_ref[c * rows:(c + 1) * rows, :] = y.astype(o_ref.dtype)


def _pool_mixer(seg_a, pool_w, pool_scale, l, bsz, seq):
    rows = min(256, seq)
    return pl.pallas_call(
        functools.partial(_pool_kernel, seq=seq, rows=rows),
        grid=(bsz, POOL_GROUPS),
        in_specs=[pl.BlockSpec((seq, POOL_GW), lambda b, g: (b, g)),
                  pl.BlockSpec((None, None, POOL_GW, POOL_GW), lambda b, g: (l, g, 0, 0)),
                  pl.BlockSpec((None, None, 1, POOL_GW), lambda b, g: (l, g, 0, 0))],
        out_specs=pl.BlockSpec((seq, POOL_GW), lambda b, g: (b, g)),
        out_shape=jax.ShapeDtypeStruct((bsz * seq, POOL_WIDTH), BF16),
        scratch_shapes=[pltpu.VMEM((POOL_PAD + seq, POOL_GW), F32)],
        compiler_params=_cparams(("parallel", "arbitrary")),
        name="pool_mixer",
    )(seg_a, pool_w, pool_scale.reshape(pool_scale.shape[0], POOL_GROUPS, 1, POOL_GW))


def _sg_kernel(u_ref, v_ref, g_ref, b_ref, w_ref, bias_ref, o_ref, *, nblk):
    ri = lax.broadcasted_iota(jnp.int32, (SG_BLOCK, SG_BLOCK), 0) // CHUNK
    ci = lax.broadcasted_iota(jnp.int32, (SG_BLOCK, SG_BLOCK), 1) // CHUNK
    causal = ri >= ci
    wm = [jnp.where(causal, w_ref[gi], 0.0).astype(BF16) for gi in range(SG_GROUPS)]
    bias = bias_ref[...]
    for n in range(nblk):
        rs = slice(n * SG_BLOCK, (n + 1) * SG_BLOCK)
        v = _gelu(v_ref[rs, :].astype(F32))
        mu = jnp.mean(v, axis=-1, keepdims=True)
        vc = v - mu
        var = jnp.mean(vc * vc, axis=-1, keepdims=True)
        vn = (vc * lax.rsqrt(var + EPS) * g_ref[...] + b_ref[...]).astype(BF16)
        u = _gelu(u_ref[rs, :].astype(F32))
        for gi in range(SG_GROUPS):
            cs = slice(gi * SG_GW, (gi + 1) * SG_GW)
            sv = jnp.dot(wm[gi], vn[:, cs], preferred_element_type=F32) + bias[:, cs]
            o_ref[rs, cs] = (u[:, cs] * sv).astype(o_ref.dtype)


def _sg_mixer(seg_a, ln_g, ln_b, sg_w, bias_tile, l, m):
    tb = min(512, m)
    ub, vb = C_U // SG_WIDTH, C_V // SG_WIDTH
    return pl.pallas_call(
        functools.partial(_sg_kernel, nblk=tb // SG_BLOCK),
        grid=(m // tb,),
        in_specs=[pl.BlockSpec((tb, SG_WIDTH), lambda i: (i, ub)),
                  pl.BlockSpec((tb, SG_WIDTH), lambda i: (i, vb)),
                  pl.BlockSpec((None, 1, SG_WIDTH), lambda i: (l, 0, 0)),
                  pl.BlockSpec((None, 1, SG_WIDTH), lambda i: (l, 0, 0)),
                  pl.BlockSpec((None, SG_GROUPS, SG_BLOCK, SG_BLOCK), lambda i: (l, 0, 0, 0)),
                  pl.BlockSpec((None, SG_BLOCK, SG_WIDTH), lambda i: (l, 0, 0))],
        out_specs=pl.BlockSpec((tb, SG_WIDTH), lambda i: (i, 0)),
        out_shape=jax.ShapeDtypeStruct((m, SG_WIDTH), BF16),
        compiler_params=_cparams(("parallel",)),
        name="sg_mixer",
    )(seg_a, seg_a, ln_g.reshape(-1, 1, SG_WIDTH), ln_b.reshape(-1, 1, SG_WIDTH), sg_w, bias_tile)


def _ssd_kernel(z_ref, xc_ref, xp_ref, bc_ref, bp_ref, dt_ref,
                cwx_ref, cwb_ref, cbx_ref, cbb_ref, dtb_ref, aexp_ref, dexp_ref, ng_ref, e_ref,
                o_ref, st_ref):
    c = pl.program_id(1)

    @pl.when(c == 0)
    def _():
        st_ref[...] = jnp.zeros(st_ref.shape, F32)

    has_prev = c > 0
    srow = lax.broadcasted_iota(jnp.int32, (3 * CHUNK, 2 * CHUNK), 0)
    scol = lax.broadcasted_iota(jnp.int32, (3 * CHUNK, 2 * CHUNK), 1)
    shift = jnp.where(scol == CHUNK + (srow % CHUNK) - (3 - srow // CHUNK), 1.0, 0.0).astype(BF16)

    def conv(cur_ref, prev_ref, w_ref, b_ref):
        cur = cur_ref[...]
        prev = jnp.where(has_prev, prev_ref[...], jnp.zeros_like(cur))
        both = jnp.concatenate([prev, cur], axis=0)
        sh = jnp.dot(shift, both, preferred_element_type=F32)
        w = w_ref[...]
        acc = cur.astype(F32) * w[3:4, :] + b_ref[...]
        for k in range(SSM_CONV - 1):
            acc = acc + sh[k * CHUNK:(k + 1) * CHUNK, :] * w[k:k + 1, :]
        return _silu(acc)

    xs = conv(xc_ref, xp_ref, cwx_ref, cbx_ref)
    bcv = conv(bc_ref, bp_ref, cwb_ref, cbb_ref)
    gn = SSM_GROUPS * SSM_STATE
    bm = bcv[:, :gn].astype(BF16)
    cm = bcv[:, gn:].astype(BF16)

    hi = lax.Precision.HIGHEST
    lane = lax.broadcasted_iota(jnp.int32, (CHUNK, LANE), 1)
    dt = jnp.where(lane < SSM_HEADS, _softplus(dt_ref[...] + dtb_ref[...]), 0.0)
    dt_e = jnp.dot(dt, e_ref[...], precision=hi, preferred_element_type=F32)
    a_e = dt_e * aexp_ref[...]
    r64 = lax.broadcasted_iota(jnp.int32, (CHUNK, CHUNK), 0)
    c64 = lax.broadcasted_iota(jnp.int32, (CHUNK, CHUNK), 1)
    tri = jnp.where(c64 <= r64, 1.0, 0.0).astype(F32)
    a_cs = jnp.dot(tri, a_e, precision=hi, preferred_element_type=F32)
    rl = lax.broadcasted_iota(jnp.int32, (CHUNK, SSM_INNER), 0)
    cl = lax.broadcasted_iota(jnp.int32, (CHUNK, SSM_INNER), 1) % SSM_HEADDIM
    diag = jnp.where(rl == cl, a_cs, 0.0)
    ones = jnp.ones((CHUNK, CHUNK), F32)
    a_row = jnp.dot(ones, diag, precision=hi, preferred_element_type=F32)
    decay = jnp.exp(jnp.where(rl >= cl, a_cs - a_row, NEG_BIG))
    a_last = a_cs[CHUNK - 1:CHUNK, :]
    xd = xs * dt_e
    xe = (xd * jnp.exp(a_last - a_cs)).astype(BF16)
    xdb = xd.astype(BF16)
    ea = jnp.exp(a_cs)
    cdec = jnp.exp(a_last)

    br = lax.broadcasted_iota(jnp.int32, (SSM_GW, SSM_GW), 0) // SSM_HEADDIM
    bc_ = lax.broadcasted_iota(jnp.int32, (SSM_GW, SSM_GW), 1) // SSM_HEADDIM
    blockdiag = br == bc_
    nt = (((1,), (1,)), ((), ()))
    tn = (((0,), (0,)), ((), ()))
    ys = []
    for g in range(SSM_GROUPS):
        ns = slice(g * SSM_STATE, (g + 1) * SSM_STATE)
        ls = slice(g * SSM_GW, (g + 1) * SSM_GW)
        cg, bg = cm[:, ns], bm[:, ns]
        b_t = jnp.concatenate([bg] * SSM_HG, axis=0)
        cb = lax.dot_general(cg, b_t, nt, preferred_element_type=F32)
        mg = (cb * decay[:, ls]).astype(BF16)
        xg = xdb[:, ls]
        bd = jnp.where(blockdiag, jnp.concatenate([xg] * SSM_HG, axis=0), jnp.zeros((), BF16))
        y_diag = jnp.dot(mg, bd, preferred_element_type=F32)
        st = st_ref[g]
        y_off = jnp.dot(cg, st.astype(BF16), preferred_element_type=F32) * ea[:, ls]
        upd = lax.dot_general(bg, xe[:, ls], tn, preferred_element_type=F32)
        st_ref[g] = st * cdec[:, ls] + upd
        ys.append(y_diag + y_off)
    y = jnp.concatenate(ys, axis=1) + xs * dexp_ref[...]
    y = y * _silu(z_ref[...].astype(F32))
    outs = []
    for g in range(SSM_GROUPS):
        yg = y[:, g * SSM_GW:(g + 1) * SSM_GW]
        ms = jnp.mean(yg * yg, axis=-1, keepdims=True)
        outs.append(yg * lax.rsqrt(ms + EPS))
    o_ref[...] = (jnp.concatenate(outs, axis=1) * ng_ref[...]).astype(o_ref.dtype)


def _ssd_mixer(seg_a, seg_s, conv_w, conv_b, dtb_pad, a_exp, d_exp, norm_g, e_mat, l, bsz, seq):
    nc = seq // CHUNK
    zb, xb, bb = C_Z // SSM_INNER, C_XBC // SSM_INNER, C_XBC // SSM_INNER + 1

    def row(b, c):
        return b * nc + c

    def prow(b, c):
        return b * nc + jnp.maximum(c - 1, 0)

    vec = lambda blk: pl.BlockSpec((None, 1, SSM_INNER), lambda b, c: (l, 0, blk))
    return pl.pallas_call(
        _ssd_kernel,
        grid=(bsz, nc),
        in_specs=[pl.BlockSpec((CHUNK, SSM_INNER), lambda b, c: (row(b, c), zb)),
                  pl.BlockSpec((CHUNK, SSM_INNER), lambda b, c: (row(b, c), xb)),
                  pl.BlockSpec((CHUNK, SSM_INNER), lambda b, c: (prow(b, c), xb)),
                  pl.BlockSpec((CHUNK, SSM_INNER), lambda b, c: (row(b, c), bb)),
                  pl.BlockSpec((CHUNK, SSM_INNER), lambda b, c: (prow(b, c), bb)),
                  pl.BlockSpec((CHUNK, LANE), lambda b, c: (row(b, c), 1)),
                  pl.BlockSpec((None, SSM_CONV, SSM_INNER), lambda b, c: (l, 0, 0)),
                  pl.BlockSpec((None, SSM_CONV, SSM_INNER), lambda b, c: (l, 0, 1)),
                  vec(0), vec(1),
                  pl.BlockSpec((None, 1, LANE), lambda b, c: (l, 0, 0)),
                  vec(0), vec(0), vec(0),
                  pl.BlockSpec((LANE, SSM_INNER), lambda b, c: (0, 0))],
        out_specs=pl.BlockSpec((CHUNK, SSM_INNER), lambda b, c: (row(b, c), 0)),
        out_shape=jax.ShapeDtypeStruct((bsz * seq, SSM_INNER), BF16),
        scratch_shapes=[pltpu.VMEM((SSM_GROUPS, SSM_STATE, SSM_GW), F32)],
        compiler_params=_cparams(("parallel", "arbitrary")),
        name="ssd_mixer",
    )(seg_a, seg_a, seg_a, seg_a, seg_a, seg_s, conv_w, conv_w,
      conv_b.reshape(-1, 1, 2 * SSM_INNER), conv_b.reshape(-1, 1, 2 * SSM_INNER),
      dtb_pad, a_exp, d_exp, norm_g.reshape(-1, 1, SSM_INNER), e_mat)


def _rope128(x, cos, sin_signed):
    return x * cos + pltpu.roll(x, ATT_HEADDIM // 2, 1) * sin_signed


def _rope64(x, cos, sin_signed):
    lane = lax.broadcasted_iota(jnp.int32, x.shape, 1)
    low = (lane % IDX_HEADDIM) < IDX_HEADDIM // 2
    rot = jnp.where(low, pltpu.roll(x, LANE - IDX_HEADDIM // 2, 1), pltpu.roll(x, IDX_HEADDIM // 2, 1))
    return x * cos + rot * sin_signed


def _dsa_kernel(q_ref, k_ref, v_ref, qi_ref, ki_ref, wi_ref,
                cq_ref, sq_ref, ck_ref, sk_ref, ciq_ref, siq_ref, cik_ref, sik_ref,
                o_ref, kr_ref, vb_ref, kir_ref, key_ref, thr_ref, pos_ref, *, seq, topk):
    i = pl.program_id(1)
    nt = (((1,), (1,)), ((), ()))
    nlc = seq // LANE

    @pl.when(i == 0)
    def _():
        for kv in range(ATT_KV_HEADS):
            hs = slice(kv * ATT_HEADDIM, (kv + 1) * ATT_HEADDIM)
            kr_ref[:, hs] = _rope128(k_ref[:, hs], ck_ref[...], sk_ref[...]).astype(BF16)
        vb_ref[...] = v_ref[...].astype(BF16)
        kir_ref[...] = _rope64(ki_ref[...], cik_ref[...], sik_ref[...])

    lane_q = lax.broadcasted_iota(jnp.int32, (Q_BLOCK, LANE), 1)
    kir = kir_ref[...]
    wi = wi_ref[...] * np.float32(IDX_HEADS ** -0.5)
    iscore = jnp.zeros((Q_BLOCK, seq), F32)
    for pair in range(IDX_HEADS // 2):
        ps = slice(pair * LANE, (pair + 1) * LANE)
        qp = _rope64(qi_ref[:, ps], ciq_ref[...], siq_ref[...])
        for sub in range(2):
            h = 2 * pair + sub
            in_head = (lane_q >= sub * IDX_HEADDIM) & (lane_q < (sub + 1) * IDX_HEADDIM)
            qh = jnp.where(in_head, qp, 0.0)
            logit = lax.dot_general(qh, kir, nt, precision=lax.Precision.HIGHEST,
                                    preferred_element_type=F32)
            wcol = wi[:, S_WI_LANE + h:S_WI_LANE + h + 1]
            iscore = iscore + jnp.maximum(logit, 0.0) * wcol

    qchunk = (lax.broadcasted_iota(jnp.int32, (Q_BLOCK, seq), 0) + i * Q_BLOCK) // CHUNK
    kpos = lax.broadcasted_iota(jnp.int32, (Q_BLOCK, seq), 1)
    adm = (kpos // CHUNK) <= qchunk
    iscore = jnp.where(iscore == 0.0, 0.0, iscore)
    bits = pltpu.bitcast(iscore, jnp.int32)
    key = jnp.where(bits < 0, bits ^ jnp.int32(0x7FFFFFFF), bits)
    key = jnp.maximum(key, jnp.int32(INT_MIN + 1))
    key_ref[...] = jnp.where(adm, key, jnp.int32(INT_MIN))

    def count_ge(cand):
        acc = jnp.zeros((Q_BLOCK, LANE), F32)
        for cidx in range(nlc):
            kc = key_ref[:, cidx * LANE:(cidx + 1) * LANE]
            acc = acc + jnp.where(kc >= cand, 1.0, 0.0)
        return jnp.broadcast_to(jnp.sum(acc, axis=-1, keepdims=True), (Q_BLOCK, LANE))

    def thr_step(it, t_u):
        bit = jnp.left_shift(jnp.int32(1), 31 - it)
        cand_u = t_u | bit
        cnt = count_ge(cand_u ^ jnp.int32(INT_MIN))
        return jnp.where(cnt >= topk, cand_u, t_u)

    t_u = lax.fori_loop(0, 32, thr_step, jnp.zeros((Q_BLOCK, LANE), jnp.int32))
    thr = t_u ^ jnp.int32(INT_MIN)
    thr_ref[...] = thr
    cnt_ge = count_ge(thr)
    cnt_gt = jnp.zeros((Q_BLOCK, LANE), F32)
    for cidx in range(nlc):
        kc = key_ref[:, cidx * LANE:(cidx + 1) * LANE]
        cnt_gt = cnt_gt + jnp.where(kc > thr, 1.0, 0.0)
    cnt_gt = jnp.broadcast_to(jnp.sum(cnt_gt, axis=-1, keepdims=True), (Q_BLOCK, LANE))
    need = topk - cnt_gt
    excess = jnp.where(thr > INT_MIN, cnt_ge - topk, 0.0)
    pos_ref[...] = jnp.full((Q_BLOCK, LANE), seq, jnp.int32)

    @pl.when(jnp.max(excess) > 0.0)
    def _():
        def count_ties_below(bound):
            acc = jnp.zeros((Q_BLOCK, LANE), F32)
            for cidx in range(nlc):
                kc = key_ref[:, cidx * LANE:(cidx + 1) * LANE]
                pc = lax.broadcasted_iota(jnp.int32, (Q_BLOCK, LANE), 1) + cidx * LANE
                acc = acc + jnp.where((kc == thr) & (pc < bound), 1.0, 0.0)
            return jnp.broadcast_to(jnp.sum(acc, axis=-1, keepdims=True), (Q_BLOCK, LANE))

        nbits = int(seq - 1).bit_length()

        def pos_step(it, bound):
            cand = bound | jnp.left_shift(jnp.int32(1), nbits - 1 - it)
            return jnp.where(count_ties_below(cand) < need, cand, bound)

        bound = lax.fori_loop(0, nbits, pos_step, jnp.zeros((Q_BLOCK, LANE), jnp.int32))
        pos_ref[...] = bound + 1

    scale = np.float32(ATT_HEADDIM ** -0.5)
    for kv in range(ATT_KV_HEADS):
        hs = slice(kv * ATT_HEADDIM, (kv + 1) * ATT_HEADDIM)
        krh = kr_ref[:, hs]
        vh = vb_ref[:, hs]
        for gq in range(ATT_GRP):
            h = kv * ATT_GRP + gq
            qs = slice(h * ATT_HEADDIM, (h + 1) * ATT_HEADDIM)
            qh = _rope128(q_ref[:, qs], cq_ref[...], sq_ref[...]).astype(BF16)
            s = lax.dot_general(qh, krh, nt, preferred_element_type=F32) * scale
            keyv = key_ref[...]
            thr_b = jnp.concatenate([thr_ref[...]] * nlc, axis=1)
            pos_b = jnp.concatenate([pos_ref[...]] * nlc, axis=1)
            sel = (keyv > thr_b) | ((keyv == thr_b) & (kpos < pos_b))
            sel = sel & (keyv > INT_MIN)
            s = jnp.where(sel, s, NEG_BIG)
            mx = jnp.max(s, axis=-1, keepdims=True)
            e = jnp.exp(s - mx)
            den = jnp.sum(e, axis=-1, keepdims=True)
            o = jnp.dot(e.astype(BF16), vh, preferred_element_type=F32)
            o_ref[:, qs] = (o / den).astype(o_ref.dtype)


def _dsa_mixer(seg_q, seg_s, tabs, bsz, seq):
    nqb = seq // Q_BLOCK
    topk = min(IDX_TOPK, seq // 4)
    cos128, sin128, cos64, sin64 = tabs
    kb, vb_, qib = 1024 // 256, 1280 // 256, 1536 // 512
    qtab = pl.BlockSpec((Q_BLOCK, LANE), lambda b, i: (i, 0))
    ktab = pl.BlockSpec((seq, LANE), lambda b, i: (0, 0))
    return pl.pallas_call(
        functools.partial(_dsa_kernel, seq=seq, topk=topk),
        grid=(bsz, nqb),
        in_specs=[pl.BlockSpec((Q_BLOCK, ATT_HEADS * ATT_HEADDIM), lambda b, i: (b * nqb + i, 0)),
                  pl.BlockSpec((seq, ATT_KV_HEADS * ATT_HEADDIM), lambda b, i: (b, kb)),
                  pl.BlockSpec((seq, ATT_KV_HEADS * ATT_HEADDIM), lambda b, i: (b, vb_)),
                  pl.BlockSpec((Q_BLOCK, IDX_HEADS * IDX_HEADDIM), lambda b, i: (b * nqb + i, qib)),
                  pl.BlockSpec((seq, LANE), lambda b, i: (b, 0)),
                  pl.BlockSpec((Q_BLOCK, LANE), lambda b, i: (b * nqb + i, 1)),
                  qtab, qtab, ktab, ktab, qtab, qtab, ktab, ktab],
        out_specs=pl.BlockSpec((Q_BLOCK, ATT_HEADS * ATT_HEADDIM), lambda b, i: (b * nqb + i, 0)),
        out_shape=jax.ShapeDtypeStruct((bsz * seq, ATT_HEADS * ATT_HEADDIM), BF16),
        scratch_shapes=[pltpu.VMEM((seq, ATT_KV_HEADS * ATT_HEADDIM), BF16),
                        pltpu.VMEM((seq, ATT_KV_HEADS * ATT_HEADDIM), BF16),
                        pltpu.VMEM((seq, LANE), F32),
                        pltpu.VMEM((Q_BLOCK, seq), jnp.int32),
                        pltpu.VMEM((Q_BLOCK, LANE), jnp.int32),
                        pltpu.VMEM((Q_BLOCK, LANE), jnp.int32)],
        compiler_params=_cparams(("parallel", "arbitrary")),
        name="dsa_mixer",
    )(seg_q, seg_q, seg_q, seg_q, seg_s, seg_s,
      cos128, sin128, cos128, sin128, cos64, sin64, cos64, sin64)


def _rope_tables(seq):
    pos = jnp.arange(seq, dtype=F32)[:, None]

    def tab(half, reps):
        inv = ROPE_THETA ** (-jnp.arange(half, dtype=F32) / half)
        ang = pos * inv[None, :]
        cos, sin = jnp.cos(ang), jnp.sin(ang)
        return (jnp.tile(jnp.concatenate([cos, cos], axis=1), (1, reps)),
                jnp.tile(jnp.concatenate([-sin, sin], axis=1), (1, reps)))

    cos128, sin128 = tab(ATT_HEADDIM // 2, 1)
    cos64, sin64 = tab(IDX_HEADDIM // 2, 2)
    return cos128, sin128, cos64, sin64


MERGE_STEP = 1024
MERGE_STEPS = (POOL_WIDTH + SG_WIDTH + SSM_INNER + ATT_HEADS * ATT_HEADDIM) // MERGE_STEP


def _merge_kernel(h_ref, wg_ref, y_ref, p_ref, o_ref, gate_ref, acc_ref):
    s = pl.program_id(2)

    @pl.when(s != 1)
    def _():
        gate_ref[...] = _sigmoid(jnp.dot(h_ref[...], wg_ref[...], preferred_element_type=F32))

    term = gate_ref[...] * jnp.dot(y_ref[...], p_ref[...], preferred_element_type=F32)

    @pl.when(s == 0)
    def _():
        acc_ref[...] = term

    @pl.when(s > 0)
    def _():
        acc_ref[...] += term

    @pl.when(s == MERGE_STEPS - 1)
    def _():
        o_ref[...] = acc_ref[...].astype(o_ref.dtype)


def _gated_merge(h, y_all, w_gate, w_branch, l, *, tm=1024, tn=512):
    m, d = h.shape
    tm = min(tm, m)
    return pl.pallas_call(
        _merge_kernel,
        grid=(m // tm, d // tn, MERGE_STEPS),
        in_specs=[pl.BlockSpec((tm, d), lambda i, j, s: (i, 0)),
                  pl.BlockSpec((None, None, d, tn), lambda i, j, s: (l, jnp.maximum(s - 1, 0), 0, j)),
                  pl.BlockSpec((tm, MERGE_STEP), lambda i, j, s: (i, s)),
                  pl.BlockSpec((None, MERGE_STEP, tn), lambda i, j, s: (l, s, j))],
        out_specs=pl.BlockSpec((tm, tn), lambda i, j, s: (i, j)),
        out_shape=jax.ShapeDtypeStruct((m, d), BF16),
        scratch_shapes=[pltpu.VMEM((tm, tn), F32), pltpu.VMEM((tm, tn), F32)],
        compiler_params=_cparams(("parallel", "arbitrary", "arbitrary")),
        name="gated_merge",
    )(h, w_gate, y_all, w_branch)


def _xattn_kernel(q_ref, kv_ref, o_ref):
    nt = (((1,), (1,)), ((), ()))
    scale = np.float32(MEM_HEADDIM ** -0.5)
    hw = MEM_HEADS * MEM_HEADDIM
    for h in range(MEM_HEADS):
        hs = slice(h * MEM_HEADDIM, (h + 1) * MEM_HEADDIM)
        s = lax.dot_general(q_ref[:, hs], kv_ref[:, hs], nt, preferred_element_type=F32) * scale
        mx = jnp.max(s, axis=-1, keepdims=True)
        e = jnp.exp(s - mx)
        den = jnp.sum(e, axis=-1, keepdims=True)
        vs = slice(hw + h * MEM_HEADDIM, hw + (h + 1) * MEM_HEADDIM)
        o = jnp.dot(e.astype(BF16), kv_ref[:, vs], preferred_element_type=F32)
        o_ref[:, hs] = (o / den).astype(o_ref.dtype)


def _xattn(q, kv, bsz, seq, mem_len):
    tq = min(512, seq)
    nq = seq // tq
    hw = MEM_HEADS * MEM_HEADDIM
    return pl.pallas_call(
        _xattn_kernel,
        grid=(bsz, nq),
        in_specs=[pl.BlockSpec((tq, hw), lambda b, i: (b * nq + i, 0)),
                  pl.BlockSpec((mem_len, 2 * hw), lambda b, i: (b, 0))],
        out_specs=pl.BlockSpec((tq, hw), lambda b, i: (b * nq + i, 0)),
        out_shape=jax.ShapeDtypeStruct((bsz * seq, hw), BF16),
        compiler_params=_cparams(("parallel", "arbitrary")),
        name="mem_xattn",
    )(q, kv)


def kernel(x, mem, g_ffn1, w_ffn1_in, w_ffn1_out, g_mix, w_in, pool_w, pool_scale, sg_ln_g, sg_ln_b, sg_w, sg_b, ssm_conv_w, ssm_conv_b, ssm_a_log, ssm_dt_bias, ssm_d, ssm_norm_g, w_branch, w_gate, w_out, g_mem, g_cross, w_mem_q, w_mem_kv, w_mem_o, g_ffn2, w_ffn2_in, w_ffn2_out, g_final):
    bsz, seq, d = x.shape
    mem_len = mem.shape[1]
    depth = w_in.shape[0]
    m = bsz * seq
    bf = lambda a: a.astype(BF16)

    w1i, w1o, w2i, w2o = bf(w_ffn1_in), bf(w_ffn1_out), bf(w_ffn2_in), bf(w_ffn2_out)
    w_a = bf(w_in[:, :, :SEG_A])
    w_q = bf(w_in[:, :, C_Q:C_KI])
    w_ki = w_in[:, :, C_KI:C_WI]
    w_s = bf(jnp.concatenate(
        [w_ki, w_ki, w_in[:, :, C_DT:C_Q], w_in[:, :, C_WI:],
         jnp.zeros((depth, d, LANE - SSM_HEADS - IDX_HEADS), F32)], axis=2))
    wg, wb, wo = bf(w_gate), bf(w_branch), bf(w_out)
    wmq, wmkv, wmo = bf(w_mem_q), bf(w_mem_kv), bf(w_mem_o)
    pw = bf(pool_w)

    expand = lambda v: jnp.repeat(v, SSM_HEADDIM, axis=-1).reshape(depth, 1, SSM_INNER)
    a_exp = expand(-jnp.exp(ssm_a_log))
    d_exp = expand(ssm_d)
    dtb_pad = jnp.pad(ssm_dt_bias, ((0, 0), (0, LANE - SSM_HEADS))).reshape(depth, 1, LANE)
    e_mat = (jnp.arange(LANE)[:, None] == (jnp.arange(SSM_INNER)[None, :] // SSM_HEADDIM)).astype(F32)
    bias_tile = jnp.repeat(jnp.swapaxes(sg_b, 1, 2), SG_GW, axis=2)
    tabs = _rope_tables(seq)

    x2 = x.reshape(m, d)
    mem_n = _rmsnorm(mem.reshape(bsz * mem_len, d), g_mem, BF16)

    for l in range(depth):
        x2 = _ffn(x2, g_ffn1[l], w1i, w1o, l)

        h = _rmsnorm(x2, g_mix[l], BF16)
        seg_a = _matmul(h, w_a, (l,), SEG_A, tn=1024, out_dtype=BF16)
        seg_q = _matmul(h, w_q, (l,), SEG_Q, tn=1024, out_dtype=F32)
        seg_s = _matmul(h, w_s, (l,), SEG_S, tn=256, out_dtype=F32)
        y_a = _pool_mixer(seg_a, pw, pool_scale, l, bsz, seq)
        y_b = _sg_mixer(seg_a, sg_ln_g, sg_ln_b, sg_w, bias_tile, l, m)
        y_c = _ssd_mixer(seg_a, seg_s, ssm_conv_w, ssm_conv_b, dtb_pad, a_exp, d_exp,
                         ssm_norm_g, e_mat, l, bsz, seq)
        y_d = _dsa_mixer(seg_q, seg_s, tabs, bsz, seq)
        y_all = jnp.concatenate([y_a, y_b, y_c, y_d], axis=1)
        merged = _gated_merge(h, y_all, wg, wb, l)
        x2 = _matmul_res(merged, wo, (l,), x2, 1.0)

        hc = _rmsnorm(x2, g_cross[l], BF16)
        q = _matmul(hc, wmq, (l,), MEM_HEADS * MEM_HEADDIM, out_dtype=BF16)
        kv = _matmul(mem_n, wmkv, (l,), 2 * MEM_HEADS * MEM_HEADDIM, out_dtype=BF16)
        att = _xattn(q, kv, bsz, seq, mem_len)
        x2 = _matmul_res(att, wmo, (l,), x2, 1.0, tn=1024)

        x2 = _ffn(x2, g_ffn2[l], w2i, w2o, l)

    return _rmsnorm(x2, g_final, F32).reshape(bsz, seq, d)
```

```python
import functools

import jax
import jax.numpy as jnp
import numpy as np
from jax import lax
from jax.experimental import pallas as pl
from jax.experimental.pallas import tpu as pltpu

F32 = jnp.float32
BF16 = jnp.bfloat16

D_MODEL = 4096
FFN_DIM = 8192
CHUNK = 64
EPS = 1e-6
ROPE_THETA = 10000.0

POOL_WINDOWS = (2, 4, 8, 16)
POOL_GROUPS = 4
POOL_WIDTH = 2048
POOL_GW = POOL_WIDTH // POOL_GROUPS
POOL_PAD = 16

SG_WIDTH = 1024
SG_BLOCK = 128
SG_GROUPS = 4
SG_GW = SG_WIDTH // SG_GROUPS

SSM_HEADS = 16
SSM_HEADDIM = 64
SSM_INNER = SSM_HEADS * SSM_HEADDIM
SSM_GROUPS = 4
SSM_STATE = 128
SSM_CONV = 4
SSM_HG = SSM_HEADS // SSM_GROUPS
SSM_GW = SSM_INNER // SSM_GROUPS

ATT_HEADS = 8
ATT_KV_HEADS = 2
ATT_HEADDIM = 128
ATT_GRP = ATT_HEADS // ATT_KV_HEADS
IDX_HEADS = 8
IDX_HEADDIM = 64
IDX_TOPK = 256
Q_BLOCK = 128

MEM_HEADS = 4
MEM_HEADDIM = 128

C_POOL, C_U, C_V, C_Z, C_XBC = 0, 2048, 3072, 4096, 5120
C_DT, C_Q, C_K, C_VAL, C_QI, C_KI, C_WI = 7168, 7184, 8208, 8464, 8720, 9232, 9296
SEG_A = 7168
SEG_Q = 2048
SEG_S = 256
S_DT_LANE = 0
S_WI_LANE = 16

LANE = 128
VMEM_LIMIT = 56 * 1024 * 1024
NEG_BIG = -1e30
INT_MIN = -2147483648


def _cparams(sem):
    return pltpu.CompilerParams(dimension_semantics=sem, vmem_limit_bytes=VMEM_LIMIT)


def _sigmoid(x):
    return 1.0 / (1.0 + jnp.exp(-x))


def _silu(x):
    return x * _sigmoid(x)


def _gelu(x):
    return 0.5 * x * (1.0 + lax.erf(x * np.float32(1.0 / np.sqrt(2.0))))


def _softplus(x):
    return jnp.maximum(x, 0.0) + jnp.log1p(jnp.exp(-jnp.abs(x)))


def _rmsnorm_kernel(x_ref, g_ref, o_ref):
    x = x_ref[...]
    ms = jnp.mean(x * x, axis=-1, keepdims=True)
    o_ref[...] = (x * lax.rsqrt(ms + EPS) * g_ref[...]).astype(o_ref.dtype)


def _rmsnorm(x, g, out_dtype):
    m, d = x.shape
    tm = min(512, m)
    return pl.pallas_call(
        _rmsnorm_kernel,
        grid=(m // tm,),
        in_specs=[pl.BlockSpec((tm, d), lambda i: (i, 0)),
                  pl.BlockSpec((1, d), lambda i: (0, 0))],
        out_specs=pl.BlockSpec((tm, d), lambda i: (i, 0)),
        out_shape=jax.ShapeDtypeStruct((m, d), out_dtype),
        compiler_params=_cparams(("parallel",)),
        name="rmsnorm",
    )(x, g.reshape(1, d))


def _mm_kernel(x_ref, w_ref, o_ref):
    o_ref[...] = jnp.dot(x_ref[...], w_ref[...].astype(BF16), preferred_element_type=F32).astype(o_ref.dtype)


def _mm_res_kernel(x_ref, w_ref, r_ref, o_ref, *, alpha):
    acc = jnp.dot(x_ref[...], w_ref[...].astype(BF16), preferred_element_type=F32)
    o_ref[...] = r_ref[...] + alpha * acc


def _mm_res_acc_kernel(x_ref, w_ref, r_ref, o_ref, acc_ref, *, alpha, nk):
    k = pl.program_id(2)
    part = jnp.dot(x_ref[...], w_ref[...].astype(BF16), preferred_element_type=F32)

    @pl.when(k == 0)
    def _():
        acc_ref[...] = part

    @pl.when(k > 0)
    def _():
        acc_ref[...] += part

    @pl.when(k == nk - 1)
    def _():
        o_ref[...] = r_ref[...] + alpha * acc_ref[...]


def _w_spec(w, widx, kblk, tn, col_blk0, kgrid):
    lead = (None,) * len(widx)
    if kgrid:
        return pl.BlockSpec(lead + (kblk, tn), lambda i, j, k: (*widx, k, j + col_blk0))
    return pl.BlockSpec(lead + (kblk, tn), lambda i, j: (*widx, 0, j + col_blk0))


def _matmul(x, w, widx, n, *, col0=0, tm=1024, tn=512, out_dtype=BF16):
    m, kdim = x.shape
    tm, tn = min(tm, m), min(tn, n)
    return pl.pallas_call(
        _mm_kernel,
        grid=(m // tm, n // tn),
        in_specs=[pl.BlockSpec((tm, kdim), lambda i, j: (i, 0)),
                  _w_spec(w, widx, kdim, tn, col0 // tn, False)],
        out_specs=pl.BlockSpec((tm, tn), lambda i, j: (i, j)),
        out_shape=jax.ShapeDtypeStruct((m, n), out_dtype),
        compiler_params=_cparams(("parallel", "arbitrary")),
        name="matmul",
    )(x, w)


def _matmul_res(x, w, widx, res, alpha, *, tm=1024, tn=512, tk=None):
    m, kdim = x.shape
    n = res.shape[1]
    tm, tn = min(tm, m), min(tn, n)
    if tk is None or tk >= kdim:
        return pl.pallas_call(
            functools.partial(_mm_res_kernel, alpha=alpha),
            grid=(m // tm, n // tn),
            in_specs=[pl.BlockSpec((tm, kdim), lambda i, j: (i, 0)),
                      _w_spec(w, widx, kdim, tn, 0, False),
                      pl.BlockSpec((tm, tn), lambda i, j: (i, j))],
            out_specs=pl.BlockSpec((tm, tn), lambda i, j: (i, j)),
            out_shape=jax.ShapeDtypeStruct((m, n), F32),
            compiler_params=_cparams(("parallel", "arbitrary")),
            name="matmul_res",
        )(x, w, res)
    nk = kdim // tk
    return pl.pallas_call(
        functools.partial(_mm_res_acc_kernel, alpha=alpha, nk=nk),
        grid=(m // tm, n // tn, nk),
        in_specs=[pl.BlockSpec((tm, tk), lambda i, j, k: (i, k)),
                  _w_spec(w, widx, tk, tn, 0, True),
                  pl.BlockSpec((tm, tn), lambda i, j, k: (i, j))],
        out_specs=pl.BlockSpec((tm, tn), lambda i, j, k: (i, j)),
        out_shape=jax.ShapeDtypeStruct((m, n), F32),
        scratch_shapes=[pltpu.VMEM((tm, tn), F32)],
        compiler_params=_cparams(("parallel", "arbitrary", "arbitrary")),
        name="matmul_res_acc",
    )(x, w, res)


def _swiglu_kernel(x_ref, wg_ref, wu_ref, o_ref):
    x = x_ref[...]
    g = jnp.dot(x, wg_ref[...].astype(BF16), preferred_element_type=F32)
    u = jnp.dot(x, wu_ref[...].astype(BF16), preferred_element_type=F32)
    o_ref[...] = (_silu(g) * u).astype(o_ref.dtype)


def _swiglu_in(x, w, l, *, tm=1024):
    m, kdim = x.shape
    f = w.shape[-1] // 2
    tm = min(tm, m)
    tn = 256 if w.dtype == F32 else 512
    nb = f // tn
    return pl.pallas_call(
        _swiglu_kernel,
        grid=(m // tm, nb),
        in_specs=[pl.BlockSpec((tm, kdim), lambda i, j: (i, 0)),
                  pl.BlockSpec((None, kdim, tn), lambda i, j: (l, 0, j)),
                  pl.BlockSpec((None, kdim, tn), lambda i, j: (l, 0, j + nb))],
        out_specs=pl.BlockSpec((tm, tn), lambda i, j: (i, j)),
        out_shape=jax.ShapeDtypeStruct((m, f), BF16),
        compiler_params=_cparams(("parallel", "arbitrary")),
        name="swiglu_in",
    )(x, w, w)


def _ffn(x, g, w_in, w_out, l):
    h = _rmsnorm(x, g, BF16)
    act = _swiglu_in(h, w_in, l)
    if w_out.dtype == F32:
        return _matmul_res(act, w_out, (l,), x, 0.5, tm=1024, tn=1024, tk=2048)
    return _matmul_res(act, w_out, (l,), x, 0.5, tm=1024, tn=256)


def _pool_kernel(a_ref, w_ref, s_ref, o_ref, pad_ref, *, seq, rows):
    g = pl.program_id(1)
    pad_ref[0:POOL_PAD, :] = jnp.zeros((POOL_PAD, POOL_GW), F32)
    pad_ref[POOL_PAD:POOL_PAD + seq, :] = a_ref[...].astype(F32)
    w = w_ref[...]
    scale = s_ref[...]
    for gi, win in enumerate(POOL_WINDOWS):

        @pl.when(g == gi)
        def _(win=win):
            for c in range(seq // rows):
                r0 = POOL_PAD + c * rows
                cur = pad_ref[r0:r0 + rows, :]
                tot = cur
                for k in range(1, win):
                    tot = tot + pad_ref[r0 - k:r0 - k + rows, :]
                t1 = lax.broadcasted_iota(jnp.int32, (rows, POOL_GW), 0) + (c * rows + 1)
                cnt = jnp.minimum(t1, win).astype(F32)
                mixed = (tot / cnt - cur).astype(BF16)
                y = jnp.dot(mixed, w, preferred_element_type=F32) * scale
                o_ref[c * rows:(c + 1) * rows, :] = y.astype(o_ref.dtype)


def _pool_mixer(seg_a, pool_w, pool_scale, l, bsz, seq):
    rows = min(256, seq)
    return pl.pallas_call(
        functools.partial(_pool_kernel, seq=seq, rows=rows),
        grid=(bsz, POOL_GROUPS),
        in_specs=[pl.BlockSpec((seq, POOL_GW), lambda b, g: (b, g)),
                  pl.BlockSpec((None, None, POOL_GW, POOL_GW), lambda b, g: (l, g, 0, 0)),
                  pl.BlockSpec((None, None, 1, POOL_GW), lambda b, g: (l, g, 0, 0))],
        out_specs=pl.BlockSpec((seq, POOL_GW), lambda b, g: (b, g)),
        out_shape=jax.ShapeDtypeStruct((bsz * seq, POOL_WIDTH), BF16),
        scratch_shapes=[pltpu.VMEM((POOL_PAD + seq, POOL_GW), F32)],
        compiler_params=_cparams(("parallel", "arbitrary")),
        name="pool_mixer",
    )(seg_a, pool_w, pool_scale.reshape(pool_scale.shape[0], POOL_GROUPS, 1, POOL_GW))


def _sg_kernel(u_ref, v_ref, g_ref, b_ref, w_ref, bias_ref, o_ref, *, nblk):
    ri = lax.broadcasted_iota(jnp.int32, (SG_BLOCK, SG_BLOCK), 0) // CHUNK
    ci = lax.broadcasted_iota(jnp.int32, (SG_BLOCK, SG_BLOCK), 1) // CHUNK
    causal = ri >= ci
    wm = [jnp.where(causal, w_ref[gi], 0.0).astype(BF16) for gi in range(SG_GROUPS)]
    bias = bias_ref[...]
    for n in range(nblk):
        rs = slice(n * SG_BLOCK, (n + 1) * SG_BLOCK)
        v = _gelu(v_ref[rs, :].astype(F32))
        mu = jnp.mean(v, axis=-1, keepdims=True)
        vc = v - mu
        var = jnp.mean(vc * vc, axis=-1, keepdims=True)
        vn = (vc * lax.rsqrt(var + EPS) * g_ref[...] + b_ref[...]).astype(BF16)
        u = _gelu(u_ref[rs, :].astype(F32))
        for gi in range(SG_GROUPS):
            cs = slice(gi * SG_GW, (gi + 1) * SG_GW)
            sv = jnp.dot(wm[gi], vn[:, cs], preferred_element_type=F32) + bias[:, cs]
            o_ref[rs, cs] = (u[:, cs] * sv).astype(o_ref.dtype)


def _sg_mixer(seg_a, ln_g, ln_b, sg_w, bias_tile, l, m):
    tb = min(512, m)
    ub, vb = C_U // SG_WIDTH, C_V // SG_WIDTH
    return pl.pallas_call(
        functools.partial(_sg_kernel, nblk=tb // SG_BLOCK),
        grid=(m // tb,),
        in_specs=[pl.BlockSpec((tb, SG_WIDTH), lambda i: (i, ub)),
                  pl.BlockSpec((tb, SG_WIDTH), lambda i: (i, vb)),
                  pl.BlockSpec((None, 1, SG_WIDTH), lambda i: (l, 0, 0)),
                  pl.BlockSpec((None, 1, SG_WIDTH), lambda i: (l, 0, 0)),
                  pl.BlockSpec((None, SG_GROUPS, SG_BLOCK, SG_BLOCK), lambda i: (l, 0, 0, 0)),
                  pl.BlockSpec((None, SG_BLOCK, SG_WIDTH), lambda i: (l, 0, 0))],
        out_specs=pl.BlockSpec((tb, SG_WIDTH), lambda i: (i, 0)),
        out_shape=jax.ShapeDtypeStruct((m, SG_WIDTH), BF16),
        compiler_params=_cparams(("parallel",)),
        name="sg_mixer",
    )(seg_a, seg_a, ln_g.reshape(-1, 1, SG_WIDTH), ln_b.reshape(-1, 1, SG_WIDTH), sg_w, bias_tile)


def _ssd_kernel(z_ref, xc_ref, xp_ref, bc_ref, bp_ref, dt_ref,
                cwx_ref, cwb_ref, cbx_ref, cbb_ref, dtb_ref, aexp_ref, dexp_ref, ng_ref, e_ref,
                o_ref, st_ref):
    c = pl.program_id(1)

    @pl.when(c == 0)
    def _():
        st_ref[...] = jnp.zeros(st_ref.shape, F32)

    has_prev = c > 0
    srow = lax.broadcasted_iota(jnp.int32, (3 * CHUNK, 2 * CHUNK), 0)
    scol = lax.broadcasted_iota(jnp.int32, (3 * CHUNK, 2 * CHUNK), 1)
    shift = jnp.where(scol == CHUNK + (srow % CHUNK) - (3 - srow // CHUNK), 1.0, 0.0).astype(BF16)

    def conv(cur_ref, prev_ref, w_ref, b_ref):
        cur = cur_ref[...]
        prev = jnp.where(has_prev, prev_ref[...], jnp.zeros_like(cur))
        both = jnp.concatenate([prev, cur], axis=0)
        sh = jnp.dot(shift, both, preferred_element_type=F32)
        w = w_ref[...]
        acc = cur.astype(F32) * w[3:4, :] + b_ref[...]
        for k in range(SSM_CONV - 1):
            acc = acc + sh[k * CHUNK:(k + 1) * CHUNK, :] * w[k:k + 1, :]
        return _silu(acc)

    xs = conv(xc_ref, xp_ref, cwx_ref, cbx_ref)
    bcv = conv(bc_ref, bp_ref, cwb_ref, cbb_ref)
    gn = SSM_GROUPS * SSM_STATE
    bm = bcv[:, :gn].astype(BF16)
    cm = bcv[:, gn:].astype(BF16)

    hi = lax.Precision.HIGHEST
    lane = lax.broadcasted_iota(jnp.int32, (CHUNK, LANE), 1)
    dt = jnp.where(lane < SSM_HEADS, _softplus(dt_ref[...] + dtb_ref[...]), 0.0)
    dt_e = jnp.dot(dt, e_ref[...], precision=hi, preferred_element_type=F32)
    a_e = dt_e * aexp_ref[...]
    r64 = lax.broadcasted_iota(jnp.int32, (CHUNK, CHUNK), 0)
    c64 = lax.broadcasted_iota(jnp.int32, (CHUNK, CHUNK), 1)
    tri = jnp.where(c64 <= r64, 1.0, 0.0).astype(F32)
    a_cs = jnp.dot(tri, a_e, precision=hi, preferred_element_type=F32)
    rl = lax.broadcasted_iota(jnp.int32, (CHUNK, SSM_INNER), 0)
    cl = lax.broadcasted_iota(jnp.int32, (CHUNK, SSM_INNER), 1) % SSM_HEADDIM
    diag = jnp.where(rl == cl, a_cs, 0.0)
    ones = jnp.ones((CHUNK, CHUNK), F32)
    a_row = jnp.dot(ones, diag, precision=hi, preferred_element_type=F32)
    decay = jnp.exp(jnp.where(rl >= cl, a_cs - a_row, NEG_BIG))
    a_last = a_cs[CHUNK - 1:CHUNK, :]
    xd = xs * dt_e
    xe = (xd * jnp.exp(a_last - a_cs)).astype(BF16)
    xdb = xd.astype(BF16)
    ea = jnp.exp(a_cs)
    cdec = jnp.exp(a_last)

    br = lax.broadcasted_iota(jnp.int32, (SSM_GW, SSM_GW), 0) // SSM_HEADDIM
    bc_ = lax.broadcasted_iota(jnp.int32, (SSM_GW, SSM_GW), 1) // SSM_HEADDIM
    blockdiag = br == bc_
    nt = (((1,), (1,)), ((), ()))
    tn = (((0,), (0,)), ((), ()))
    ys = []
    for g in range(SSM_GROUPS):
        ns = slice(g * SSM_STATE, (g + 1) * SSM_STATE)
        ls = slice(g * SSM_GW, (g + 1) * SSM_GW)
        cg, bg = cm[:, ns], bm[:, ns]
        b_t = jnp.concatenate([bg] * SSM_HG, axis=0)
        cb = lax.dot_general(cg, b_t, nt, preferred_element_type=F32)
        mg = (cb * decay[:, ls]).astype(BF16)
        xg = xdb[:, ls]
        bd = jnp.where(blockdiag, jnp.concatenate([xg] * SSM_HG, axis=0), jnp.zeros((), BF16))
        y_diag = jnp.dot(mg, bd, preferred_element_type=F32)
        st = st_ref[g]
        y_off = jnp.dot(cg, st.astype(BF16), preferred_element_type=F32) * ea[:, ls]
        upd = lax.dot_general(bg, xe[:, ls], tn, preferred_element_type=F32)
        st_ref[g] = st * cdec[:, ls] + upd
        ys.append(y_diag + y_off)
    y = jnp.concatenate(ys, axis=1) + xs * dexp_ref[...]
    y = y * _silu(z_ref[...].astype(F32))
    outs = []
    for g in range(SSM_GROUPS):
        yg = y[:, g * SSM_GW:(g + 1) * SSM_GW]
        ms = jnp.mean(yg * yg, axis=-1, keepdims=True)
        outs.append(yg * lax.rsqrt(ms + EPS))
    o_ref[...] = (jnp.concatenate(outs, axis=1) * ng_ref[...]).astype(o_ref.dtype)


def _ssd_mixer(seg_a, seg_s, conv_w, conv_b, dtb_pad, a_exp, d_exp, norm_g, e_mat, l, bsz, seq):
    nc = seq // CHUNK
    zb, xb, bb = C_Z // SSM_INNER, C_XBC // SSM_INNER, C_XBC // SSM_INNER + 1

    def row(b, c):
        return b * nc + c

    def prow(b, c):
        return b * nc + jnp.maximum(c - 1, 0)

    vec = lambda blk: pl.BlockSpec((None, 1, SSM_INNER), lambda b, c: (l, 0, blk))
    return pl.pallas_call(
        _ssd_kernel,
        grid=(bsz, nc),
        in_specs=[pl.BlockSpec((CHUNK, SSM_INNER), lambda b, c: (row(b, c), zb)),
                  pl.BlockSpec((CHUNK, SSM_INNER), lambda b, c: (row(b, c), xb)),
                  pl.BlockSpec((CHUNK, SSM_INNER), lambda b, c: (prow(b, c), xb)),
                  pl.BlockSpec((CHUNK, SSM_INNER), lambda b, c: (row(b, c), bb)),
                  pl.BlockSpec((CHUNK, SSM_INNER), lambda b, c: (prow(b, c), bb)),
                  pl.BlockSpec((CHUNK, LANE), lambda b, c: (row(b, c), 1)),
                  pl.BlockSpec((None, SSM_CONV, SSM_INNER), lambda b, c: (l, 0, 0)),
                  pl.BlockSpec((None, SSM_CONV, SSM_INNER), lambda b, c: (l, 0, 1)),
                  vec(0), vec(1),
                  pl.BlockSpec((None, 1, LANE), lambda b, c: (l, 0, 0)),
                  vec(0), vec(0), vec(0),
                  pl.BlockSpec((LANE, SSM_INNER), lambda b, c: (0, 0))],
        out_specs=pl.BlockSpec((CHUNK, SSM_INNER), lambda b, c: (row(b, c), 0)),
        out_shape=jax.ShapeDtypeStruct((bsz * seq, SSM_INNER), BF16),
        scratch_shapes=[pltpu.VMEM((SSM_GROUPS, SSM_STATE, SSM_GW), F32)],
        compiler_params=_cparams(("parallel", "arbitrary")),
        name="ssd_mixer",
    )(seg_a, seg_a, seg_a, seg_a, seg_a, seg_s, conv_w, conv_w,
      conv_b.reshape(-1, 1, 2 * SSM_INNER), conv_b.reshape(-1, 1, 2 * SSM_INNER),
      dtb_pad, a_exp, d_exp, norm_g.reshape(-1, 1, SSM_INNER), e_mat)


def _rope128(x, cos, sin_signed):
    return x * cos + pltpu.roll(x, ATT_HEADDIM // 2, 1) * sin_signed


def _rope64(x, cos, sin_signed):
    lane = lax.broadcasted_iota(jnp.int32, x.shape, 1)
    low = (lane % IDX_HEADDIM) < IDX_HEADDIM // 2
    rot = jnp.where(low, pltpu.roll(x, LANE - IDX_HEADDIM // 2, 1), pltpu.roll(x, IDX_HEADDIM // 2, 1))
    return x * cos + rot * sin_signed


def _dsa_kernel(q_ref, k_ref, v_ref, qi_ref, ki_ref, wi_ref,
                cq_ref, sq_ref, ck_ref, sk_ref, ciq_ref, siq_ref, cik_ref, sik_ref, *rest,
                klen, q0, topk):
    o_ref, kr_ref, vb_ref, kir_ref, key_ref, bias_ref, pos_ref = rest[-7:]
    i = pl.program_id(1)
    nt = (((1,), (1,)), ((), ()))
    nlc = klen // LANE

    @pl.when(i == 0)
    def _():
        for kv in range(ATT_KV_HEADS):
            hs = slice(kv * ATT_HEADDIM, (kv + 1) * ATT_HEADDIM)
            kr_ref[:, hs] = _rope128(k_ref[:, hs], ck_ref[...], sk_ref[...]).astype(BF16)
        vb_ref[...] = v_ref[...].astype(BF16)
        kir_ref[...] = _rope64(ki_ref[...], cik_ref[...], sik_ref[...])

    lane_q = lax.broadcasted_iota(jnp.int32, (Q_BLOCK, LANE), 1)
    kir = kir_ref[...]
    wi = wi_ref[...] * np.float32(IDX_HEADS ** -0.5)
    iscore = jnp.zeros((Q_BLOCK, klen), F32)
    for pair in range(IDX_HEADS // 2):
        ps = slice(pair * LANE, (pair + 1) * LANE)
        qp = _rope64(qi_ref[:, ps], ciq_ref[...], siq_ref[...])
        for sub in range(2):
            h = 2 * pair + sub
            qh = jnp.where((lane_q // IDX_HEADDIM) == sub, qp, 0.0)
            logit = lax.dot_general(qh, kir, nt, precision=lax.Precision.HIGHEST,
                                    preferred_element_type=F32)
            wcol = wi[:, S_WI_LANE + h:S_WI_LANE + h + 1]
            iscore = iscore + jnp.maximum(logit, 0.0) * wcol

    qchunk = (lax.broadcasted_iota(jnp.int32, (Q_BLOCK, klen), 0) + (q0 + i) * Q_BLOCK) // CHUNK
    kchunk = lax.broadcasted_iota(jnp.int32, (Q_BLOCK, klen), 1) // CHUNK
    iscore = jnp.where(iscore == 0.0, 0.0, iscore)
    bits = pltpu.bitcast(iscore, jnp.int32)
    key = jnp.where(bits < 0, bits ^ jnp.int32(0x7FFFFFFF), bits)
    key = jnp.maximum(key, jnp.int32(INT_MIN + 1))
    key_ref[...] = jnp.where(kchunk <= qchunk, key, jnp.int32(INT_MIN))

    def row_count(pred):
        acc = jnp.zeros((Q_BLOCK, LANE), F32)
        for cidx in range(nlc):
            acc = acc + jnp.where(pred(key_ref[:, cidx * LANE:(cidx + 1) * LANE], cidx), 1.0, 0.0)
        return jnp.broadcast_to(jnp.sum(acc, axis=-1, keepdims=True), (Q_BLOCK, LANE))

    def thr_step(it, t_u):
        cand_u = t_u | jnp.left_shift(jnp.int32(1), 31 - it)
        cand = cand_u ^ jnp.int32(INT_MIN)
        return jnp.where(row_count(lambda kc, _: kc >= cand) >= topk, cand_u, t_u)

    t_u = lax.fori_loop(0, 32, thr_step, jnp.zeros((Q_BLOCK, LANE), jnp.int32))
    thr = t_u ^ jnp.int32(INT_MIN)
    cnt_ge = row_count(lambda kc, _: kc >= thr)
    cnt_gt = row_count(lambda kc, _: kc > thr)
    need = topk - cnt_gt
    excess = jnp.where(thr > INT_MIN, cnt_ge - topk, 0.0)
    pos_ref[...] = jnp.full((Q_BLOCK, LANE), klen, jnp.int32)
    lane_pos = lax.broadcasted_iota(jnp.int32, (Q_BLOCK, LANE), 1)

    @pl.when(jnp.max(excess) > 0.0)
    def _():
        nbits = int(klen - 1).bit_length()
        never = jnp.int32(1 << 30)

        def pos_step(it, bound):
            cand = bound | jnp.left_shift(jnp.int32(1), nbits - 1 - it)
            ties = row_count(lambda kc, cidx: jnp.where(kc == thr, lane_pos + cidx * LANE, never) < cand)
            return jnp.where(ties < need, cand, bound)

        bound = lax.fori_loop(0, nbits, pos_step, jnp.zeros((Q_BLOCK, LANE), jnp.int32))
        pos_ref[...] = bound + 1

    pos = pos_ref[...]
    for cidx in range(nlc):
        cs = slice(cidx * LANE, (cidx + 1) * LANE)
        kc = key_ref[:, cs]
        tie = jnp.where(lane_pos + cidx * LANE < pos, 0.0, NEG_BIG)
        sel = jnp.where(kc > thr, 0.0, jnp.where(kc == thr, tie, NEG_BIG))
        bias_ref[:, cs] = jnp.where(kc == INT_MIN, NEG_BIG, sel)

    scale = np.float32(ATT_HEADDIM ** -0.5)
    for kv in range(ATT_KV_HEADS):
        hs = slice(kv * ATT_HEADDIM, (kv + 1) * ATT_HEADDIM)
        krh = kr_ref[:, hs]
        vh = vb_ref[:, hs]
        for gq in range(ATT_GRP):
            h = kv * ATT_GRP + gq
            qs = slice(h * ATT_HEADDIM, (h + 1) * ATT_HEADDIM)
            qh = _rope128(q_ref[:, qs], cq_ref[...], sq_ref[...]).astype(BF16)
            s = lax.dot_general(qh, krh, nt, preferred_element_type=F32) * scale + bias_ref[...]
            mx = jnp.max(s, axis=-1, keepdims=True)
            e = jnp.exp(s - mx)
            den = jnp.sum(e, axis=-1, keepdims=True)
            o = jnp.dot(e.astype(BF16), vh, preferred_element_type=F32)
            o_ref[:, qs] = (o / den).astype(o_ref.dtype)


DSA_BUCKETS = 4


def _dsa_mixer(seg_q, seg_s, tabs, bsz, seq):
    nqb = seq // Q_BLOCK
    topk = min(IDX_TOPK, seq // 4)
    cos128, sin128, cos64, sin64 = tabs
    kb, vb_, qib = 1024 // 256, 1280 // 256, 1536 // 512
    nbk = DSA_BUCKETS if nqb % DSA_BUCKETS == 0 else 1
    qpb = nqb // nbk
    width = ATT_HEADS * ATT_HEADDIM
    kvw = ATT_KV_HEADS * ATT_HEADDIM
    out = None
    for u in range(nbk):
        q0 = u * qpb
        klen = (u + 1) * qpb * Q_BLOCK
        qtab = pl.BlockSpec((Q_BLOCK, LANE), lambda b, i, q0=q0: (q0 + i, 0))
        ktab = pl.BlockSpec((klen, LANE), lambda b, i: (0, 0))
        qrow = lambda blk, q0=q0: (lambda b, i: (b, q0 + i, blk))
        in_specs = [pl.BlockSpec((None, Q_BLOCK, width), qrow(0)),
                    pl.BlockSpec((None, klen, kvw), lambda b, i: (b, 0, kb)),
                    pl.BlockSpec((None, klen, kvw), lambda b, i: (b, 0, vb_)),
                    pl.BlockSpec((None, Q_BLOCK, IDX_HEADS * IDX_HEADDIM), qrow(qib)),
                    pl.BlockSpec((None, klen, LANE), lambda b, i: (b, 0, 0)),
                    pl.BlockSpec((None, Q_BLOCK, LANE), qrow(1)),
                    qtab, qtab, ktab, ktab, qtab, qtab, ktab, ktab]
        args = [seg_q, seg_q, seg_q, seg_q, seg_s, seg_s,
                cos128, sin128, cos128, sin128, cos64, sin64, cos64, sin64]
        aliases = {}
        if out is not None:
            in_specs.append(pl.BlockSpec(memory_space=pl.ANY))
            args.append(out)
            aliases = {len(args) - 1: 0}
        out = pl.pallas_call(
            functools.partial(_dsa_kernel, klen=klen, q0=q0, topk=topk),
            grid=(bsz, qpb),
            in_specs=in_specs,
            out_specs=pl.BlockSpec((None, Q_BLOCK, width), qrow(0)),
            out_shape=jax.ShapeDtypeStruct((bsz, seq, width), BF16),
            scratch_shapes=[pltpu.VMEM((klen, kvw), BF16),
                            pltpu.VMEM((klen, kvw), BF16),
                            pltpu.VMEM((klen, LANE), F32),
                            pltpu.VMEM((Q_BLOCK, klen), jnp.int32),
                            pltpu.VMEM((Q_BLOCK, klen), F32),
                            pltpu.VMEM((Q_BLOCK, LANE), jnp.int32)],
            input_output_aliases=aliases,
            compiler_params=_cparams(("parallel", "arbitrary")),
            name="dsa_mixer",
        )(*args)
    return out


def _rope_tables(seq):
    pos = jnp.arange(seq, dtype=F32)[:, None]

    def tab(half, reps):
        inv = ROPE_THETA ** (-jnp.arange(half, dtype=F32) / half)
        ang = pos * inv[None, :]
        cos, sin = jnp.cos(ang), jnp.sin(ang)
        return (jnp.tile(jnp.concatenate([cos, cos], axis=1), (1, reps)),
                jnp.tile(jnp.concatenate([-sin, sin], axis=1), (1, reps)))

    cos128, sin128 = tab(ATT_HEADDIM // 2, 1)
    cos64, sin64 = tab(IDX_HEADDIM // 2, 2)
    return cos128, sin128, cos64, sin64


def _merge_kernel(h_ref, wg_ref, ya_ref, yb_ref, yc_ref, yd_ref, p_ref, o_ref):
    h = h_ref[...]
    acc = None
    row = 0
    for i, y_ref in enumerate((ya_ref, yb_ref, yc_ref, yd_ref)):
        width = y_ref.shape[1]
        gate = _sigmoid(jnp.dot(h, wg_ref[i], preferred_element_type=F32))
        term = gate * jnp.dot(y_ref[...], p_ref[row:row + width, :], preferred_element_type=F32)
        acc = term if acc is None else acc + term
        row += width
    o_ref[...] = acc.astype(o_ref.dtype)


def _gated_merge(h, ys, w_gate, w_branch, l, *, tm=1024, tn=256):
    m, d = h.shape
    tm = min(tm, m)
    nbr = len(ys)
    resident = lambda width: pl.BlockSpec((tm, width), lambda i, j: (i, 0), pipeline_mode=pl.Buffered(1))
    return pl.pallas_call(
        _merge_kernel,
        grid=(m // tm, d // tn),
        in_specs=[resident(d),
                  pl.BlockSpec((None, nbr, d, tn), lambda i, j: (l, 0, 0, j))]
                 + [resident(y.shape[1]) for y in ys]
                 + [pl.BlockSpec((None, w_branch.shape[1], tn), lambda i, j: (l, 0, j))],
        out_specs=pl.BlockSpec((tm, tn), lambda i, j: (i, j)),
        out_shape=jax.ShapeDtypeStruct((m, d), BF16),
        compiler_params=_cparams(("parallel", "arbitrary")),
        name="gated_merge",
    )(h, w_gate, *ys, w_branch)


def _xattn_kernel(q_ref, kv_ref, o_ref):
    nt = (((1,), (1,)), ((), ()))
    scale = np.float32(MEM_HEADDIM ** -0.5)
    hw = MEM_HEADS * MEM_HEADDIM
    for h in range(MEM_HEADS):
        hs = slice(h * MEM_HEADDIM, (h + 1) * MEM_HEADDIM)
        s = lax.dot_general(q_ref[:, hs], kv_ref[:, hs], nt, preferred_element_type=F32) * scale
        mx = jnp.max(s, axis=-1, keepdims=True)
        e = jnp.exp(s - mx)
        den = jnp.sum(e, axis=-1, keepdims=True)
        vs = slice(hw + h * MEM_HEADDIM, hw + (h + 1) * MEM_HEADDIM)
        o = jnp.dot(e.astype(BF16), kv_ref[:, vs], preferred_element_type=F32)
        o_ref[:, hs] = (o / den).astype(o_ref.dtype)


def _xattn(q, kv, bsz, seq, mem_len):
    tq = min(512, seq)
    nq = seq // tq
    hw = MEM_HEADS * MEM_HEADDIM
    return pl.pallas_call(
        _xattn_kernel,
        grid=(bsz, nq),
        in_specs=[pl.BlockSpec((tq, hw), lambda b, i: (b * nq + i, 0)),
                  pl.BlockSpec((mem_len, 2 * hw), lambda b, i: (b, 0))],
        out_specs=pl.BlockSpec((tq, hw), lambda b, i: (b * nq + i, 0)),
        out_shape=jax.ShapeDtypeStruct((bsz * seq, hw), BF16),
        compiler_params=_cparams(("parallel", "arbitrary")),
        name="mem_xattn",
    )(q, kv)


def kernel(x, mem, g_ffn1, w_ffn1_in, w_ffn1_out, g_mix, w_in, pool_w, pool_scale, sg_ln_g, sg_ln_b, sg_w, sg_b, ssm_conv_w, ssm_conv_b, ssm_a_log, ssm_dt_bias, ssm_d, ssm_norm_g, w_branch, w_gate, w_out, g_mem, g_cross, w_mem_q, w_mem_kv, w_mem_o, g_ffn2, w_ffn2_in, w_ffn2_out, g_final):
    bsz, seq, d = x.shape
    mem_len = mem.shape[1]
    depth = w_in.shape[0]
    m = bsz * seq
    bf = lambda a: a.astype(BF16)

    w1i, w1o, w2i, w2o = w_ffn1_in, w_ffn1_out, bf(w_ffn2_in), bf(w_ffn2_out)
    w_a = bf(w_in[:, :, :SEG_A])
    w_q = bf(w_in[:, :, C_Q:C_KI])
    w_ki = w_in[:, :, C_KI:C_WI]
    w_s = bf(jnp.concatenate(
        [w_ki, w_ki, w_in[:, :, C_DT:C_Q], w_in[:, :, C_WI:],
         jnp.zeros((depth, d, LANE - SSM_HEADS - IDX_HEADS), F32)], axis=2))
    wg, wb, wo = bf(w_gate), bf(w_branch), bf(w_out)
    wmq, wmkv, wmo = bf(w_mem_q), bf(w_mem_kv), bf(w_mem_o)
    pw = bf(pool_w)

    expand = lambda v: jnp.repeat(v, SSM_HEADDIM, axis=-1).reshape(depth, 1, SSM_INNER)
    a_exp = expand(-jnp.exp(ssm_a_log))
    d_exp = expand(ssm_d)
    dtb_pad = jnp.pad(ssm_dt_bias, ((0, 0), (0, LANE - SSM_HEADS))).reshape(depth, 1, LANE)
    e_mat = (jnp.arange(LANE)[:, None] == (jnp.arange(SSM_INNER)[None, :] // SSM_HEADDIM)).astype(F32)
    bias_tile = jnp.repeat(jnp.swapaxes(sg_b, 1, 2), SG_GW, axis=2)
    tabs = _rope_tables(seq)

    x2 = x.reshape(m, d)
    mem_n = _rmsnorm(mem.reshape(bsz * mem_len, d), g_mem, BF16)

    for l in range(depth):
        x2 = _ffn(x2, g_ffn1[l], w1i, w1o, l)

        h = _rmsnorm(x2, g_mix[l], BF16)
        seg_a = _matmul(h, w_a, (l,), SEG_A, tn=1024, out_dtype=BF16)
        seg_q = _matmul(h, w_q, (l,), SEG_Q, tn=1024, out_dtype=F32)
        seg_s = _matmul(h, w_s, (l,), SEG_S, tn=256, out_dtype=F32)
        y_a = _pool_mixer(seg_a, pw, pool_scale, l, bsz, seq)
        y_b = _sg_mixer(seg_a, sg_ln_g, sg_ln_b, sg_w, bias_tile, l, m)
        y_c = _ssd_mixer(seg_a, seg_s, ssm_conv_w, ssm_conv_b, dtb_pad, a_exp, d_exp,
                         ssm_norm_g, e_mat, l, bsz, seq)
        y_d = _dsa_mixer(seg_q.reshape(bsz, seq, SEG_Q), seg_s.reshape(bsz, seq, SEG_S), tabs, bsz, seq)
        merged = _gated_merge(h, (y_a, y_b, y_c, y_d.reshape(m, -1)), wg, wb, l)
        x2 = _matmul_res(merged, wo, (l,), x2, 1.0)

        hc = _rmsnorm(x2, g_cross[l], BF16)
        q = _matmul(hc, wmq, (l,), MEM_HEADS * MEM_HEADDIM, out_dtype=BF16)
        kv = _matmul(mem_n, wmkv, (l,), 2 * MEM_HEADS * MEM_HEADDIM, out_dtype=BF16)
        att = _xattn(q, kv, bsz, seq, mem_len)
        x2 = _matmul_res(att, wmo, (l,), x2, 1.0, tn=1024)

        x2 = _ffn(x2, g_ffn2[l], w2i, w2o, l)

    return _rmsnorm(x2, g_final, F32).reshape(bsz, seq, d)
```

```python
import functools

import jax
import jax.numpy as jnp
import numpy as np
from jax import lax
from jax.experimental import pallas as pl
from jax.experimental.pallas import tpu as pltpu

F32 = jnp.float32
BF16 = jnp.bfloat16

D_MODEL = 4096
FFN_DIM = 8192
CHUNK = 64
EPS = 1e-6
ROPE_THETA = 10000.0

POOL_WINDOWS = (2, 4, 8, 16)
POOL_GROUPS = 4
POOL_WIDTH = 2048
POOL_GW = POOL_WIDTH // POOL_GROUPS
POOL_PAD = 16

SG_WIDTH = 1024
SG_BLOCK = 128
SG_GROUPS = 4
SG_GW = SG_WIDTH // SG_GROUPS

SSM_HEADS = 16
SSM_HEADDIM = 64
SSM_INNER = SSM_HEADS * SSM_HEADDIM
SSM_GROUPS = 4
SSM_STATE = 128
SSM_CONV = 4
SSM_HG = SSM_HEADS // SSM_GROUPS
SSM_GW = SSM_INNER // SSM_GROUPS

ATT_HEADS = 8
ATT_KV_HEADS = 2
ATT_HEADDIM = 128
ATT_GRP = ATT_HEADS // ATT_KV_HEADS
IDX_HEADS = 8
IDX_HEADDIM = 64
IDX_TOPK = 256
Q_BLOCK = 128

MEM_HEADS = 4
MEM_HEADDIM = 128

C_POOL, C_U, C_V, C_Z, C_XBC = 0, 2048, 3072, 4096, 5120
C_DT, C_Q, C_K, C_VAL, C_QI, C_KI, C_WI = 7168, 7184, 8208, 8464, 8720, 9232, 9296
SEG_A = 7168
SEG_Q = 2048
SEG_S = 256
S_DT_LANE = 0
S_WI_LANE = 16

LANE = 128
VMEM_LIMIT = 56 * 1024 * 1024
NEG_BIG = -1e30
INT_MIN = -2147483648


def _cparams(sem):
    return pltpu.CompilerParams(dimension_semantics=sem, vmem_limit_bytes=VMEM_LIMIT)


def _sigmoid(x):
    return 1.0 / (1.0 + jnp.exp(-x))


def _silu(x):
    return x * _sigmoid(x)


def _gelu(x):
    return 0.5 * x * (1.0 + lax.erf(x * np.float32(1.0 / np.sqrt(2.0))))


def _softplus(x):
    return jnp.maximum(x, 0.0) + jnp.log1p(jnp.exp(-jnp.abs(x)))


def _rmsnorm_kernel(x_ref, g_ref, o_ref):
    x = x_ref[...]
    ms = jnp.mean(x * x, axis=-1, keepdims=True)
    o_ref[...] = (x * lax.rsqrt(ms + EPS) * g_ref[...]).astype(o_ref.dtype)


def _rmsnorm(x, g, out_dtype):
    m, d = x.shape
    tm = min(512, m)
    return pl.pallas_call(
        _rmsnorm_kernel,
        grid=(m // tm,),
        in_specs=[pl.BlockSpec((tm, d), lambda i: (i, 0)),
                  pl.BlockSpec((1, d), lambda i: (0, 0))],
        out_specs=pl.BlockSpec((tm, d), lambda i: (i, 0)),
        out_shape=jax.ShapeDtypeStruct((m, d), out_dtype),
        compiler_params=_cparams(("parallel",)),
        name="rmsnorm",
    )(x, g.reshape(1, d))


def _mm_kernel(x_ref, w_ref, o_ref):
    o_ref[...] = jnp.dot(x_ref[...], w_ref[...].astype(BF16), preferred_element_type=F32).astype(o_ref.dtype)


def _mm_res_kernel(x_ref, w_ref, r_ref, o_ref, *, alpha):
    acc = jnp.dot(x_ref[...], w_ref[...].astype(BF16), preferred_element_type=F32)
    o_ref[...] = r_ref[...] + alpha * acc


def _mm_res_acc_kernel(x_ref, w_ref, r_ref, o_ref, acc_ref, *, alpha, nk):
    k = pl.program_id(2)
    part = jnp.dot(x_ref[...], w_ref[...].astype(BF16), preferred_element_type=F32)

    @pl.when(k == 0)
    def _():
        acc_ref[...] = part

    @pl.when(k > 0)
    def _():
        acc_ref[...] += part

    @pl.when(k == nk - 1)
    def _():
        o_ref[...] = r_ref[...] + alpha * acc_ref[...]


def _w_spec(w, widx, kblk, tn, col_blk0, kgrid):
    lead = (None,) * len(widx)
    if kgrid:
        return pl.BlockSpec(lead + (kblk, tn), lambda i, j, k: (*widx, k, j + col_blk0))
    return pl.BlockSpec(lead + (kblk, tn), lambda i, j: (*widx, 0, j + col_blk0))


def _matmul(x, w, widx, n, *, col0=0, tm=1024, tn=512, out_dtype=BF16):
    m, kdim = x.shape
    tm, tn = min(tm, m), min(tn, n)
    return pl.pallas_call(
        _mm_kernel,
        grid=(m // tm, n // tn),
        in_specs=[pl.BlockSpec((tm, kdim), lambda i, j: (i, 0)),
                  _w_spec(w, widx, kdim, tn, col0 // tn, False)],
        out_specs=pl.BlockSpec((tm, tn), lambda i, j: (i, j)),
        out_shape=jax.ShapeDtypeStruct((m, n), out_dtype),
        compiler_params=_cparams(("parallel", "arbitrary")),
        name="matmul",
    )(x, w)


def _x_spec(tm, kdim, resident=False):
    if resident:
        return pl.BlockSpec((tm, kdim), lambda i, j: (i, 0), pipeline_mode=pl.Buffered(1))
    return pl.BlockSpec((tm, kdim), lambda i, j: (i, 0))


def _matmul_res(x, w, widx, res, alpha, *, tm=1024, tn=512, tk=None, resident_x=False):
    m, kdim = x.shape
    n = res.shape[1]
    tm, tn = min(tm, m), min(tn, n)
    if tk is None or tk >= kdim:
        return pl.pallas_call(
            functools.partial(_mm_res_kernel, alpha=alpha),
            grid=(m // tm, n // tn),
            in_specs=[_x_spec(tm, kdim, resident_x),
                      _w_spec(w, widx, kdim, tn, 0, False),
                      pl.BlockSpec((tm, tn), lambda i, j: (i, j))],
            out_specs=pl.BlockSpec((tm, tn), lambda i, j: (i, j)),
            out_shape=jax.ShapeDtypeStruct((m, n), F32),
            compiler_params=_cparams(("parallel", "arbitrary")),
            name="matmul_res",
        )(x, w, res)
    nk = kdim // tk
    return pl.pallas_call(
        functools.partial(_mm_res_acc_kernel, alpha=alpha, nk=nk),
        grid=(m // tm, n // tn, nk),
        in_specs=[pl.BlockSpec((tm, tk), lambda i, j, k: (i, k)),
                  _w_spec(w, widx, tk, tn, 0, True),
                  pl.BlockSpec((tm, tn), lambda i, j, k: (i, j))],
        out_specs=pl.BlockSpec((tm, tn), lambda i, j, k: (i, j)),
        out_shape=jax.ShapeDtypeStruct((m, n), F32),
        scratch_shapes=[pltpu.VMEM((tm, tn), F32)],
        compiler_params=_cparams(("parallel", "arbitrary", "arbitrary")),
        name="matmul_res_acc",
    )(x, w, res)


def _mm_res_norm_kernel(x_ref, w_ref, r_ref, g_ref, o_ref, h_ref):
    y = r_ref[...] + jnp.dot(x_ref[...], w_ref[...].astype(BF16), preferred_element_type=F32)
    o_ref[...] = y
    ms = jnp.mean(y * y, axis=-1, keepdims=True)
    h_ref[...] = (y * lax.rsqrt(ms + EPS) * g_ref[...]).astype(h_ref.dtype)


def _matmul_res_norm(x, w, widx, res, g, *, tm=256):
    m, kdim = x.shape
    n = res.shape[1]
    tm = min(tm, m)
    lead = (None,) * len(widx)
    row = pl.BlockSpec((tm, n), lambda i: (i, 0))
    return pl.pallas_call(
        _mm_res_norm_kernel,
        grid=(m // tm,),
        in_specs=[pl.BlockSpec((tm, kdim), lambda i: (i, 0)),
                  pl.BlockSpec(lead + (kdim, n), lambda i: (*widx, 0, 0)),
                  row,
                  pl.BlockSpec((1, n), lambda i: (0, 0))],
        out_specs=[row, row],
        out_shape=[jax.ShapeDtypeStruct((m, n), F32), jax.ShapeDtypeStruct((m, n), BF16)],
        compiler_params=_cparams(("parallel",)),
        name="matmul_res_norm",
    )(x, w, res, g.reshape(1, n))


def _swiglu_kernel(x_ref, wg_ref, wu_ref, o_ref):
    x = x_ref[...]
    g = jnp.dot(x, wg_ref[...].astype(BF16), preferred_element_type=F32)
    u = jnp.dot(x, wu_ref[...].astype(BF16), preferred_element_type=F32)
    o_ref[...] = (_silu(g) * u).astype(o_ref.dtype)


def _swiglu_in(x, w, l, *, tm=1024):
    m, kdim = x.shape
    f = w.shape[-1] // 2
    tm = min(tm, m)
    tn = 256 if w.dtype == F32 else 512
    nb = f // tn
    return pl.pallas_call(
        _swiglu_kernel,
        grid=(m // tm, nb),
        in_specs=[_x_spec(tm, kdim, resident=tm > 1024),
                  pl.BlockSpec((None, kdim, tn), lambda i, j: (l, 0, j)),
                  pl.BlockSpec((None, kdim, tn), lambda i, j: (l, 0, j + nb))],
        out_specs=pl.BlockSpec((tm, tn), lambda i, j: (i, j)),
        out_shape=jax.ShapeDtypeStruct((m, f), BF16),
        compiler_params=_cparams(("parallel", "arbitrary")),
        name="swiglu_in",
    )(x, w, w)


def _ffn(x, g, w_in, w_out, l, *, tm_in=1024, h=None):
    if h is None:
        h = _rmsnorm(x, g, BF16)
    act = _swiglu_in(h, w_in, l, tm=tm_in)
    return _matmul_res(act, w_out, (l,), x, 0.5, tm=1024, tn=256, resident_x=w_out.dtype == F32)


def _pool_kernel(a_ref, w_ref, s_ref, o_ref, pad_ref, *, seq, rows):
    g = pl.program_id(1)
    pad_ref[0:POOL_PAD, :] = jnp.zeros((POOL_PAD, POOL_GW), F32)
    pad_ref[POOL_PAD:POOL_PAD + seq, :] = a_ref[...].astype(F32)
    w = w_ref[...]
    scale = s_ref[...]
    for gi, win in enumerate(POOL_WINDOWS):

        @pl.when(g == gi)
        def _(win=win):
            for c in range(seq // rows):
                r0 = POOL_PAD + c * rows
                cur = pad_ref[r0:r0 + rows, :]
                tot = cur
                for k in range(1, win):
                    tot = tot + pad_ref[r0 - k:r0 - k + rows, :]
                t1 = lax.broadcasted_iota(jnp.int32, (rows, POOL_GW), 0) + (c * rows + 1)
                cnt = jnp.minimum(t1, win).astype(F32)
                mixed = (tot / cnt - cur).astype(BF16)
                y = jnp.dot(mixed, w, preferred_element_type=F32) * scale
                o_ref[c * rows:(c + 1) * rows, :] = y.astype(o_ref.dtype)


def _pool_mixer(seg_a, pool_w, pool_scale, l, bsz, seq):
    rows = min(256, seq)
    return pl.pallas_call(
        functools.partial(_pool_kernel, seq=seq, rows=rows),
        grid=(bsz, POOL_GROUPS),
        in_specs=[pl.BlockSpec((seq, POOL_GW), lambda b, g: (b, g)),
                  pl.BlockSpec((None, None, POOL_GW, POOL_GW), lambda b, g: (l, g, 0, 0)),
                  pl.BlockSpec((None, None, 1, POOL_GW), lambda b, g: (l, g, 0, 0))],
        out_specs=pl.BlockSpec((seq, POOL_GW), lambda b, g: (b, g)),
        out_shape=jax.ShapeDtypeStruct((bsz * seq, POOL_WIDTH), BF16),
        scratch_shapes=[pltpu.VMEM((POOL_PAD + seq, POOL_GW), F32)],
        compiler_params=_cparams(("parallel", "arbitrary")),
        name="pool_mixer",
    )(seg_a, pool_w, pool_scale.reshape(pool_scale.shape[0], POOL_GROUPS, 1, POOL_GW))


def _sg_kernel(u_ref, v_ref, g_ref, b_ref, w_ref, bias_ref, o_ref, *, nblk):
    ri = lax.broadcasted_iota(jnp.int32, (SG_BLOCK, SG_BLOCK), 0) // CHUNK
    ci = lax.broadcasted_iota(jnp.int32, (SG_BLOCK, SG_BLOCK), 1) // CHUNK
    causal = ri >= ci
    wm = [jnp.where(causal, w_ref[gi], 0.0).astype(BF16) for gi in range(SG_GROUPS)]
    bias = bias_ref[...]
    for n in range(nblk):
        rs = slice(n * SG_BLOCK, (n + 1) * SG_BLOCK)
        v = _gelu(v_ref[rs, :].astype(F32))
        mu = jnp.mean(v, axis=-1, keepdims=True)
        vc = v - mu
        var = jnp.mean(vc * vc, axis=-1, keepdims=True)
        vn = (vc * lax.rsqrt(var + EPS) * g_ref[...] + b_ref[...]).astype(BF16)
        u = _gelu(u_ref[rs, :].astype(F32))
        for gi in range(SG_GROUPS):
            cs = slice(gi * SG_GW, (gi + 1) * SG_GW)
            sv = jnp.dot(wm[gi], vn[:, cs], preferred_element_type=F32) + bias[:, cs]
            o_ref[rs, cs] = (u[:, cs] * sv).astype(o_ref.dtype)


def _sg_mixer(seg_a, ln_g, ln_b, sg_w, bias_tile, l, m):
    tb = min(512, m)
    ub, vb = C_U // SG_WIDTH, C_V // SG_WIDTH
    return pl.pallas_call(
        functools.partial(_sg_kernel, nblk=tb // SG_BLOCK),
        grid=(m // tb,),
        in_specs=[pl.BlockSpec((tb, SG_WIDTH), lambda i: (i, ub)),
                  pl.BlockSpec((tb, SG_WIDTH), lambda i: (i, vb)),
                  pl.BlockSpec((None, 1, SG_WIDTH), lambda i: (l, 0, 0)),
                  pl.BlockSpec((None, 1, SG_WIDTH), lambda i: (l, 0, 0)),
                  pl.BlockSpec((None, SG_GROUPS, SG_BLOCK, SG_BLOCK), lambda i: (l, 0, 0, 0)),
                  pl.BlockSpec((None, SG_BLOCK, SG_WIDTH), lambda i: (l, 0, 0))],
        out_specs=pl.BlockSpec((tb, SG_WIDTH), lambda i: (i, 0)),
        out_shape=jax.ShapeDtypeStruct((m, SG_WIDTH), BF16),
        compiler_params=_cparams(("parallel",)),
        name="sg_mixer",
    )(seg_a, seg_a, ln_g.reshape(-1, 1, SG_WIDTH), ln_b.reshape(-1, 1, SG_WIDTH), sg_w, bias_tile)


def _ssd_kernel(z_ref, xc_ref, xp_ref, bc_ref, bp_ref, dt_ref,
                cwx_ref, cwb_ref, cbx_ref, cbb_ref, dtb_ref, aexp_ref, dexp_ref, ng_ref, e_ref,
                o_ref, st_ref):
    c = pl.program_id(1)

    @pl.when(c == 0)
    def _():
        st_ref[...] = jnp.zeros(st_ref.shape, F32)

    has_prev = c > 0
    srow = lax.broadcasted_iota(jnp.int32, (3 * CHUNK, 2 * CHUNK), 0)
    scol = lax.broadcasted_iota(jnp.int32, (3 * CHUNK, 2 * CHUNK), 1)
    shift = jnp.where(scol == CHUNK + (srow % CHUNK) - (3 - srow // CHUNK), 1.0, 0.0).astype(BF16)

    def conv(cur_ref, prev_ref, w_ref, b_ref):
        cur = cur_ref[...]
        prev = jnp.where(has_prev, prev_ref[...], jnp.zeros_like(cur))
        both = jnp.concatenate([prev, cur], axis=0)
        sh = jnp.dot(shift, both, preferred_element_type=F32)
        w = w_ref[...]
        acc = cur.astype(F32) * w[3:4, :] + b_ref[...]
        for k in range(SSM_CONV - 1):
            acc = acc + sh[k * CHUNK:(k + 1) * CHUNK, :] * w[k:k + 1, :]
        return _silu(acc)

    xs = conv(xc_ref, xp_ref, cwx_ref, cbx_ref)
    bcv = conv(bc_ref, bp_ref, cwb_ref, cbb_ref)
    gn = SSM_GROUPS * SSM_STATE
    bm = bcv[:, :gn].astype(BF16)
    cm = bcv[:, gn:].astype(BF16)

    hi = lax.Precision.HIGHEST
    lane = lax.broadcasted_iota(jnp.int32, (CHUNK, LANE), 1)
    dt = jnp.where(lane < SSM_HEADS, _softplus(dt_ref[...] + dtb_ref[...]), 0.0)
    dt_e = jnp.dot(dt, e_ref[...], precision=hi, preferred_element_type=F32)
    a_e = dt_e * aexp_ref[...]
    r64 = lax.broadcasted_iota(jnp.int32, (CHUNK, CHUNK), 0)
    c64 = lax.broadcasted_iota(jnp.int32, (CHUNK, CHUNK), 1)
    tri = jnp.where(c64 <= r64, 1.0, 0.0).astype(F32)
    a_cs = jnp.dot(tri, a_e, precision=hi, preferred_element_type=F32)
    rl = lax.broadcasted_iota(jnp.int32, (CHUNK, SSM_INNER), 0)
    cl = lax.broadcasted_iota(jnp.int32, (CHUNK, SSM_INNER), 1) % SSM_HEADDIM
    diag = jnp.where(rl == cl, a_cs, 0.0)
    ones = jnp.ones((CHUNK, CHUNK), F32)
    a_row = jnp.dot(ones, diag, precision=hi, preferred_element_type=F32)
    decay = jnp.exp(jnp.where(rl >= cl, a_cs - a_row, NEG_BIG))
    a_last = a_cs[CHUNK - 1:CHUNK, :]
    xd = xs * dt_e
    xe = (xd * jnp.exp(a_last - a_cs)).astype(BF16)
    xdb = xd.astype(BF16)
    ea = jnp.exp(a_cs)
    cdec = jnp.exp(a_last)

    br = lax.broadcasted_iota(jnp.int32, (SSM_GW, SSM_GW), 0) // SSM_HEADDIM
    bc_ = lax.broadcasted_iota(jnp.int32, (SSM_GW, SSM_GW), 1) // SSM_HEADDIM
    blockdiag = br == bc_
    nt = (((1,), (1,)), ((), ()))
    tn = (((0,), (0,)), ((), ()))
    ys = []
    for g in range(SSM_GROUPS):
        ns = slice(g * SSM_STATE, (g + 1) * SSM_STATE)
        ls = slice(g * SSM_GW, (g + 1) * SSM_GW)
        cg, bg = cm[:, ns], bm[:, ns]
        b_t = jnp.concatenate([bg] * SSM_HG, axis=0)
        cb = lax.dot_general(cg, b_t, nt, preferred_element_type=F32)
        mg = (cb * decay[:, ls]).astype(BF16)
        xg = xdb[:, ls]
        bd = jnp.where(blockdiag, jnp.concatenate([xg] * SSM_HG, axis=0), jnp.zeros((), BF16))
        y_diag = jnp.dot(mg, bd, preferred_element_type=F32)
        st = st_ref[g]
        y_off = jnp.dot(cg, st.astype(BF16), preferred_element_type=F32) * ea[:, ls]
        upd = lax.dot_general(bg, xe[:, ls], tn, preferred_element_type=F32)
        st_ref[g] = st * cdec[:, ls] + upd
        ys.append(y_diag + y_off)
    y = jnp.concatenate(ys, axis=1) + xs * dexp_ref[...]
    y = y * _silu(z_ref[...].astype(F32))
    outs = []
    for g in range(SSM_GROUPS):
        yg = y[:, g * SSM_GW:(g + 1) * SSM_GW]
        ms = jnp.mean(yg * yg, axis=-1, keepdims=True)
        outs.append(yg * lax.rsqrt(ms + EPS))
    o_ref[...] = (jnp.concatenate(outs, axis=1) * ng_ref[...]).astype(o_ref.dtype)


def _ssd_mixer(seg_a, seg_s, conv_w, conv_b, dtb_pad, a_exp, d_exp, norm_g, e_mat, l, bsz, seq):
    nc = seq // CHUNK
    zb, xb, bb = C_Z // SSM_INNER, C_XBC // SSM_INNER, C_XBC // SSM_INNER + 1

    def row(b, c):
        return b * nc + c

    def prow(b, c):
        return b * nc + jnp.maximum(c - 1, 0)

    vec = lambda blk: pl.BlockSpec((None, 1, SSM_INNER), lambda b, c: (l, 0, blk))
    return pl.pallas_call(
        _ssd_kernel,
        grid=(bsz, nc),
        in_specs=[pl.BlockSpec((CHUNK, SSM_INNER), lambda b, c: (row(b, c), zb)),
                  pl.BlockSpec((CHUNK, SSM_INNER), lambda b, c: (row(b, c), xb)),
                  pl.BlockSpec((CHUNK, SSM_INNER), lambda b, c: (prow(b, c), xb)),
                  pl.BlockSpec((CHUNK, SSM_INNER), lambda b, c: (row(b, c), bb)),
                  pl.BlockSpec((CHUNK, SSM_INNER), lambda b, c: (prow(b, c), bb)),
                  pl.BlockSpec((CHUNK, LANE), lambda b, c: (row(b, c), 1)),
                  pl.BlockSpec((None, SSM_CONV, SSM_INNER), lambda b, c: (l, 0, 0)),
                  pl.BlockSpec((None, SSM_CONV, SSM_INNER), lambda b, c: (l, 0, 1)),
                  vec(0), vec(1),
                  pl.BlockSpec((None, 1, LANE), lambda b, c: (l, 0, 0)),
                  vec(0), vec(0), vec(0),
                  pl.BlockSpec((LANE, SSM_INNER), lambda b, c: (0, 0))],
        out_specs=pl.BlockSpec((CHUNK, SSM_INNER), lambda b, c: (row(b, c), 0)),
        out_shape=jax.ShapeDtypeStruct((bsz * seq, SSM_INNER), BF16),
        scratch_shapes=[pltpu.VMEM((SSM_GROUPS, SSM_STATE, SSM_GW), F32)],
        compiler_params=_cparams(("parallel", "arbitrary")),
        name="ssd_mixer",
    )(seg_a, seg_a, seg_a, seg_a, seg_a, seg_s, conv_w, conv_w,
      conv_b.reshape(-1, 1, 2 * SSM_INNER), conv_b.reshape(-1, 1, 2 * SSM_INNER),
      dtb_pad, a_exp, d_exp, norm_g.reshape(-1, 1, SSM_INNER), e_mat)


def _rope128(x, cos, sin_signed):
    return x * cos + pltpu.roll(x, ATT_HEADDIM // 2, 1) * sin_signed


def _rope64(x, cos, sin_signed):
    lane = lax.broadcasted_iota(jnp.int32, x.shape, 1)
    low = (lane % IDX_HEADDIM) < IDX_HEADDIM // 2
    rot = jnp.where(low, pltpu.roll(x, LANE - IDX_HEADDIM // 2, 1), pltpu.roll(x, IDX_HEADDIM // 2, 1))
    return x * cos + rot * sin_signed


def _dsa_kernel(q_ref, k_ref, v_ref, qi_ref, ki_ref, wi_ref,
                cq_ref, sq_ref, ck_ref, sk_ref, ciq_ref, siq_ref, cik_ref, sik_ref, *rest,
                klen, q0, topk):
    o_ref, kr_ref, vb_ref, kir_ref, key_ref, bias_ref, pos_ref = rest[-7:]
    i = pl.program_id(1)
    nt = (((1,), (1,)), ((), ()))
    nlc = klen // LANE

    @pl.when(i == 0)
    def _():
        for kv in range(ATT_KV_HEADS):
            hs = slice(kv * ATT_HEADDIM, (kv + 1) * ATT_HEADDIM)
            kr_ref[:, hs] = _rope128(k_ref[:, hs], ck_ref[...], sk_ref[...]).astype(BF16)
        vb_ref[...] = v_ref[...].astype(BF16)
        kx = _rope64(ki_ref[...], cik_ref[...], sik_ref[...])
        kx_hi = kx.astype(BF16)
        kir_ref[:, 0:LANE] = kx_hi
        kir_ref[:, LANE:2 * LANE] = (kx - kx_hi.astype(F32)).astype(BF16)

    lane_q = lax.broadcasted_iota(jnp.int32, (Q_BLOCK, LANE), 1)
    kir = kir_ref[...]
    wi = wi_ref[...] * np.float32(IDX_HEADS ** -0.5)
    iscore = jnp.zeros((Q_BLOCK, klen), F32)
    for quad in range(IDX_HEADS // 4):
        parts = []
        for pair in range(2 * quad, 2 * quad + 2):
            ps = slice(pair * LANE, (pair + 1) * LANE)
            qp = _rope64(qi_ref[:, ps], ciq_ref[...], siq_ref[...])
            q_hi = qp.astype(BF16).astype(F32)
            q_lo_swapped = pltpu.roll(qp - q_hi, IDX_HEADDIM, 1)
            for sub in range(2):
                own = (lane_q // IDX_HEADDIM) == sub
                parts.append(jnp.concatenate([jnp.where(own, q_hi, q_lo_swapped), jnp.where(own, q_hi, 0.0)],
                                             axis=1).astype(BF16))
        logits = lax.dot_general(jnp.concatenate(parts, axis=0), kir, nt,
                                 preferred_element_type=F32)
        for hh in range(4):
            h = 4 * quad + hh
            wcol = wi[:, S_WI_LANE + h:S_WI_LANE + h + 1]
            iscore = iscore + jnp.maximum(logits[hh * Q_BLOCK:(hh + 1) * Q_BLOCK, :], 0.0) * wcol

    qchunk = (lax.broadcasted_iota(jnp.int32, (Q_BLOCK, klen), 0) + (q0 + i) * Q_BLOCK) // CHUNK
    kchunk = lax.broadcasted_iota(jnp.int32, (Q_BLOCK, klen), 1) // CHUNK
    iscore = jnp.where(iscore == 0.0, 0.0, iscore)
    bits = pltpu.bitcast(iscore, jnp.int32)
    key = jnp.where(bits < 0, bits ^ jnp.int32(0x7FFFFFFF), bits)
    key = jnp.maximum(key, jnp.int32(INT_MIN + 1))
    key_ref[...] = jnp.where(kchunk <= qchunk, key, jnp.int32(INT_MIN))

    def row_count(pred, rows=slice(0, Q_BLOCK)):
        nrow = rows.stop - rows.start
        acc = jnp.zeros((nrow, LANE), F32)
        for cidx in range(nlc):
            acc = acc + jnp.where(pred(key_ref[rows, cidx * LANE:(cidx + 1) * LANE], cidx), 1.0, 0.0)
        return jnp.broadcast_to(jnp.sum(acc, axis=-1, keepdims=True), (nrow, LANE))

    halves = (slice(0, Q_BLOCK // 2), slice(Q_BLOCK // 2, Q_BLOCK))

    def thr_step(it, t_us):
        bit = jnp.left_shift(jnp.int32(1), 31 - it)
        out = []
        for rows, t_u in zip(halves, t_us):
            cand_u = t_u | bit
            cand = cand_u ^ jnp.int32(INT_MIN)
            out.append(jnp.where(row_count(lambda kc, _: kc >= cand, rows) >= topk, cand_u, t_u))
        return tuple(out)

    t_us = lax.fori_loop(0, 32, thr_step, tuple(jnp.zeros((Q_BLOCK // 2, LANE), jnp.int32) for _ in halves))
    thr = jnp.concatenate(t_us, axis=0) ^ jnp.int32(INT_MIN)
    cnt_ge = row_count(lambda kc, _: kc >= thr)
    cnt_gt = row_count(lambda kc, _: kc > thr)
    need = topk - cnt_gt
    excess = jnp.where(thr > INT_MIN, cnt_ge - topk, 0.0)
    pos_ref[...] = jnp.full((Q_BLOCK, LANE), klen, jnp.int32)
    lane_pos = lax.broadcasted_iota(jnp.int32, (Q_BLOCK, LANE), 1)

    @pl.when(jnp.max(excess) > 0.0)
    def _():
        nbits = int(klen - 1).bit_length()
        never = jnp.int32(1 << 30)

        def pos_step(it, bound):
            cand = bound | jnp.left_shift(jnp.int32(1), nbits - 1 - it)
            ties = row_count(lambda kc, cidx: jnp.where(kc == thr, lane_pos + cidx * LANE, never) < cand)
            return jnp.where(ties < need, cand, bound)

        bound = lax.fori_loop(0, nbits, pos_step, jnp.zeros((Q_BLOCK, LANE), jnp.int32))
        pos_ref[...] = bound + 1

    pos = pos_ref[...]
    for cidx in range(nlc):
        cs = slice(cidx * LANE, (cidx + 1) * LANE)
        kc = key_ref[:, cs]
        tie = jnp.where(lane_pos + cidx * LANE < pos, 0.0, NEG_BIG)
        sel = jnp.where(kc > thr, 0.0, jnp.where(kc == thr, tie, NEG_BIG))
        bias_ref[:, cs] = jnp.where(kc == INT_MIN, NEG_BIG, sel)

    scale = np.float32(ATT_HEADDIM ** -0.5)
    for kv in range(ATT_KV_HEADS):
        hs = slice(kv * ATT_HEADDIM, (kv + 1) * ATT_HEADDIM)
        krh = kr_ref[:, hs]
        vh = vb_ref[:, hs]
        heads = [slice((kv * ATT_GRP + gq) * ATT_HEADDIM, (kv * ATT_GRP + gq + 1) * ATT_HEADDIM)
                 for gq in range(ATT_GRP)]
        qg = jnp.concatenate([_rope128(q_ref[:, qs], cq_ref[...], sq_ref[...]).astype(BF16) for qs in heads],
                             axis=0)
        s = lax.dot_general(qg, krh, nt, preferred_element_type=F32)
        es, dens = [], []
        for gq in range(ATT_GRP):
            sg = s[gq * Q_BLOCK:(gq + 1) * Q_BLOCK, :] * scale + bias_ref[...]
            e = jnp.exp(sg - jnp.max(sg, axis=-1, keepdims=True))
            dens.append(jnp.sum(e, axis=-1, keepdims=True))
            es.append(e.astype(BF16))
        o = jnp.dot(jnp.concatenate(es, axis=0), vh, preferred_element_type=F32)
        for gq, qs in enumerate(heads):
            o_ref[:, qs] = (o[gq * Q_BLOCK:(gq + 1) * Q_BLOCK, :] / dens[gq]).astype(o_ref.dtype)


DSA_BUCKETS = 4


def _dsa_mixer(seg_q, seg_s, tabs, bsz, seq):
    nqb = seq // Q_BLOCK
    topk = min(IDX_TOPK, seq // 4)
    cos128, sin128, cos64, sin64 = tabs
    kb, vb_, qib = 1024 // 256, 1280 // 256, 1536 // 512
    nbk = DSA_BUCKETS if nqb % DSA_BUCKETS == 0 else 1
    qpb = nqb // nbk
    width = ATT_HEADS * ATT_HEADDIM
    kvw = ATT_KV_HEADS * ATT_HEADDIM
    out = None
    for u in range(nbk):
        q0 = u * qpb
        klen = (u + 1) * qpb * Q_BLOCK
        qtab = pl.BlockSpec((Q_BLOCK, LANE), lambda b, i, q0=q0: (q0 + i, 0))
        ktab = pl.BlockSpec((klen, LANE), lambda b, i: (0, 0))
        qrow = lambda blk, q0=q0: (lambda b, i: (b, q0 + i, blk))
        in_specs = [pl.BlockSpec((None, Q_BLOCK, width), qrow(0)),
                    pl.BlockSpec((None, klen, kvw), lambda b, i: (b, 0, kb)),
                    pl.BlockSpec((None, klen, kvw), lambda b, i: (b, 0, vb_)),
                    pl.BlockSpec((None, Q_BLOCK, IDX_HEADS * IDX_HEADDIM), qrow(qib)),
                    pl.BlockSpec((None, klen, LANE), lambda b, i: (b, 0, 0)),
                    pl.BlockSpec((None, Q_BLOCK, LANE), qrow(1)),
                    qtab, qtab, ktab, ktab, qtab, qtab, ktab, ktab]
        args = [seg_q, seg_q, seg_q, seg_q, seg_s, seg_s,
                cos128, sin128, cos128, sin128, cos64, sin64, cos64, sin64]
        aliases = {}
        if out is not None:
            in_specs.append(pl.BlockSpec(memory_space=pl.ANY))
            args.append(out)
            aliases = {len(args) - 1: 0}
        out = pl.pallas_call(
            functools.partial(_dsa_kernel, klen=klen, q0=q0, topk=topk),
            grid=(bsz, qpb),
            in_specs=in_specs,
            out_specs=pl.BlockSpec((None, Q_BLOCK, width), qrow(0)),
            out_shape=jax.ShapeDtypeStruct((bsz, seq, width), BF16),
            scratch_shapes=[pltpu.VMEM((klen, kvw), BF16),
                            pltpu.VMEM((klen, kvw), BF16),
                            pltpu.VMEM((klen, 2 * LANE), BF16),
                            pltpu.VMEM((Q_BLOCK, klen), jnp.int32),
                            pltpu.VMEM((Q_BLOCK, klen), F32),
                            pltpu.VMEM((Q_BLOCK, LANE), jnp.int32)],
            input_output_aliases=aliases,
            compiler_params=_cparams(("parallel", "arbitrary")),
            name="dsa_mixer",
        )(*args)
    return out


def _rope_tables(seq):
    pos = jnp.arange(seq, dtype=F32)[:, None]

    def tab(half, reps):
        inv = ROPE_THETA ** (-jnp.arange(half, dtype=F32) / half)
        ang = pos * inv[None, :]
        cos, sin = jnp.cos(ang), jnp.sin(ang)
        return (jnp.tile(jnp.concatenate([cos, cos], axis=1), (1, reps)),
                jnp.tile(jnp.concatenate([-sin, sin], axis=1), (1, reps)))

    cos128, sin128 = tab(ATT_HEADDIM // 2, 1)
    cos64, sin64 = tab(IDX_HEADDIM // 2, 2)
    return cos128, sin128, cos64, sin64


def _merge_kernel(h_ref, wg_ref, ya_ref, yb_ref, yc_ref, yd_ref, p_ref, o_ref):
    h = h_ref[...]
    acc = None
    row = 0
    for i, y_ref in enumerate((ya_ref, yb_ref, yc_ref, yd_ref)):
        width = y_ref.shape[1]
        gate = _sigmoid(jnp.dot(h, wg_ref[i], preferred_element_type=F32))
        term = gate * jnp.dot(y_ref[...], p_ref[row:row + width, :], preferred_element_type=F32)
        acc = term if acc is None else acc + term
        row += width
    o_ref[...] = acc.astype(o_ref.dtype)


def _gated_merge(h, ys, w_gate, w_branch, l, *, tm=1024, tn=256):
    m, d = h.shape
    tm = min(tm, m)
    nbr = len(ys)
    resident = lambda width: pl.BlockSpec((tm, width), lambda i, j: (i, 0), pipeline_mode=pl.Buffered(1))
    return pl.pallas_call(
        _merge_kernel,
        grid=(m // tm, d // tn),
        in_specs=[resident(d),
                  pl.BlockSpec((None, nbr, d, tn), lambda i, j: (l, 0, 0, j))]
                 + [resident(y.shape[1]) for y in ys]
                 + [pl.BlockSpec((None, w_branch.shape[1], tn), lambda i, j: (l, 0, j))],
        out_specs=pl.BlockSpec((tm, tn), lambda i, j: (i, j)),
        out_shape=jax.ShapeDtypeStruct((m, d), BF16),
        compiler_params=_cparams(("parallel", "arbitrary")),
        name="gated_merge",
    )(h, w_gate, *ys, w_branch)


def _xattn_kernel(q_ref, kv_ref, o_ref):
    nt = (((1,), (1,)), ((), ()))
    scale = np.float32(MEM_HEADDIM ** -0.5)
    hw = MEM_HEADS * MEM_HEADDIM
    for h in range(MEM_HEADS):
        hs = slice(h * MEM_HEADDIM, (h + 1) * MEM_HEADDIM)
        s = lax.dot_general(q_ref[:, hs], kv_ref[:, hs], nt, preferred_element_type=F32) * scale
        mx = jnp.max(s, axis=-1, keepdims=True)
        e = jnp.exp(s - mx)
        den = jnp.sum(e, axis=-1, keepdims=True)
        vs = slice(hw + h * MEM_HEADDIM, hw + (h + 1) * MEM_HEADDIM)
        o = jnp.dot(e.astype(BF16), kv_ref[:, vs], preferred_element_type=F32)
        o_ref[:, hs] = (o / den).astype(o_ref.dtype)


def _xattn(q, kv, bsz, seq, mem_len):
    tq = min(512, seq)
    nq = seq // tq
    hw = MEM_HEADS * MEM_HEADDIM
    return pl.pallas_call(
        _xattn_kernel,
        grid=(bsz, nq),
        in_specs=[pl.BlockSpec((tq, hw), lambda b, i: (b * nq + i, 0)),
                  pl.BlockSpec((mem_len, 2 * hw), lambda b, i: (b, 0))],
        out_specs=pl.BlockSpec((tq, hw), lambda b, i: (b * nq + i, 0)),
        out_shape=jax.ShapeDtypeStruct((bsz * seq, hw), BF16),
        compiler_params=_cparams(("parallel", "arbitrary")),
        name="mem_xattn",
    )(q, kv)


def kernel(x, mem, g_ffn1, w_ffn1_in, w_ffn1_out, g_mix, w_in, pool_w, pool_scale, sg_ln_g, sg_ln_b, sg_w, sg_b, ssm_conv_w, ssm_conv_b, ssm_a_log, ssm_dt_bias, ssm_d, ssm_norm_g, w_branch, w_gate, w_out, g_mem, g_cross, w_mem_q, w_mem_kv, w_mem_o, g_ffn2, w_ffn2_in, w_ffn2_out, g_final):
    bsz, seq, d = x.shape
    mem_len = mem.shape[1]
    depth = w_in.shape[0]
    m = bsz * seq
    bf = lambda a: a.astype(BF16)

    w1i, w1o, w2i, w2o = w_ffn1_in, w_ffn1_out, w_ffn2_in, bf(w_ffn2_out)
    w_a = bf(w_in[:, :, :SEG_A])
    w_q = bf(w_in[:, :, C_Q:C_KI])
    w_ki = w_in[:, :, C_KI:C_WI]
    w_s = bf(jnp.concatenate(
        [w_ki, w_ki, w_in[:, :, C_DT:C_Q], w_in[:, :, C_WI:],
         jnp.zeros((depth, d, LANE - SSM_HEADS - IDX_HEADS), F32)], axis=2))
    wg, wb, wo = bf(w_gate), bf(w_branch), bf(w_out)
    wmq, wmkv, wmo = bf(w_mem_q), bf(w_mem_kv), bf(w_mem_o)
    pw = bf(pool_w)

    expand = lambda v: jnp.repeat(v, SSM_HEADDIM, axis=-1).reshape(depth, 1, SSM_INNER)
    a_exp = expand(-jnp.exp(ssm_a_log))
    d_exp = expand(ssm_d)
    dtb_pad = jnp.pad(ssm_dt_bias, ((0, 0), (0, LANE - SSM_HEADS))).reshape(depth, 1, LANE)
    e_mat = (jnp.arange(LANE)[:, None] == (jnp.arange(SSM_INNER)[None, :] // SSM_HEADDIM)).astype(F32)
    bias_tile = jnp.repeat(jnp.swapaxes(sg_b, 1, 2), SG_GW, axis=2)
    tabs = _rope_tables(seq)

    x2 = x.reshape(m, d)
    mem_n = _rmsnorm(mem.reshape(bsz * mem_len, d), g_mem, BF16)

    for l in range(depth):
        x2 = _ffn(x2, g_ffn1[l], w1i, w1o, l, tm_in=2048)

        h = _rmsnorm(x2, g_mix[l], BF16)
        seg_a = _matmul(h, w_a, (l,), SEG_A, tn=1024, out_dtype=BF16)
        seg_q = _matmul(h, w_q, (l,), SEG_Q, tn=1024, out_dtype=F32)
        seg_s = _matmul(h, w_s, (l,), SEG_S, tn=256, out_dtype=F32)
        y_a = _pool_mixer(seg_a, pw, pool_scale, l, bsz, seq)
        y_b = _sg_mixer(seg_a, sg_ln_g, sg_ln_b, sg_w, bias_tile, l, m)
        y_c = _ssd_mixer(seg_a, seg_s, ssm_conv_w, ssm_conv_b, dtb_pad, a_exp, d_exp,
                         ssm_norm_g, e_mat, l, bsz, seq)
        y_d = _dsa_mixer(seg_q.reshape(bsz, seq, SEG_Q), seg_s.reshape(bsz, seq, SEG_S), tabs, bsz, seq)
        merged = _gated_merge(h, (y_a, y_b, y_c, y_d.reshape(m, -1)), wg, wb, l)
        x2 = _matmul_res(merged, wo, (l,), x2, 1.0)

        hc = _rmsnorm(x2, g_cross[l], BF16)
        q = _matmul(hc, wmq, (l,), MEM_HEADS * MEM_HEADDIM, out_dtype=BF16)
        kv = _matmul(mem_n, wmkv, (l,), 2 * MEM_HEADS * MEM_HEADDIM, out_dtype=BF16)
        att = _xattn(q, kv, bsz, seq, mem_len)
        x2, h2 = _matmul_res_norm(att, wmo, (l,), x2, g_ffn2[l])

        x2 = _ffn(x2, g_ffn2[l], w2i, w2o, l, h=h2)

    return _rmsnorm(x2, g_final, F32).reshape(bsz, seq, d)
```

```python
import functools

import jax
import jax.numpy as jnp
import numpy as np
from jax import lax
from jax.experimental import pallas as pl
from jax.experimental.pallas import tpu as pltpu

F32 = jnp.float32
BF16 = jnp.bfloat16

D_MODEL = 4096
FFN_DIM = 8192
CHUNK = 64
EPS = 1e-6
ROPE_THETA = 10000.0

POOL_WINDOWS = (2, 4, 8, 16)
POOL_GROUPS = 4
POOL_WIDTH = 2048
POOL_GW = POOL_WIDTH // POOL_GROUPS
POOL_PAD = 16

SG_WIDTH = 1024
SG_BLOCK = 128
SG_GROUPS = 4
SG_GW = SG_WIDTH // SG_GROUPS

SSM_HEADS = 16
SSM_HEADDIM = 64
SSM_INNER = SSM_HEADS * SSM_HEADDIM
SSM_GROUPS = 4
SSM_STATE = 128
SSM_CONV = 4
SSM_HG = SSM_HEADS // SSM_GROUPS
SSM_GW = SSM_INNER // SSM_GROUPS

ATT_HEADS = 8
ATT_KV_HEADS = 2
ATT_HEADDIM = 128
ATT_GRP = ATT_HEADS // ATT_KV_HEADS
IDX_HEADS = 8
IDX_HEADDIM = 64
IDX_TOPK = 256
Q_BLOCK = 128

MEM_HEADS = 4
MEM_HEADDIM = 128

C_POOL, C_U, C_V, C_Z, C_XBC = 0, 2048, 3072, 4096, 5120
C_DT, C_Q, C_K, C_VAL, C_QI, C_KI, C_WI = 7168, 7184, 8208, 8464, 8720, 9232, 9296
SEG_A = 7168
SEG_Q = 2048
SEG_S = 256
S_DT_LANE = 0
S_WI_LANE = 16

LANE = 128
VMEM_LIMIT = 56 * 1024 * 1024
NEG_BIG = -1e30
INT_MIN = -2147483648


def _cparams(sem):
    return pltpu.CompilerParams(dimension_semantics=sem, vmem_limit_bytes=VMEM_LIMIT)


def _sigmoid(x):
    return 1.0 / (1.0 + jnp.exp(-x))


def _silu(x):
    return x * _sigmoid(x)


def _gelu(x):
    return 0.5 * x * (1.0 + lax.erf(x * np.float32(1.0 / np.sqrt(2.0))))


def _softplus(x):
    return jnp.maximum(x, 0.0) + jnp.log1p(jnp.exp(-jnp.abs(x)))


def _rmsnorm_kernel(x_ref, g_ref, o_ref):
    x = x_ref[...]
    ms = jnp.mean(x * x, axis=-1, keepdims=True)
    o_ref[...] = (x * lax.rsqrt(ms + EPS) * g_ref[...]).astype(o_ref.dtype)


def _rmsnorm(x, g, out_dtype):
    m, d = x.shape
    tm = min(512, m)
    return pl.pallas_call(
        _rmsnorm_kernel,
        grid=(m // tm,),
        in_specs=[pl.BlockSpec((tm, d), lambda i: (i, 0)),
                  pl.BlockSpec((1, d), lambda i: (0, 0))],
        out_specs=pl.BlockSpec((tm, d), lambda i: (i, 0)),
        out_shape=jax.ShapeDtypeStruct((m, d), out_dtype),
        compiler_params=_cparams(("parallel",)),
        name="rmsnorm",
    )(x, g.reshape(1, d))


def _mm_kernel(x_ref, w_ref, o_ref):
    o_ref[...] = jnp.dot(x_ref[...], w_ref[...].astype(BF16), preferred_element_type=F32).astype(o_ref.dtype)


def _mm_res_kernel(x_ref, w_ref, r_ref, o_ref, *, alpha):
    acc = jnp.dot(x_ref[...], w_ref[...].astype(BF16), preferred_element_type=F32)
    o_ref[...] = r_ref[...] + alpha * acc


def _mm_res_acc_kernel(x_ref, w_ref, r_ref, o_ref, acc_ref, *, alpha, nk):
    k = pl.program_id(2)
    part = jnp.dot(x_ref[...], w_ref[...].astype(BF16), preferred_element_type=F32)

    @pl.when(k == 0)
    def _():
        acc_ref[...] = part

    @pl.when(k > 0)
    def _():
        acc_ref[...] += part

    @pl.when(k == nk - 1)
    def _():
        o_ref[...] = r_ref[...] + alpha * acc_ref[...]


def _w_spec(w, widx, kblk, tn, col_blk0, kgrid):
    lead = (None,) * len(widx)
    if kgrid:
        return pl.BlockSpec(lead + (kblk, tn), lambda i, j, k: (*widx, k, j + col_blk0))
    return pl.BlockSpec(lead + (kblk, tn), lambda i, j: (*widx, 0, j + col_blk0))


def _matmul(x, w, widx, n, *, col0=0, tm=1024, tn=512, out_dtype=BF16):
    m, kdim = x.shape
    tm, tn = min(tm, m), min(tn, n)
    return pl.pallas_call(
        _mm_kernel,
        grid=(m // tm, n // tn),
        in_specs=[pl.BlockSpec((tm, kdim), lambda i, j: (i, 0)),
                  _w_spec(w, widx, kdim, tn, col0 // tn, False)],
        out_specs=pl.BlockSpec((tm, tn), lambda i, j: (i, j)),
        out_shape=jax.ShapeDtypeStruct((m, n), out_dtype),
        compiler_params=_cparams(("parallel", "arbitrary")),
        name="matmul",
    )(x, w)


def _x_spec(tm, kdim, resident=False):
    if resident:
        return pl.BlockSpec((tm, kdim), lambda i, j: (i, 0), pipeline_mode=pl.Buffered(1))
    return pl.BlockSpec((tm, kdim), lambda i, j: (i, 0))


def _matmul_res(x, w, widx, res, alpha, *, tm=1024, tn=512, tk=None, resident_x=False):
    m, kdim = x.shape
    n = res.shape[1]
    tm, tn = min(tm, m), min(tn, n)
    if tk is None or tk >= kdim:
        return pl.pallas_call(
            functools.partial(_mm_res_kernel, alpha=alpha),
            grid=(m // tm, n // tn),
            in_specs=[_x_spec(tm, kdim, resident_x),
                      _w_spec(w, widx, kdim, tn, 0, False),
                      pl.BlockSpec((tm, tn), lambda i, j: (i, j))],
            out_specs=pl.BlockSpec((tm, tn), lambda i, j: (i, j)),
            out_shape=jax.ShapeDtypeStruct((m, n), F32),
            compiler_params=_cparams(("parallel", "arbitrary")),
            name="matmul_res",
        )(x, w, res)
    nk = kdim // tk
    return pl.pallas_call(
        functools.partial(_mm_res_acc_kernel, alpha=alpha, nk=nk),
        grid=(m // tm, n // tn, nk),
        in_specs=[pl.BlockSpec((tm, tk), lambda i, j, k: (i, k)),
                  _w_spec(w, widx, tk, tn, 0, True),
                  pl.BlockSpec((tm, tn), lambda i, j, k: (i, j))],
        out_specs=pl.BlockSpec((tm, tn), lambda i, j, k: (i, j)),
        out_shape=jax.ShapeDtypeStruct((m, n), F32),
        scratch_shapes=[pltpu.VMEM((tm, tn), F32)],
        compiler_params=_cparams(("parallel", "arbitrary", "arbitrary")),
        name="matmul_res_acc",
    )(x, w, res)


def _mm_res_norm_kernel(x_ref, w_ref, r_ref, g_ref, o_ref, h_ref):
    y = r_ref[...] + jnp.dot(x_ref[...], w_ref[...].astype(BF16), preferred_element_type=F32)
    o_ref[...] = y
    ms = jnp.mean(y * y, axis=-1, keepdims=True)
    h_ref[...] = (y * lax.rsqrt(ms + EPS) * g_ref[...]).astype(h_ref.dtype)


def _matmul_res_norm(x, w, widx, res, g, *, tm=256):
    m, kdim = x.shape
    n = res.shape[1]
    tm = min(tm, m)
    lead = (None,) * len(widx)
    row = pl.BlockSpec((tm, n), lambda i: (i, 0))
    return pl.pallas_call(
        _mm_res_norm_kernel,
        grid=(m // tm,),
        in_specs=[pl.BlockSpec((tm, kdim), lambda i: (i, 0)),
                  pl.BlockSpec(lead + (kdim, n), lambda i: (*widx, 0, 0)),
                  row,
                  pl.BlockSpec((1, n), lambda i: (0, 0))],
        out_specs=[row, row],
        out_shape=[jax.ShapeDtypeStruct((m, n), F32), jax.ShapeDtypeStruct((m, n), BF16)],
        compiler_params=_cparams(("parallel",)),
        name="matmul_res_norm",
    )(x, w, res, g.reshape(1, n))


def _norm_mm_kernel(x_ref, g_ref, w_ref, o_ref):
    x = x_ref[...]
    ms = jnp.mean(x * x, axis=-1, keepdims=True)
    xn = (x * lax.rsqrt(ms + EPS) * g_ref[...]).astype(BF16)
    o_ref[...] = jnp.dot(xn, w_ref[...].astype(BF16), preferred_element_type=F32).astype(o_ref.dtype)


def _norm_matmul(x, g, w, widx, *, tm=512, out_dtype=BF16):
    m, d = x.shape
    n = w.shape[-1]
    tm = min(tm, m)
    lead = (None,) * len(widx)
    return pl.pallas_call(
        _norm_mm_kernel,
        grid=(m // tm,),
        in_specs=[pl.BlockSpec((tm, d), lambda i: (i, 0)),
                  pl.BlockSpec((1, d), lambda i: (0, 0)),
                  pl.BlockSpec(lead + (d, n), lambda i: (*widx, 0, 0))],
        out_specs=pl.BlockSpec((tm, n), lambda i: (i, 0)),
        out_shape=jax.ShapeDtypeStruct((m, n), out_dtype),
        compiler_params=_cparams(("parallel",)),
        name="norm_matmul",
    )(x, g.reshape(1, d), w)


def _swiglu_kernel(x_ref, wg_ref, wu_ref, o_ref):
    x = x_ref[...]
    g = jnp.dot(x, wg_ref[...].astype(BF16), preferred_element_type=F32)
    u = jnp.dot(x, wu_ref[...].astype(BF16), preferred_element_type=F32)
    o_ref[...] = (_silu(g) * u).astype(o_ref.dtype)


def _swiglu_in(x, w, l, *, tm=1024):
    m, kdim = x.shape
    f = w.shape[-1] // 2
    tm = min(tm, m)
    tn = 256 if w.dtype == F32 else 512
    nb = f // tn
    return pl.pallas_call(
        _swiglu_kernel,
        grid=(m // tm, nb),
        in_specs=[_x_spec(tm, kdim),
                  pl.BlockSpec((None, kdim, tn), lambda i, j: (l, 0, j)),
                  pl.BlockSpec((None, kdim, tn), lambda i, j: (l, 0, j + nb))],
        out_specs=pl.BlockSpec((tm, tn), lambda i, j: (i, j)),
        out_shape=jax.ShapeDtypeStruct((m, f), BF16),
        compiler_params=_cparams(("parallel", "arbitrary")),
        name="swiglu_in",
    )(x, w, w)


def _ffn(x, g, w_in, w_out, l, *, h=None):
    if h is None:
        h = _rmsnorm(x, g, BF16)
    act = _swiglu_in(h, w_in, l)
    return _matmul_res(act, w_out, (l,), x, 0.5, tm=1024, tn=256, resident_x=True)


def _pool_kernel(a_ref, w_ref, s_ref, o_ref, pad_ref, *, seq, rows):
    g = pl.program_id(1)
    pad_ref[0:POOL_PAD, :] = jnp.zeros((POOL_PAD, POOL_GW), F32)
    pad_ref[POOL_PAD:POOL_PAD + seq, :] = a_ref[...].astype(F32)
    w = w_ref[...]
    scale = s_ref[...]
    for gi, win in enumerate(POOL_WINDOWS):

        @pl.when(g == gi)
        def _(win=win):
            for c in range(seq // rows):
                r0 = POOL_PAD + c * rows
                cur = pad_ref[r0:r0 + rows, :]
                tot = cur
                for k in range(1, win):
                    tot = tot + pad_ref[r0 - k:r0 - k + rows, :]
                t1 = lax.broadcasted_iota(jnp.int32, (rows, POOL_GW), 0) + (c * rows + 1)
                cnt = jnp.minimum(t1, win).astype(F32)
                mixed = (tot / cnt - cur).astype(BF16)
                y = jnp.dot(mixed, w, preferred_element_type=F32) * scale
                o_ref[c * rows:(c + 1) * rows, :] = y.astype(o_ref.dtype)


def _pool_mixer(seg_a, pool_w, pool_scale, l, bsz, seq):
    rows = min(256, seq)
    return pl.pallas_call(
        functools.partial(_pool_kernel, seq=seq, rows=rows),
        grid=(bsz, POOL_GROUPS),
        in_specs=[pl.BlockSpec((seq, POOL_GW), lambda b, g: (b, g)),
                  pl.BlockSpec((None, None, POOL_GW, POOL_GW), lambda b, g: (l, g, 0, 0)),
                  pl.BlockSpec((None, None, 1, POOL_GW), lambda b, g: (l, g, 0, 0))],
        out_specs=pl.BlockSpec((seq, POOL_GW), lambda b, g: (b, g)),
        out_shape=jax.ShapeDtypeStruct((bsz * seq, POOL_WIDTH), BF16),
        scratch_shapes=[pltpu.VMEM((POOL_PAD + seq, POOL_GW), F32)],
        compiler_params=_cparams(("parallel", "arbitrary")),
        name="pool_mixer",
    )(seg_a, pool_w, pool_scale.reshape(pool_scale.shape[0], POOL_GROUPS, 1, POOL_GW))


def _sg_kernel(u_ref, v_ref, g_ref, b_ref, w_ref, bias_ref, o_ref, *, nblk):
    ri = lax.broadcasted_iota(jnp.int32, (SG_BLOCK, SG_BLOCK), 0) // CHUNK
    ci = lax.broadcasted_iota(jnp.int32, (SG_BLOCK, SG_BLOCK), 1) // CHUNK
    causal = ri >= ci
    wm = [jnp.where(causal, w_ref[gi], 0.0).astype(BF16) for gi in range(SG_GROUPS)]
    bias = bias_ref[...]
    for n in range(nblk):
        rs = slice(n * SG_BLOCK, (n + 1) * SG_BLOCK)
        v = _gelu(v_ref[rs, :].astype(F32))
        mu = jnp.mean(v, axis=-1, keepdims=True)
        vc = v - mu
        var = jnp.mean(vc * vc, axis=-1, keepdims=True)
        vn = (vc * lax.rsqrt(var + EPS) * g_ref[...] + b_ref[...]).astype(BF16)
        u = _gelu(u_ref[rs, :].astype(F32))
        for gi in range(SG_GROUPS):
            cs = slice(gi * SG_GW, (gi + 1) * SG_GW)
            sv = jnp.dot(wm[gi], vn[:, cs], preferred_element_type=F32) + bias[:, cs]
            o_ref[rs, cs] = (u[:, cs] * sv).astype(o_ref.dtype)


def _sg_mixer(seg_a, ln_g, ln_b, sg_w, bias_tile, l, m):
    tb = min(512, m)
    ub, vb = C_U // SG_WIDTH, C_V // SG_WIDTH
    return pl.pallas_call(
        functools.partial(_sg_kernel, nblk=tb // SG_BLOCK),
        grid=(m // tb,),
        in_specs=[pl.BlockSpec((tb, SG_WIDTH), lambda i: (i, ub)),
                  pl.BlockSpec((tb, SG_WIDTH), lambda i: (i, vb)),
                  pl.BlockSpec((None, 1, SG_WIDTH), lambda i: (l, 0, 0)),
                  pl.BlockSpec((None, 1, SG_WIDTH), lambda i: (l, 0, 0)),
                  pl.BlockSpec((None, SG_GROUPS, SG_BLOCK, SG_BLOCK), lambda i: (l, 0, 0, 0)),
                  pl.BlockSpec((None, SG_BLOCK, SG_WIDTH), lambda i: (l, 0, 0))],
        out_specs=pl.BlockSpec((tb, SG_WIDTH), lambda i: (i, 0)),
        out_shape=jax.ShapeDtypeStruct((m, SG_WIDTH), BF16),
        compiler_params=_cparams(("parallel",)),
        name="sg_mixer",
    )(seg_a, seg_a, ln_g.reshape(-1, 1, SG_WIDTH), ln_b.reshape(-1, 1, SG_WIDTH), sg_w, bias_tile)


def _ssd_kernel(z_ref, xc_ref, xp_ref, bc_ref, bp_ref, dt_ref,
                cwx_ref, cwb_ref, cbx_ref, cbb_ref, dtb_ref, aexp_ref, dexp_ref, ng_ref, e_ref,
                o_ref, st_ref):
    c = pl.program_id(1)

    @pl.when(c == 0)
    def _():
        st_ref[...] = jnp.zeros(st_ref.shape, F32)

    has_prev = c > 0
    srow = lax.broadcasted_iota(jnp.int32, (3 * CHUNK, 2 * CHUNK), 0)
    scol = lax.broadcasted_iota(jnp.int32, (3 * CHUNK, 2 * CHUNK), 1)
    shift = jnp.where(scol == CHUNK + (srow % CHUNK) - (3 - srow // CHUNK), 1.0, 0.0).astype(BF16)

    def conv(cur_ref, prev_ref, w_ref, b_ref):
        cur = cur_ref[...]
        prev = jnp.where(has_prev, prev_ref[...], jnp.zeros_like(cur))
        both = jnp.concatenate([prev, cur], axis=0)
        sh = jnp.dot(shift, both, preferred_element_type=F32)
        w = w_ref[...]
        acc = cur.astype(F32) * w[3:4, :] + b_ref[...]
        for k in range(SSM_CONV - 1):
            acc = acc + sh[k * CHUNK:(k + 1) * CHUNK, :] * w[k:k + 1, :]
        return _silu(acc)

    xs = conv(xc_ref, xp_ref, cwx_ref, cbx_ref)
    bcv = conv(bc_ref, bp_ref, cwb_ref, cbb_ref)
    gn = SSM_GROUPS * SSM_STATE
    bm = bcv[:, :gn].astype(BF16)
    cm = bcv[:, gn:].astype(BF16)

    hi = lax.Precision.HIGHEST
    lane = lax.broadcasted_iota(jnp.int32, (CHUNK, LANE), 1)
    dt = jnp.where(lane < SSM_HEADS, _softplus(dt_ref[...] + dtb_ref[...]), 0.0)
    dt_e = jnp.dot(dt, e_ref[...], precision=hi, preferred_element_type=F32)
    a_e = dt_e * aexp_ref[...]
    r64 = lax.broadcasted_iota(jnp.int32, (CHUNK, CHUNK), 0)
    c64 = lax.broadcasted_iota(jnp.int32, (CHUNK, CHUNK), 1)
    tri = jnp.where(c64 <= r64, 1.0, 0.0).astype(F32)
    a_cs = jnp.dot(tri, a_e, precision=hi, preferred_element_type=F32)
    rl = lax.broadcasted_iota(jnp.int32, (CHUNK, SSM_INNER), 0)
    cl = lax.broadcasted_iota(jnp.int32, (CHUNK, SSM_INNER), 1) % SSM_HEADDIM
    diag = jnp.where(rl == cl, a_cs, 0.0)
    ones = jnp.ones((CHUNK, CHUNK), F32)
    a_row = jnp.dot(ones, diag, precision=hi, preferred_element_type=F32)
    decay = jnp.exp(jnp.where(rl >= cl, a_cs - a_row, NEG_BIG))
    a_last = a_cs[CHUNK - 1:CHUNK, :]
    xd = xs * dt_e
    xe = (xd * jnp.exp(a_last - a_cs)).astype(BF16)
    xdb = xd.astype(BF16)
    ea = jnp.exp(a_cs)
    cdec = jnp.exp(a_last)

    br = lax.broadcasted_iota(jnp.int32, (SSM_GW, SSM_GW), 0) // SSM_HEADDIM
    bc_ = lax.broadcasted_iota(jnp.int32, (SSM_GW, SSM_GW), 1) // SSM_HEADDIM
    blockdiag = br == bc_
    nt = (((1,), (1,)), ((), ()))
    tn = (((0,), (0,)), ((), ()))
    ys = []
    for g in range(SSM_GROUPS):
        ns = slice(g * SSM_STATE, (g + 1) * SSM_STATE)
        ls = slice(g * SSM_GW, (g + 1) * SSM_GW)
        cg, bg = cm[:, ns], bm[:, ns]
        b_t = jnp.concatenate([bg] * SSM_HG, axis=0)
        cb = lax.dot_general(cg, b_t, nt, preferred_element_type=F32)
        mg = (cb * decay[:, ls]).astype(BF16)
        xg = xdb[:, ls]
        bd = jnp.where(blockdiag, jnp.concatenate([xg] * SSM_HG, axis=0), jnp.zeros((), BF16))
        y_diag = jnp.dot(mg, bd, preferred_element_type=F32)
        st = st_ref[g]
        y_off = jnp.dot(cg, st.astype(BF16), preferred_element_type=F32) * ea[:, ls]
        upd = lax.dot_general(bg, xe[:, ls], tn, preferred_element_type=F32)
        st_ref[g] = st * cdec[:, ls] + upd
        ys.append(y_diag + y_off)
    y = jnp.concatenate(ys, axis=1) + xs * dexp_ref[...]
    y = y * _silu(z_ref[...].astype(F32))
    outs = []
    for g in range(SSM_GROUPS):
        yg = y[:, g * SSM_GW:(g + 1) * SSM_GW]
        ms = jnp.mean(yg * yg, axis=-1, keepdims=True)
        outs.append(yg * lax.rsqrt(ms + EPS))
    o_ref[...] = (jnp.concatenate(outs, axis=1) * ng_ref[...]).astype(o_ref.dtype)


def _ssd_mixer(seg_a, seg_s, conv_w, conv_b, dtb_pad, a_exp, d_exp, norm_g, e_mat, l, bsz, seq):
    nc = seq // CHUNK
    zb, xb, bb = C_Z // SSM_INNER, C_XBC // SSM_INNER, C_XBC // SSM_INNER + 1

    def row(b, c):
        return b * nc + c

    def prow(b, c):
        return b * nc + jnp.maximum(c - 1, 0)

    vec = lambda blk: pl.BlockSpec((None, 1, SSM_INNER), lambda b, c: (l, 0, blk))
    return pl.pallas_call(
        _ssd_kernel,
        grid=(bsz, nc),
        in_specs=[pl.BlockSpec((CHUNK, SSM_INNER), lambda b, c: (row(b, c), zb)),
                  pl.BlockSpec((CHUNK, SSM_INNER), lambda b, c: (row(b, c), xb)),
                  pl.BlockSpec((CHUNK, SSM_INNER), lambda b, c: (prow(b, c), xb)),
                  pl.BlockSpec((CHUNK, SSM_INNER), lambda b, c: (row(b, c), bb)),
                  pl.BlockSpec((CHUNK, SSM_INNER), lambda b, c: (prow(b, c), bb)),
                  pl.BlockSpec((CHUNK, LANE), lambda b, c: (row(b, c), 1)),
                  pl.BlockSpec((None, SSM_CONV, SSM_INNER), lambda b, c: (l, 0, 0)),
                  pl.BlockSpec((None, SSM_CONV, SSM_INNER), lambda b, c: (l, 0, 1)),
                  vec(0), vec(1),
                  pl.BlockSpec((None, 1, LANE), lambda b, c: (l, 0, 0)),
                  vec(0), vec(0), vec(0),
                  pl.BlockSpec((LANE, SSM_INNER), lambda b, c: (0, 0))],
        out_specs=pl.BlockSpec((CHUNK, SSM_INNER), lambda b, c: (row(b, c), 0)),
        out_shape=jax.ShapeDtypeStruct((bsz * seq, SSM_INNER), BF16),
        scratch_shapes=[pltpu.VMEM((SSM_GROUPS, SSM_STATE, SSM_GW), F32)],
        compiler_params=_cparams(("parallel", "arbitrary")),
        name="ssd_mixer",
    )(seg_a, seg_a, seg_a, seg_a, seg_a, seg_s, conv_w, conv_w,
      conv_b.reshape(-1, 1, 2 * SSM_INNER), conv_b.reshape(-1, 1, 2 * SSM_INNER),
      dtb_pad, a_exp, d_exp, norm_g.reshape(-1, 1, SSM_INNER), e_mat)


def _rope128(x, cos, sin_signed):
    return x * cos + pltpu.roll(x, ATT_HEADDIM // 2, 1) * sin_signed


def _rope64(x, cos, sin_signed):
    lane = lax.broadcasted_iota(jnp.int32, x.shape, 1)
    low = (lane % IDX_HEADDIM) < IDX_HEADDIM // 2
    rot = jnp.where(low, pltpu.roll(x, LANE - IDX_HEADDIM // 2, 1), pltpu.roll(x, IDX_HEADDIM // 2, 1))
    return x * cos + rot * sin_signed


def _dsa_kernel(q_ref, k_ref, v_ref, qi_ref, ki_ref, wi_ref,
                cq_ref, sq_ref, ck_ref, sk_ref, ciq_ref, siq_ref, cik_ref, sik_ref, *rest,
                klen, q0, topk):
    o_ref, kr_ref, vb_ref, kir_ref, key_ref, bias_ref, pos_ref = rest[-7:]
    i = pl.program_id(1)
    nt = (((1,), (1,)), ((), ()))
    nlc = klen // LANE

    @pl.when(i == 0)
    def _():
        for kv in range(ATT_KV_HEADS):
            hs = slice(kv * ATT_HEADDIM, (kv + 1) * ATT_HEADDIM)
            kr_ref[:, hs] = _rope128(k_ref[:, hs], ck_ref[...], sk_ref[...]).astype(BF16)
        vb_ref[...] = v_ref[...].astype(BF16)
        kx = _rope64(ki_ref[...], cik_ref[...], sik_ref[...])
        kx_hi = kx.astype(BF16)
        kir_ref[:, 0:LANE] = kx_hi
        kir_ref[:, LANE:2 * LANE] = (kx - kx_hi.astype(F32)).astype(BF16)

    lane_q = lax.broadcasted_iota(jnp.int32, (Q_BLOCK, LANE), 1)
    kir = kir_ref[...]
    wi = wi_ref[...] * np.float32(IDX_HEADS ** -0.5)
    iscore = jnp.zeros((Q_BLOCK, klen), F32)
    for quad in range(IDX_HEADS // 4):
        parts = []
        for pair in range(2 * quad, 2 * quad + 2):
            ps = slice(pair * LANE, (pair + 1) * LANE)
            qp = _rope64(qi_ref[:, ps], ciq_ref[...], siq_ref[...])
            q_hi = qp.astype(BF16).astype(F32)
            q_lo_swapped = pltpu.roll(qp - q_hi, IDX_HEADDIM, 1)
            for sub in range(2):
                own = (lane_q // IDX_HEADDIM) == sub
                parts.append(jnp.concatenate([jnp.where(own, q_hi, q_lo_swapped), jnp.where(own, q_hi, 0.0)],
                                             axis=1).astype(BF16))
        logits = lax.dot_general(jnp.concatenate(parts, axis=0), kir, nt,
                                 preferred_element_type=F32)
        for hh in range(4):
            h = 4 * quad + hh
            wcol = wi[:, S_WI_LANE + h:S_WI_LANE + h + 1]
            iscore = iscore + jnp.maximum(logits[hh * Q_BLOCK:(hh + 1) * Q_BLOCK, :], 0.0) * wcol

    qchunk = (lax.broadcasted_iota(jnp.int32, (Q_BLOCK, klen), 0) + (q0 + i) * Q_BLOCK) // CHUNK
    kchunk = lax.broadcasted_iota(jnp.int32, (Q_BLOCK, klen), 1) // CHUNK
    iscore = jnp.where(iscore == 0.0, 0.0, iscore)
    bits = pltpu.bitcast(iscore, jnp.int32)
    key = jnp.where(bits < 0, bits ^ jnp.int32(0x7FFFFFFF), bits)
    key = jnp.maximum(key, jnp.int32(INT_MIN + 1))
    key_ref[...] = jnp.where(kchunk <= qchunk, key, jnp.int32(INT_MIN))

    def row_count(pred, rows=slice(0, Q_BLOCK)):
        nrow = rows.stop - rows.start
        acc = jnp.zeros((nrow, LANE), F32)
        for cidx in range(nlc):
            acc = acc + jnp.where(pred(key_ref[rows, cidx * LANE:(cidx + 1) * LANE], cidx), 1.0, 0.0)
        return jnp.broadcast_to(jnp.sum(acc, axis=-1, keepdims=True), (nrow, LANE))

    halves = (slice(0, Q_BLOCK // 2), slice(Q_BLOCK // 2, Q_BLOCK))

    def thr_step(it, t_us):
        bit = jnp.left_shift(jnp.int32(1), 31 - it)
        out = []
        for rows, t_u in zip(halves, t_us):
            cand_u = t_u | bit
            cand = cand_u ^ jnp.int32(INT_MIN)
            out.append(jnp.where(row_count(lambda kc, _: kc >= cand, rows) >= topk, cand_u, t_u))
        return tuple(out)

    t_us = lax.fori_loop(0, 32, thr_step, tuple(jnp.zeros((Q_BLOCK // 2, LANE), jnp.int32) for _ in halves),
                         unroll=8)
    thr = jnp.concatenate(t_us, axis=0) ^ jnp.int32(INT_MIN)
    cnt_ge = row_count(lambda kc, _: kc >= thr)
    cnt_gt = row_count(lambda kc, _: kc > thr)
    need = topk - cnt_gt
    excess = jnp.where(thr > INT_MIN, cnt_ge - topk, 0.0)
    pos_ref[...] = jnp.full((Q_BLOCK, LANE), klen, jnp.int32)
    lane_pos = lax.broadcasted_iota(jnp.int32, (Q_BLOCK, LANE), 1)

    @pl.when(jnp.max(excess) > 0.0)
    def _():
        nbits = int(klen - 1).bit_length()
        never = jnp.int32(1 << 30)

        def pos_step(it, bound):
            cand = bound | jnp.left_shift(jnp.int32(1), nbits - 1 - it)
            ties = row_count(lambda kc, cidx: jnp.where(kc == thr, lane_pos + cidx * LANE, never) < cand)
            return jnp.where(ties < need, cand, bound)

        bound = lax.fori_loop(0, nbits, pos_step, jnp.zeros((Q_BLOCK, LANE), jnp.int32))
        pos_ref[...] = bound + 1

    pos = pos_ref[...]
    for cidx in range(nlc):
        cs = slice(cidx * LANE, (cidx + 1) * LANE)
        kc = key_ref[:, cs]
        tie = jnp.where(lane_pos + cidx * LANE < pos, 0.0, NEG_BIG)
        sel = jnp.where(kc > thr, 0.0, jnp.where(kc == thr, tie, NEG_BIG))
        bias_ref[:, cs] = jnp.where(kc == INT_MIN, NEG_BIG, sel)

    scale = np.float32(ATT_HEADDIM ** -0.5)
    for kv in range(ATT_KV_HEADS):
        hs = slice(kv * ATT_HEADDIM, (kv + 1) * ATT_HEADDIM)
        krh = kr_ref[:, hs]
        vh = vb_ref[:, hs]
        heads = [slice((kv * ATT_GRP + gq) * ATT_HEADDIM, (kv * ATT_GRP + gq + 1) * ATT_HEADDIM)
                 for gq in range(ATT_GRP)]
        qg = jnp.concatenate([_rope128(q_ref[:, qs], cq_ref[...], sq_ref[...]).astype(BF16) for qs in heads],
                             axis=0)
        s = lax.dot_general(qg, krh, nt, preferred_element_type=F32)
        es, dens = [], []
        for gq in range(ATT_GRP):
            sg = s[gq * Q_BLOCK:(gq + 1) * Q_BLOCK, :] * scale + bias_ref[...]
            e = jnp.exp(sg - jnp.max(sg, axis=-1, keepdims=True))
            dens.append(jnp.sum(e, axis=-1, keepdims=True))
            es.append(e.astype(BF16))
        o = jnp.dot(jnp.concatenate(es, axis=0), vh, preferred_element_type=F32)
        for gq, qs in enumerate(heads):
            o_ref[:, qs] = (o[gq * Q_BLOCK:(gq + 1) * Q_BLOCK, :] / dens[gq]).astype(o_ref.dtype)


DSA_BUCKETS = 4


def _dsa_mixer(seg_q, seg_s, tabs, bsz, seq):
    nqb = seq // Q_BLOCK
    topk = min(IDX_TOPK, seq // 4)
    cos128, sin128, cos64, sin64 = tabs
    kb, vb_, qib = 1024 // 256, 1280 // 256, 1536 // 512
    nbk = DSA_BUCKETS if nqb % DSA_BUCKETS == 0 else 1
    qpb = nqb // nbk
    width = ATT_HEADS * ATT_HEADDIM
    kvw = ATT_KV_HEADS * ATT_HEADDIM
    out = None
    for u in range(nbk):
        q0 = u * qpb
        klen = (u + 1) * qpb * Q_BLOCK
        qtab = pl.BlockSpec((Q_BLOCK, LANE), lambda b, i, q0=q0: (q0 + i, 0))
        ktab = pl.BlockSpec((klen, LANE), lambda b, i: (0, 0))
        qrow = lambda blk, q0=q0: (lambda b, i: (b, q0 + i, blk))
        in_specs = [pl.BlockSpec((None, Q_BLOCK, width), qrow(0)),
                    pl.BlockSpec((None, klen, kvw), lambda b, i: (b, 0, kb)),
                    pl.BlockSpec((None, klen, kvw), lambda b, i: (b, 0, vb_)),
                    pl.BlockSpec((None, Q_BLOCK, IDX_HEADS * IDX_HEADDIM), qrow(qib)),
                    pl.BlockSpec((None, klen, LANE), lambda b, i: (b, 0, 0)),
                    pl.BlockSpec((None, Q_BLOCK, LANE), qrow(1)),
                    qtab, qtab, ktab, ktab, qtab, qtab, ktab, ktab]
        args = [seg_q, seg_q, seg_q, seg_q, seg_s, seg_s,
                cos128, sin128, cos128, sin128, cos64, sin64, cos64, sin64]
        aliases = {}
        if out is not None:
            in_specs.append(pl.BlockSpec(memory_space=pl.ANY))
            args.append(out)
            aliases = {len(args) - 1: 0}
        out = pl.pallas_call(
            functools.partial(_dsa_kernel, klen=klen, q0=q0, topk=topk),
            grid=(bsz, qpb),
            in_specs=in_specs,
            out_specs=pl.BlockSpec((None, Q_BLOCK, width), qrow(0)),
            out_shape=jax.ShapeDtypeStruct((bsz, seq, width), BF16),
            scratch_shapes=[pltpu.VMEM((klen, kvw), BF16),
                            pltpu.VMEM((klen, kvw), BF16),
                            pltpu.VMEM((klen, 2 * LANE), BF16),
                            pltpu.VMEM((Q_BLOCK, klen), jnp.int32),
                            pltpu.VMEM((Q_BLOCK, klen), F32),
                            pltpu.VMEM((Q_BLOCK, LANE), jnp.int32)],
            input_output_aliases=aliases,
            compiler_params=_cparams(("parallel", "arbitrary")),
            name="dsa_mixer",
        )(*args)
    return out


def _rope_tables(seq):
    pos = jnp.arange(seq, dtype=F32)[:, None]

    def tab(half, reps):
        inv = ROPE_THETA ** (-jnp.arange(half, dtype=F32) / half)
        ang = pos * inv[None, :]
        cos, sin = jnp.cos(ang), jnp.sin(ang)
        return (jnp.tile(jnp.concatenate([cos, cos], axis=1), (1, reps)),
                jnp.tile(jnp.concatenate([-sin, sin], axis=1), (1, reps)))

    cos128, sin128 = tab(ATT_HEADDIM // 2, 1)
    cos64, sin64 = tab(IDX_HEADDIM // 2, 2)
    return cos128, sin128, cos64, sin64


def _merge_kernel(h_ref, wg_ref, ya_ref, yb_ref, yc_ref, yd_ref, p_ref, o_ref):
    h = h_ref[...]
    acc = None
    row = 0
    for i, y_ref in enumerate((ya_ref, yb_ref, yc_ref, yd_ref)):
        width = y_ref.shape[1]
        gate = _sigmoid(jnp.dot(h, wg_ref[i], preferred_element_type=F32))
        term = gate * jnp.dot(y_ref[...], p_ref[row:row + width, :], preferred_element_type=F32)
        acc = term if acc is None else acc + term
        row += width
    o_ref[...] = acc.astype(o_ref.dtype)


def _gated_merge(h, ys, w_gate, w_branch, l, *, tm=1024, tn=256):
    m, d = h.shape
    tm = min(tm, m)
    nbr = len(ys)
    resident = lambda width: pl.BlockSpec((tm, width), lambda i, j: (i, 0), pipeline_mode=pl.Buffered(1))
    return pl.pallas_call(
        _merge_kernel,
        grid=(m // tm, d // tn),
        in_specs=[resident(d),
                  pl.BlockSpec((None, nbr, d, tn), lambda i, j: (l, 0, 0, j))]
                 + [resident(y.shape[1]) for y in ys]
                 + [pl.BlockSpec((None, w_branch.shape[1], tn), lambda i, j: (l, 0, j))],
        out_specs=pl.BlockSpec((tm, tn), lambda i, j: (i, j)),
        out_shape=jax.ShapeDtypeStruct((m, d), BF16),
        compiler_params=_cparams(("parallel", "arbitrary")),
        name="gated_merge",
    )(h, w_gate, *ys, w_branch)


def _xattn_kernel(q_ref, kv_ref, o_ref):
    nt = (((1,), (1,)), ((), ()))
    scale = np.float32(MEM_HEADDIM ** -0.5)
    hw = MEM_HEADS * MEM_HEADDIM
    for h in range(MEM_HEADS):
        hs = slice(h * MEM_HEADDIM, (h + 1) * MEM_HEADDIM)
        s = lax.dot_general(q_ref[:, hs], kv_ref[:, hs], nt, preferred_element_type=F32) * scale
        mx = jnp.max(s, axis=-1, keepdims=True)
        e = jnp.exp(s - mx)
        den = jnp.sum(e, axis=-1, keepdims=True)
        vs = slice(hw + h * MEM_HEADDIM, hw + (h + 1) * MEM_HEADDIM)
        o = jnp.dot(e.astype(BF16), kv_ref[:, vs], preferred_element_type=F32)
        o_ref[:, hs] = (o / den).astype(o_ref.dtype)


def _xattn(q, kv, bsz, seq, mem_len):
    tq = min(512, seq)
    nq = seq // tq
    hw = MEM_HEADS * MEM_HEADDIM
    return pl.pallas_call(
        _xattn_kernel,
        grid=(bsz, nq),
        in_specs=[pl.BlockSpec((tq, hw), lambda b, i: (b * nq + i, 0)),
                  pl.BlockSpec((mem_len, 2 * hw), lambda b, i: (b, 0))],
        out_specs=pl.BlockSpec((tq, hw), lambda b, i: (b * nq + i, 0)),
        out_shape=jax.ShapeDtypeStruct((bsz * seq, hw), BF16),
        compiler_params=_cparams(("parallel", "arbitrary")),
        name="mem_xattn",
    )(q, kv)


def kernel(x, mem, g_ffn1, w_ffn1_in, w_ffn1_out, g_mix, w_in, pool_w, pool_scale, sg_ln_g, sg_ln_b, sg_w, sg_b, ssm_conv_w, ssm_conv_b, ssm_a_log, ssm_dt_bias, ssm_d, ssm_norm_g, w_branch, w_gate, w_out, g_mem, g_cross, w_mem_q, w_mem_kv, w_mem_o, g_ffn2, w_ffn2_in, w_ffn2_out, g_final):
    bsz, seq, d = x.shape
    mem_len = mem.shape[1]
    depth = w_in.shape[0]
    m = bsz * seq
    bf = lambda a: a.astype(BF16)

    w1i, w1o, w2i, w2o = w_ffn1_in, w_ffn1_out, w_ffn2_in, w_ffn2_out
    w_q = bf(w_in[:, :, C_Q:C_KI])
    w_ki = w_in[:, :, C_KI:C_WI]
    w_s = bf(jnp.concatenate(
        [w_ki, w_ki, w_in[:, :, C_DT:C_Q], w_in[:, :, C_WI:],
         jnp.zeros((depth, d, LANE - SSM_HEADS - IDX_HEADS), F32)], axis=2))
    wg, wb, wo = bf(w_gate), bf(w_branch), bf(w_out)
    wmkv, wmo = w_mem_kv, w_mem_o
    pw = bf(pool_w)

    expand = lambda v: jnp.repeat(v, SSM_HEADDIM, axis=-1).reshape(depth, 1, SSM_INNER)
    a_exp = expand(-jnp.exp(ssm_a_log))
    d_exp = expand(ssm_d)
    dtb_pad = jnp.pad(ssm_dt_bias, ((0, 0), (0, LANE - SSM_HEADS))).reshape(depth, 1, LANE)
    e_mat = (jnp.arange(LANE)[:, None] == (jnp.arange(SSM_INNER)[None, :] // SSM_HEADDIM)).astype(F32)
    bias_tile = jnp.repeat(jnp.swapaxes(sg_b, 1, 2), SG_GW, axis=2)
    tabs = _rope_tables(seq)

    x2 = x.reshape(m, d)
    mem_n = _rmsnorm(mem.reshape(bsz * mem_len, d), g_mem, BF16)

    for l in range(depth):
        x2 = _ffn(x2, g_ffn1[l], w1i, w1o, l)

        h = _rmsnorm(x2, g_mix[l], BF16)
        seg_a = _matmul(h, w_in, (l,), SEG_A, tn=512, out_dtype=BF16)
        seg_q = _matmul(h, w_q, (l,), SEG_Q, tn=1024, out_dtype=F32)
        seg_s = _matmul(h, w_s, (l,), SEG_S, tn=256, out_dtype=F32)
        y_a = _pool_mixer(seg_a, pw, pool_scale, l, bsz, seq)
        y_b = _sg_mixer(seg_a, sg_ln_g, sg_ln_b, sg_w, bias_tile, l, m)
        y_c = _ssd_mixer(seg_a, seg_s, ssm_conv_w, ssm_conv_b, dtb_pad, a_exp, d_exp,
                         ssm_norm_g, e_mat, l, bsz, seq)
        y_d = _dsa_mixer(seg_q.reshape(bsz, seq, SEG_Q), seg_s.reshape(bsz, seq, SEG_S), tabs, bsz, seq)
        merged = _gated_merge(h, (y_a, y_b, y_c, y_d.reshape(m, -1)), wg, wb, l)
        x2 = _matmul_res(merged, wo, (l,), x2, 1.0)

        q = _norm_matmul(x2, g_cross[l], w_mem_q, (l,))
        kv = _matmul(mem_n, wmkv, (l,), 2 * MEM_HEADS * MEM_HEADDIM, out_dtype=BF16)
        att = _xattn(q, kv, bsz, seq, mem_len)
        x2, h2 = _matmul_res_norm(att, wmo, (l,), x2, g_ffn2[l])

        x2 = _ffn(x2, g_ffn2[l], w2i, w2o, l, h=h2)

    return _rmsnorm(x2, g_final, F32).reshape(bsz, seq, d)
```

```python
import functools

import jax
import jax.numpy as jnp
import numpy as np
from jax import lax
from jax.experimental import pallas as pl
from jax.experimental.pallas import tpu as pltpu

F32 = jnp.float32
BF16 = jnp.bfloat16

D_MODEL = 4096
FFN_DIM = 8192
CHUNK = 64
EPS = 1e-6
ROPE_THETA = 10000.0

POOL_WINDOWS = (2, 4, 8, 16)
POOL_GROUPS = 4
POOL_WIDTH = 2048
POOL_GW = POOL_WIDTH // POOL_GROUPS
POOL_PAD = 16

SG_WIDTH = 1024
SG_BLOCK = 128
SG_GROUPS = 4
SG_GW = SG_WIDTH // SG_GROUPS

SSM_HEADS = 16
SSM_HEADDIM = 64
SSM_INNER = SSM_HEADS * SSM_HEADDIM
SSM_GROUPS = 4
SSM_STATE = 128
SSM_CONV = 4
SSM_HG = SSM_HEADS // SSM_GROUPS
SSM_GW = SSM_INNER // SSM_GROUPS

ATT_HEADS = 8
ATT_KV_HEADS = 2
ATT_HEADDIM = 128
ATT_GRP = ATT_HEADS // ATT_KV_HEADS
IDX_HEADS = 8
IDX_HEADDIM = 64
IDX_TOPK = 256
Q_BLOCK = 128

MEM_HEADS = 4
MEM_HEADDIM = 128

C_POOL, C_U, C_V, C_Z, C_XBC = 0, 2048, 3072, 4096, 5120
C_DT, C_Q, C_K, C_VAL, C_QI, C_KI, C_WI = 7168, 7184, 8208, 8464, 8720, 9232, 9296
SEG_A = 7168
SEG_Q = 2048
SEG_S = 256
S_DT_LANE = 0
S_WI_LANE = 16

LANE = 128
VMEM_LIMIT = 56 * 1024 * 1024
NEG_BIG = -1e30
INT_MIN = -2147483648


def _cparams(sem):
    return pltpu.CompilerParams(dimension_semantics=sem, vmem_limit_bytes=VMEM_LIMIT)


def _sigmoid(x):
    return 1.0 / (1.0 + jnp.exp(-x))


def _silu(x):
    return x * _sigmoid(x)


def _gelu(x):
    return 0.5 * x * (1.0 + lax.erf(x * np.float32(1.0 / np.sqrt(2.0))))


def _softplus(x):
    return jnp.maximum(x, 0.0) + jnp.log1p(jnp.exp(-jnp.abs(x)))


def _rmsnorm_kernel(x_ref, g_ref, o_ref):
    x = x_ref[...]
    ms = jnp.mean(x * x, axis=-1, keepdims=True)
    o_ref[...] = (x * lax.rsqrt(ms + EPS) * g_ref[...]).astype(o_ref.dtype)


def _rmsnorm(x, g, out_dtype):
    m, d = x.shape
    tm = min(512, m)
    return pl.pallas_call(
        _rmsnorm_kernel,
        grid=(m // tm,),
        in_specs=[pl.BlockSpec((tm, d), lambda i: (i, 0)),
                  pl.BlockSpec((1, d), lambda i: (0, 0))],
        out_specs=pl.BlockSpec((tm, d), lambda i: (i, 0)),
        out_shape=jax.ShapeDtypeStruct((m, d), out_dtype),
        compiler_params=_cparams(("parallel",)),
        name="rmsnorm",
    )(x, g.reshape(1, d))


def _mm_kernel(x_ref, w_ref, o_ref):
    o_ref[...] = jnp.dot(x_ref[...], w_ref[...].astype(BF16), preferred_element_type=F32).astype(o_ref.dtype)


def _mm_res_kernel(x_ref, w_ref, r_ref, o_ref, *, alpha):
    acc = jnp.dot(x_ref[...], w_ref[...].astype(BF16), preferred_element_type=F32)
    o_ref[...] = r_ref[...] + alpha * acc


def _mm_res_acc_kernel(x_ref, w_ref, r_ref, o_ref, acc_ref, *, alpha, nk):
    k = pl.program_id(2)
    part = jnp.dot(x_ref[...], w_ref[...].astype(BF16), preferred_element_type=F32)

    @pl.when(k == 0)
    def _():
        acc_ref[...] = part

    @pl.when(k > 0)
    def _():
        acc_ref[...] += part

    @pl.when(k == nk - 1)
    def _():
        o_ref[...] = r_ref[...] + alpha * acc_ref[...]


def _w_spec(w, widx, kblk, tn, col_blk0, kgrid):
    lead = (None,) * len(widx)
    if kgrid:
        return pl.BlockSpec(lead + (kblk, tn), lambda i, j, k: (*widx, k, j + col_blk0))
    return pl.BlockSpec(lead + (kblk, tn), lambda i, j: (*widx, 0, j + col_blk0))


def _matmul(x, w, widx, n, *, col0=0, tm=1024, tn=512, out_dtype=BF16):
    m, kdim = x.shape
    tm, tn = min(tm, m), min(tn, n)
    return pl.pallas_call(
        _mm_kernel,
        grid=(m // tm, n // tn),
        in_specs=[pl.BlockSpec((tm, kdim), lambda i, j: (i, 0)),
                  _w_spec(w, widx, kdim, tn, col0 // tn, False)],
        out_specs=pl.BlockSpec((tm, tn), lambda i, j: (i, j)),
        out_shape=jax.ShapeDtypeStruct((m, n), out_dtype),
        compiler_params=_cparams(("parallel", "arbitrary")),
        name="matmul",
    )(x, w)


def _mm_nt_kernel(x_ref, wt_ref, o_ref):
    nt = (((1,), (1,)), ((), ()))
    o_ref[...] = lax.dot_general(x_ref[...], wt_ref[...].astype(BF16), nt,
                                 preferred_element_type=F32).astype(o_ref.dtype)


def _matmul_nt(x, wt, l, n, *, tm=1024, tn=512, out_dtype=BF16):
    m, kdim = x.shape
    tm, tn = min(tm, m), min(tn, n)
    return pl.pallas_call(
        _mm_nt_kernel,
        grid=(m // tm, n // tn),
        in_specs=[pl.BlockSpec((tm, kdim), lambda i, j: (i, 0)),
                  pl.BlockSpec((None, tn, kdim), lambda i, j: (l, j, 0))],
        out_specs=pl.BlockSpec((tm, tn), lambda i, j: (i, j)),
        out_shape=jax.ShapeDtypeStruct((m, n), out_dtype),
        compiler_params=_cparams(("parallel", "arbitrary")),
        name="matmul_nt",
    )(x, wt)


def _x_spec(tm, kdim, resident=False):
    if resident:
        return pl.BlockSpec((tm, kdim), lambda i, j: (i, 0), pipeline_mode=pl.Buffered(1))
    return pl.BlockSpec((tm, kdim), lambda i, j: (i, 0))


def _matmul_res(x, w, widx, res, alpha, *, tm=1024, tn=512, tk=None, resident_x=False):
    m, kdim = x.shape
    n = res.shape[1]
    tm, tn = min(tm, m), min(tn, n)
    if tk is None or tk >= kdim:
        return pl.pallas_call(
            functools.partial(_mm_res_kernel, alpha=alpha),
            grid=(m // tm, n // tn),
            in_specs=[_x_spec(tm, kdim, resident_x),
                      _w_spec(w, widx, kdim, tn, 0, False),
                      pl.BlockSpec((tm, tn), lambda i, j: (i, j))],
            out_specs=pl.BlockSpec((tm, tn), lambda i, j: (i, j)),
            out_shape=jax.ShapeDtypeStruct((m, n), F32),
            compiler_params=_cparams(("parallel", "arbitrary")),
            name="matmul_res",
        )(x, w, res)
    nk = kdim // tk
    return pl.pallas_call(
        functools.partial(_mm_res_acc_kernel, alpha=alpha, nk=nk),
        grid=(m // tm, n // tn, nk),
        in_specs=[pl.BlockSpec((tm, tk), lambda i, j, k: (i, k)),
                  _w_spec(w, widx, tk, tn, 0, True),
                  pl.BlockSpec((tm, tn), lambda i, j, k: (i, j))],
        out_specs=pl.BlockSpec((tm, tn), lambda i, j, k: (i, j)),
        out_shape=jax.ShapeDtypeStruct((m, n), F32),
        scratch_shapes=[pltpu.VMEM((tm, tn), F32)],
        compiler_params=_cparams(("parallel", "arbitrary", "arbitrary")),
        name="matmul_res_acc",
    )(x, w, res)


def _mm_res_norm_kernel(x_ref, w_ref, r_ref, g_ref, o_ref, h_ref):
    y = r_ref[...] + jnp.dot(x_ref[...], w_ref[...].astype(BF16), preferred_element_type=F32)
    o_ref[...] = y
    ms = jnp.mean(y * y, axis=-1, keepdims=True)
    h_ref[...] = (y * lax.rsqrt(ms + EPS) * g_ref[...]).astype(h_ref.dtype)


def _matmul_res_norm(x, w, widx, res, g, *, tm=256):
    m, kdim = x.shape
    n = res.shape[1]
    tm = min(tm, m)
    lead = (None,) * len(widx)
    row = pl.BlockSpec((tm, n), lambda i: (i, 0))
    return pl.pallas_call(
        _mm_res_norm_kernel,
        grid=(m // tm,),
        in_specs=[pl.BlockSpec((tm, kdim), lambda i: (i, 0)),
                  pl.BlockSpec(lead + (kdim, n), lambda i: (*widx, 0, 0)),
                  row,
                  pl.BlockSpec((1, n), lambda i: (0, 0))],
        out_specs=[row, row],
        out_shape=[jax.ShapeDtypeStruct((m, n), F32), jax.ShapeDtypeStruct((m, n), BF16)],
        compiler_params=_cparams(("parallel",)),
        name="matmul_res_norm",
    )(x, w, res, g.reshape(1, n))


def _norm_mm_kernel(x_ref, g_ref, w_ref, o_ref):
    x = x_ref[...]
    ms = jnp.mean(x * x, axis=-1, keepdims=True)
    xn = (x * lax.rsqrt(ms + EPS) * g_ref[...]).astype(BF16)
    o_ref[...] = jnp.dot(xn, w_ref[...].astype(BF16), preferred_element_type=F32).astype(o_ref.dtype)


def _norm_matmul(x, g, w, widx, *, tm=512, out_dtype=BF16):
    m, d = x.shape
    n = w.shape[-1]
    tm = min(tm, m)
    lead = (None,) * len(widx)
    return pl.pallas_call(
        _norm_mm_kernel,
        grid=(m // tm,),
        in_specs=[pl.BlockSpec((tm, d), lambda i: (i, 0)),
                  pl.BlockSpec((1, d), lambda i: (0, 0)),
                  pl.BlockSpec(lead + (d, n), lambda i: (*widx, 0, 0))],
        out_specs=pl.BlockSpec((tm, n), lambda i: (i, 0)),
        out_shape=jax.ShapeDtypeStruct((m, n), out_dtype),
        compiler_params=_cparams(("parallel",)),
        name="norm_matmul",
    )(x, g.reshape(1, d), w)


def _swiglu_kernel(x_ref, wg_ref, wu_ref, o_ref):
    x = x_ref[...]
    g = jnp.dot(x, wg_ref[...].astype(BF16), preferred_element_type=F32)
    u = jnp.dot(x, wu_ref[...].astype(BF16), preferred_element_type=F32)
    o_ref[...] = (_silu(g) * u).astype(o_ref.dtype)


def _swiglu_in(x, w, l, *, tm=1024):
    m, kdim = x.shape
    f = w.shape[-1] // 2
    resident = tm > 1024 and m >= tm
    tm = min(tm, m)
    tn = 256 if (w.dtype == F32 or resident) else 512
    nb = f // tn
    return pl.pallas_call(
        _swiglu_kernel,
        grid=(m // tm, nb),
        in_specs=[_x_spec(tm, kdim, resident),
                  pl.BlockSpec((None, kdim, tn), lambda i, j: (l, 0, j)),
                  pl.BlockSpec((None, kdim, tn), lambda i, j: (l, 0, j + nb))],
        out_specs=pl.BlockSpec((tm, tn), lambda i, j: (i, j)),
        out_shape=jax.ShapeDtypeStruct((m, f), BF16),
        compiler_params=_cparams(("parallel", "arbitrary")),
        name="swiglu_in",
    )(x, w, w)


def _ffn(x, g, w_in, w_out, l, *, h=None, tm_in=1024, tn_out=256):
    if h is None:
        h = _rmsnorm(x, g, BF16)
    act = _swiglu_in(h, w_in, l, tm=tm_in)
    return _matmul_res(act, w_out, (l,), x, 0.5, tm=1024, tn=tn_out, resident_x=tn_out > 256)


def _pool_kernel(a_ref, w_ref, s_ref, o_ref, pad_ref, *, seq, rows):
    g = pl.program_id(1)
    pad_ref[0:POOL_PAD, :] = jnp.zeros((POOL_PAD, POOL_GW), F32)
    pad_ref[POOL_PAD:POOL_PAD + seq, :] = a_ref[...].astype(F32)
    w = w_ref[...]
    scale = s_ref[...]
    for gi, win in enumerate(POOL_WINDOWS):

        @pl.when(g == gi)
        def _(win=win):
            for c in range(seq // rows):
                r0 = POOL_PAD + c * rows
                cur = pad_ref[r0:r0 + rows, :]
                tot = cur
                for k in range(1, win):
                    tot = tot + pad_ref[r0 - k:r0 - k + rows, :]
                t1 = lax.broadcasted_iota(jnp.int32, (rows, POOL_GW), 0) + (c * rows + 1)
                cnt = jnp.minimum(t1, win).astype(F32)
                mixed = (tot / cnt - cur).astype(BF16)
                y = jnp.dot(mixed, w, preferred_element_type=F32) * scale
                o_ref[c * rows:(c + 1) * rows, :] = y.astype(o_ref.dtype)


def _pool_mixer(seg_a, pool_w, pool_scale, l, bsz, seq):
    rows = min(256, seq)
    return pl.pallas_call(
        functools.partial(_pool_kernel, seq=seq, rows=rows),
        grid=(bsz, POOL_GROUPS),
        in_specs=[pl.BlockSpec((seq, POOL_GW), lambda b, g: (b, g)),
                  pl.BlockSpec((None, None, POOL_GW, POOL_GW), lambda b, g: (l, g, 0, 0)),
                  pl.BlockSpec((None, None, 1, POOL_GW), lambda b, g: (l, g, 0, 0))],
        out_specs=pl.BlockSpec((seq, POOL_GW), lambda b, g: (b, g)),
        out_shape=jax.ShapeDtypeStruct((bsz * seq, POOL_WIDTH), BF16),
        scratch_shapes=[pltpu.VMEM((POOL_PAD + seq, POOL_GW), F32)],
        compiler_params=_cparams(("parallel", "arbitrary")),
        name="pool_mixer",
    )(seg_a, pool_w, pool_scale.reshape(pool_scale.shape[0], POOL_GROUPS, 1, POOL_GW))


def _sg_kernel(u_ref, v_ref, g_ref, b_ref, w_ref, bias_ref, o_ref, *, nblk):
    ri = lax.broadcasted_iota(jnp.int32, (SG_BLOCK, SG_BLOCK), 0) // CHUNK
    ci = lax.broadcasted_iota(jnp.int32, (SG_BLOCK, SG_BLOCK), 1) // CHUNK
    causal = ri >= ci
    wm = [jnp.where(causal, w_ref[gi], 0.0).astype(BF16) for gi in range(SG_GROUPS)]
    bias = bias_ref[...]
    for n in range(nblk):
        rs = slice(n * SG_BLOCK, (n + 1) * SG_BLOCK)
        v = _gelu(v_ref[rs, :].astype(F32))
        mu = jnp.mean(v, axis=-1, keepdims=True)
        vc = v - mu
        var = jnp.mean(vc * vc, axis=-1, keepdims=True)
        vn = (vc * lax.rsqrt(var + EPS) * g_ref[...] + b_ref[...]).astype(BF16)
        u = _gelu(u_ref[rs, :].astype(F32))
        for gi in range(SG_GROUPS):
            cs = slice(gi * SG_GW, (gi + 1) * SG_GW)
            sv = jnp.dot(wm[gi], vn[:, cs], preferred_element_type=F32) + bias[:, cs]
            o_ref[rs, cs] = (u[:, cs] * sv).astype(o_ref.dtype)


def _sg_mixer(seg_a, ln_g, ln_b, sg_w, bias_tile, l, m):
    tb = min(512, m)
    ub, vb = C_U // SG_WIDTH, C_V // SG_WIDTH
    return pl.pallas_call(
        functools.partial(_sg_kernel, nblk=tb // SG_BLOCK),
        grid=(m // tb,),
        in_specs=[pl.BlockSpec((tb, SG_WIDTH), lambda i: (i, ub)),
                  pl.BlockSpec((tb, SG_WIDTH), lambda i: (i, vb)),
                  pl.BlockSpec((None, 1, SG_WIDTH), lambda i: (l, 0, 0)),
                  pl.BlockSpec((None, 1, SG_WIDTH), lambda i: (l, 0, 0)),
                  pl.BlockSpec((None, SG_GROUPS, SG_BLOCK, SG_BLOCK), lambda i: (l, 0, 0, 0)),
                  pl.BlockSpec((None, SG_BLOCK, SG_WIDTH), lambda i: (l, 0, 0))],
        out_specs=pl.BlockSpec((tb, SG_WIDTH), lambda i: (i, 0)),
        out_shape=jax.ShapeDtypeStruct((m, SG_WIDTH), BF16),
        compiler_params=_cparams(("parallel",)),
        name="sg_mixer",
    )(seg_a, seg_a, ln_g.reshape(-1, 1, SG_WIDTH), ln_b.reshape(-1, 1, SG_WIDTH), sg_w, bias_tile)


def _ssd_kernel(z_ref, xc_ref, xp_ref, bc_ref, bp_ref, dt_ref,
                cwx_ref, cwb_ref, cbx_ref, cbb_ref, dtb_ref, aexp_ref, dexp_ref, ng_ref, e_ref,
                o_ref, st_ref):
    c = pl.program_id(1)

    @pl.when(c == 0)
    def _():
        st_ref[...] = jnp.zeros(st_ref.shape, F32)

    has_prev = c > 0
    srow = lax.broadcasted_iota(jnp.int32, (3 * CHUNK, 2 * CHUNK), 0)
    scol = lax.broadcasted_iota(jnp.int32, (3 * CHUNK, 2 * CHUNK), 1)
    shift = jnp.where(scol == CHUNK + (srow % CHUNK) - (3 - srow // CHUNK), 1.0, 0.0).astype(BF16)

    def conv(cur_ref, prev_ref, w_ref, b_ref):
        cur = cur_ref[...]
        prev = jnp.where(has_prev, prev_ref[...], jnp.zeros_like(cur))
        both = jnp.concatenate([prev, cur], axis=0)
        sh = jnp.dot(shift, both, preferred_element_type=F32)
        w = w_ref[...]
        acc = cur.astype(F32) * w[3:4, :] + b_ref[...]
        for k in range(SSM_CONV - 1):
            acc = acc + sh[k * CHUNK:(k + 1) * CHUNK, :] * w[k:k + 1, :]
        return _silu(acc)

    xs = conv(xc_ref, xp_ref, cwx_ref, cbx_ref)
    bcv = conv(bc_ref, bp_ref, cwb_ref, cbb_ref)
    gn = SSM_GROUPS * SSM_STATE
    bm = bcv[:, :gn].astype(BF16)
    cm = bcv[:, gn:].astype(BF16)

    hi = lax.Precision.HIGHEST
    lane = lax.broadcasted_iota(jnp.int32, (CHUNK, LANE), 1)
    dt = jnp.where(lane < SSM_HEADS, _softplus(dt_ref[...] + dtb_ref[...]), 0.0)
    dt_e = jnp.dot(dt, e_ref[...], precision=hi, preferred_element_type=F32)
    a_e = dt_e * aexp_ref[...]
    r64 = lax.broadcasted_iota(jnp.int32, (CHUNK, CHUNK), 0)
    c64 = lax.broadcasted_iota(jnp.int32, (CHUNK, CHUNK), 1)
    tri = jnp.where(c64 <= r64, 1.0, 0.0).astype(F32)
    a_cs = jnp.dot(tri, a_e, precision=hi, preferred_element_type=F32)
    rl = lax.broadcasted_iota(jnp.int32, (CHUNK, SSM_INNER), 0)
    cl = lax.broadcasted_iota(jnp.int32, (CHUNK, SSM_INNER), 1) % SSM_HEADDIM
    diag = jnp.where(rl == cl, a_cs, 0.0)
    ones = jnp.ones((CHUNK, CHUNK), F32)
    a_row = jnp.dot(ones, diag, precision=hi, preferred_element_type=F32)
    decay = jnp.exp(jnp.where(rl >= cl, a_cs - a_row, NEG_BIG))
    a_last = a_cs[CHUNK - 1:CHUNK, :]
    xd = xs * dt_e
    xe = (xd * jnp.exp(a_last - a_cs)).astype(BF16)
    xdb = xd.astype(BF16)
    ea = jnp.exp(a_cs)
    cdec = jnp.exp(a_last)

    br = lax.broadcasted_iota(jnp.int32, (SSM_GW, SSM_GW), 0) // SSM_HEADDIM
    bc_ = lax.broadcasted_iota(jnp.int32, (SSM_GW, SSM_GW), 1) // SSM_HEADDIM
    blockdiag = br == bc_
    nt = (((1,), (1,)), ((), ()))
    tn = (((0,), (0,)), ((), ()))
    ys = []
    for g in range(SSM_GROUPS):
        ns = slice(g * SSM_STATE, (g + 1) * SSM_STATE)
        ls = slice(g * SSM_GW, (g + 1) * SSM_GW)
        cg, bg = cm[:, ns], bm[:, ns]
        b_t = jnp.concatenate([bg] * SSM_HG, axis=0)
        cb = lax.dot_general(cg, b_t, nt, preferred_element_type=F32)
        mg = (cb * decay[:, ls]).astype(BF16)
        xg = xdb[:, ls]
        bd = jnp.where(blockdiag, jnp.concatenate([xg] * SSM_HG, axis=0), jnp.zeros((), BF16))
        y_diag = jnp.dot(mg, bd, preferred_element_type=F32)
        st = st_ref[g]
        y_off = jnp.dot(cg, st.astype(BF16), preferred_element_type=F32) * ea[:, ls]
        upd = lax.dot_general(bg, xe[:, ls], tn, preferred_element_type=F32)
        st_ref[g] = st * cdec[:, ls] + upd
        ys.append(y_diag + y_off)
    y = jnp.concatenate(ys, axis=1) + xs * dexp_ref[...]
    y = y * _silu(z_ref[...].astype(F32))
    outs = []
    for g in range(SSM_GROUPS):
        yg = y[:, g * SSM_GW:(g + 1) * SSM_GW]
        ms = jnp.mean(yg * yg, axis=-1, keepdims=True)
        outs.append(yg * lax.rsqrt(ms + EPS))
    o_ref[...] = (jnp.concatenate(outs, axis=1) * ng_ref[...]).astype(o_ref.dtype)


def _ssd_mixer(seg_a, seg_s, conv_w, conv_b, dtb_pad, a_exp, d_exp, norm_g, e_mat, l, bsz, seq):
    nc = seq // CHUNK
    zb, xb, bb = C_Z // SSM_INNER, C_XBC // SSM_INNER, C_XBC // SSM_INNER + 1

    def row(b, c):
        return b * nc + c

    def prow(b, c):
        return b * nc + jnp.maximum(c - 1, 0)

    vec = lambda blk: pl.BlockSpec((None, 1, SSM_INNER), lambda b, c: (l, 0, blk))
    return pl.pallas_call(
        _ssd_kernel,
        grid=(bsz, nc),
        in_specs=[pl.BlockSpec((CHUNK, SSM_INNER), lambda b, c: (row(b, c), zb)),
                  pl.BlockSpec((CHUNK, SSM_INNER), lambda b, c: (row(b, c), xb)),
                  pl.BlockSpec((CHUNK, SSM_INNER), lambda b, c: (prow(b, c), xb)),
                  pl.BlockSpec((CHUNK, SSM_INNER), lambda b, c: (row(b, c), bb)),
                  pl.BlockSpec((CHUNK, SSM_INNER), lambda b, c: (prow(b, c), bb)),
                  pl.BlockSpec((CHUNK, LANE), lambda b, c: (row(b, c), 1)),
                  pl.BlockSpec((None, SSM_CONV, SSM_INNER), lambda b, c: (l, 0, 0)),
                  pl.BlockSpec((None, SSM_CONV, SSM_INNER), lambda b, c: (l, 0, 1)),
                  vec(0), vec(1),
                  pl.BlockSpec((None, 1, LANE), lambda b, c: (l, 0, 0)),
                  vec(0), vec(0), vec(0),
                  pl.BlockSpec((LANE, SSM_INNER), lambda b, c: (0, 0))],
        out_specs=pl.BlockSpec((CHUNK, SSM_INNER), lambda b, c: (row(b, c), 0)),
        out_shape=jax.ShapeDtypeStruct((bsz * seq, SSM_INNER), BF16),
        scratch_shapes=[pltpu.VMEM((SSM_GROUPS, SSM_STATE, SSM_GW), F32)],
        compiler_params=_cparams(("parallel", "arbitrary")),
        name="ssd_mixer",
    )(seg_a, seg_a, seg_a, seg_a, seg_a, seg_s, conv_w, conv_w,
      conv_b.reshape(-1, 1, 2 * SSM_INNER), conv_b.reshape(-1, 1, 2 * SSM_INNER),
      dtb_pad, a_exp, d_exp, norm_g.reshape(-1, 1, SSM_INNER), e_mat)


def _rope128(x, cos, sin_signed):
    return x * cos + pltpu.roll(x, ATT_HEADDIM // 2, 1) * sin_signed


def _rope64(x, cos, sin_signed):
    lane = lax.broadcasted_iota(jnp.int32, x.shape, 1)
    low = (lane % IDX_HEADDIM) < IDX_HEADDIM // 2
    rot = jnp.where(low, pltpu.roll(x, LANE - IDX_HEADDIM // 2, 1), pltpu.roll(x, IDX_HEADDIM // 2, 1))
    return x * cos + rot * sin_signed


def _dsa_kernel(q_ref, k_ref, v_ref, qi_ref, ki_ref, wi_ref,
                cq_ref, sq_ref, ck_ref, sk_ref, ciq_ref, siq_ref, cik_ref, sik_ref, *rest,
                klen, q0, topk):
    o_ref, kr_ref, vb_ref, kir_ref, key_ref, bias_ref, pos_ref = rest[-7:]
    i = pl.program_id(1)
    nt = (((1,), (1,)), ((), ()))
    nlc = klen // LANE

    @pl.when(i == 0)
    def _():
        for kv in range(ATT_KV_HEADS):
            hs = slice(kv * ATT_HEADDIM, (kv + 1) * ATT_HEADDIM)
            kr_ref[:, hs] = _rope128(k_ref[:, hs], ck_ref[...], sk_ref[...]).astype(BF16)
        vb_ref[...] = v_ref[...].astype(BF16)
        kx = _rope64(ki_ref[...], cik_ref[...], sik_ref[...])
        kx_hi = kx.astype(BF16)
        kir_ref[:, 0:LANE] = kx_hi
        kir_ref[:, LANE:2 * LANE] = (kx - kx_hi.astype(F32)).astype(BF16)

    lane_q = lax.broadcasted_iota(jnp.int32, (Q_BLOCK, LANE), 1)
    kir = kir_ref[...]
    wi = wi_ref[...] * np.float32(IDX_HEADS ** -0.5)
    iscore = jnp.zeros((Q_BLOCK, klen), F32)
    for quad in range(IDX_HEADS // 4):
        parts = []
        for pair in range(2 * quad, 2 * quad + 2):
            ps = slice(pair * LANE, (pair + 1) * LANE)
            qp = _rope64(qi_ref[:, ps], ciq_ref[...], siq_ref[...])
            q_hi = qp.astype(BF16).astype(F32)
            q_lo_swapped = pltpu.roll(qp - q_hi, IDX_HEADDIM, 1)
            for sub in range(2):
                own = (lane_q // IDX_HEADDIM) == sub
                parts.append(jnp.concatenate([jnp.where(own, q_hi, q_lo_swapped), jnp.where(own, q_hi, 0.0)],
                                             axis=1).astype(BF16))
        logits = lax.dot_general(jnp.concatenate(parts, axis=0), kir, nt,
                                 preferred_element_type=F32)
        for hh in range(4):
            h = 4 * quad + hh
            wcol = wi[:, S_WI_LANE + h:S_WI_LANE + h + 1]
            iscore = iscore + jnp.maximum(logits[hh * Q_BLOCK:(hh + 1) * Q_BLOCK, :], 0.0) * wcol

    qchunk = (lax.broadcasted_iota(jnp.int32, (Q_BLOCK, klen), 0) + (q0 + i) * Q_BLOCK) // CHUNK
    kchunk = lax.broadcasted_iota(jnp.int32, (Q_BLOCK, klen), 1) // CHUNK
    iscore = jnp.where(iscore == 0.0, 0.0, iscore)
    bits = pltpu.bitcast(iscore, jnp.int32)
    key = jnp.where(bits < 0, bits ^ jnp.int32(0x7FFFFFFF), bits)
    key = jnp.maximum(key, jnp.int32(INT_MIN + 1))
    key_ref[...] = jnp.where(kchunk <= qchunk, key, jnp.int32(INT_MIN))

    def row_count(pred, rows=slice(0, Q_BLOCK)):
        nrow = rows.stop - rows.start
        acc = jnp.zeros((nrow, LANE), F32)
        for cidx in range(nlc):
            acc = acc + jnp.where(pred(key_ref[rows, cidx * LANE:(cidx + 1) * LANE], cidx), 1.0, 0.0)
        return jnp.broadcast_to(jnp.sum(acc, axis=-1, keepdims=True), (nrow, LANE))

    halves = (slice(0, Q_BLOCK // 2), slice(Q_BLOCK // 2, Q_BLOCK))

    def thr_step(it, t_us):
        bit = jnp.left_shift(jnp.int32(1), 31 - it)
        out = []
        for rows, t_u in zip(halves, t_us):
            cand_u = t_u | bit
            cand = cand_u ^ jnp.int32(INT_MIN)
            out.append(jnp.where(row_count(lambda kc, _: kc >= cand, rows) >= topk, cand_u, t_u))
        return tuple(out)

    t_us = lax.fori_loop(0, 32, thr_step, tuple(jnp.zeros((Q_BLOCK // 2, LANE), jnp.int32) for _ in halves),
                         unroll=8)
    thr = jnp.concatenate(t_us, axis=0) ^ jnp.int32(INT_MIN)
    cnt_ge = row_count(lambda kc, _: kc >= thr)
    cnt_gt = row_count(lambda kc, _: kc > thr)
    need = topk - cnt_gt
    excess = jnp.where(thr > INT_MIN, cnt_ge - topk, 0.0)
    pos_ref[...] = jnp.full((Q_BLOCK, LANE), klen, jnp.int32)
    lane_pos = lax.broadcasted_iota(jnp.int32, (Q_BLOCK, LANE), 1)

    @pl.when(jnp.max(excess) > 0.0)
    def _():
        nbits = int(klen - 1).bit_length()
        never = jnp.int32(1 << 30)

        def pos_step(it, bound):
            cand = bound | jnp.left_shift(jnp.int32(1), nbits - 1 - it)
            ties = row_count(lambda kc, cidx: jnp.where(kc == thr, lane_pos + cidx * LANE, never) < cand)
            return jnp.where(ties < need, cand, bound)

        bound = lax.fori_loop(0, nbits, pos_step, jnp.zeros((Q_BLOCK, LANE), jnp.int32))
        pos_ref[...] = bound + 1

    pos = pos_ref[...]
    for cidx in range(nlc):
        cs = slice(cidx * LANE, (cidx + 1) * LANE)
        kc = key_ref[:, cs]
        tie = jnp.where(lane_pos + cidx * LANE < pos, 0.0, NEG_BIG)
        sel = jnp.where(kc > thr, 0.0, jnp.where(kc == thr, tie, NEG_BIG))
        bias_ref[:, cs] = jnp.where(kc == INT_MIN, NEG_BIG, sel)

    scale = np.float32(ATT_HEADDIM ** -0.5)
    for kv in range(ATT_KV_HEADS):
        hs = slice(kv * ATT_HEADDIM, (kv + 1) * ATT_HEADDIM)
        krh = kr_ref[:, hs]
        vh = vb_ref[:, hs]
        heads = [slice((kv * ATT_GRP + gq) * ATT_HEADDIM, (kv * ATT_GRP + gq + 1) * ATT_HEADDIM)
                 for gq in range(ATT_GRP)]
        qg = jnp.concatenate([(_rope128(q_ref[:, qs], cq_ref[...], sq_ref[...]) * scale).astype(BF16)
                              for qs in heads], axis=0)
        s = lax.dot_general(qg, krh, nt, preferred_element_type=F32)
        es, dens = [], []
        for gq in range(ATT_GRP):
            sg = s[gq * Q_BLOCK:(gq + 1) * Q_BLOCK, :] + bias_ref[...]
            e = jnp.exp(sg - jnp.max(sg, axis=-1, keepdims=True))
            dens.append(jnp.sum(e, axis=-1, keepdims=True))
            es.append(e.astype(BF16))
        o = jnp.dot(jnp.concatenate(es, axis=0), vh, preferred_element_type=F32)
        for gq, qs in enumerate(heads):
            o_ref[:, qs] = (o[gq * Q_BLOCK:(gq + 1) * Q_BLOCK, :] / dens[gq]).astype(o_ref.dtype)


DSA_BUCKETS = 4


def _dsa_mixer(seg_q, seg_s, tabs, bsz, seq):
    nqb = seq // Q_BLOCK
    topk = min(IDX_TOPK, seq // 4)
    cos128, sin128, cos64, sin64 = tabs
    kb, vb_, qib = 1024 // 256, 1280 // 256, 1536 // 512
    nbk = DSA_BUCKETS if nqb % DSA_BUCKETS == 0 else 1
    qpb = nqb // nbk
    width = ATT_HEADS * ATT_HEADDIM
    kvw = ATT_KV_HEADS * ATT_HEADDIM
    out = None
    for u in range(nbk):
        q0 = u * qpb
        klen = (u + 1) * qpb * Q_BLOCK
        qtab = pl.BlockSpec((Q_BLOCK, LANE), lambda b, i, q0=q0: (q0 + i, 0))
        ktab = pl.BlockSpec((klen, LANE), lambda b, i: (0, 0))
        qrow = lambda blk, q0=q0: (lambda b, i: (b, q0 + i, blk))
        in_specs = [pl.BlockSpec((None, Q_BLOCK, width), qrow(0)),
                    pl.BlockSpec((None, klen, kvw), lambda b, i: (b, 0, kb)),
                    pl.BlockSpec((None, klen, kvw), lambda b, i: (b, 0, vb_)),
                    pl.BlockSpec((None, Q_BLOCK, IDX_HEADS * IDX_HEADDIM), qrow(qib)),
                    pl.BlockSpec((None, klen, LANE), lambda b, i: (b, 0, 0)),
                    pl.BlockSpec((None, Q_BLOCK, LANE), qrow(1)),
                    qtab, qtab, ktab, ktab, qtab, qtab, ktab, ktab]
        args = [seg_q, seg_q, seg_q, seg_q, seg_s, seg_s,
                cos128, sin128, cos128, sin128, cos64, sin64, cos64, sin64]
        aliases = {}
        if out is not None:
            in_specs.append(pl.BlockSpec(memory_space=pl.ANY))
            args.append(out)
            aliases = {len(args) - 1: 0}
        out = pl.pallas_call(
            functools.partial(_dsa_kernel, klen=klen, q0=q0, topk=topk),
            grid=(bsz, qpb),
            in_specs=in_specs,
            out_specs=pl.BlockSpec((None, Q_BLOCK, width), qrow(0)),
            out_shape=jax.ShapeDtypeStruct((bsz, seq, width), BF16),
            scratch_shapes=[pltpu.VMEM((klen, kvw), BF16),
                            pltpu.VMEM((klen, kvw), BF16),
                            pltpu.VMEM((klen, 2 * LANE), BF16),
                            pltpu.VMEM((Q_BLOCK, klen), jnp.int32),
                            pltpu.VMEM((Q_BLOCK, klen), F32),
                            pltpu.VMEM((Q_BLOCK, LANE), jnp.int32)],
            input_output_aliases=aliases,
            compiler_params=_cparams(("parallel", "arbitrary")),
            name="dsa_mixer",
        )(*args)
    return out


def _rope_tables(seq):
    pos = jnp.arange(seq, dtype=F32)[:, None]

    def tab(half, reps):
        inv = ROPE_THETA ** (-jnp.arange(half, dtype=F32) / half)
        ang = pos * inv[None, :]
        cos, sin = jnp.cos(ang), jnp.sin(ang)
        return (jnp.tile(jnp.concatenate([cos, cos], axis=1), (1, reps)),
                jnp.tile(jnp.concatenate([-sin, sin], axis=1), (1, reps)))

    cos128, sin128 = tab(ATT_HEADDIM // 2, 1)
    cos64, sin64 = tab(IDX_HEADDIM // 2, 2)
    return cos128, sin128, cos64, sin64


def _merge_kernel(h_ref, wg_ref, ya_ref, yb_ref, yc_ref, yd_ref, p_ref, o_ref):
    h = h_ref[...]
    acc = None
    row = 0
    for i, y_ref in enumerate((ya_ref, yb_ref, yc_ref, yd_ref)):
        width = y_ref.shape[1]
        gate = _sigmoid(jnp.dot(h, wg_ref[i], preferred_element_type=F32))
        term = gate * jnp.dot(y_ref[...], p_ref[row:row + width, :], preferred_element_type=F32)
        acc = term if acc is None else acc + term
        row += width
    o_ref[...] = acc.astype(o_ref.dtype)


def _gated_merge(h, ys, w_gate, w_branch, l, *, tm=1024, tn=256):
    m, d = h.shape
    tm = min(tm, m)
    nbr = len(ys)
    resident = lambda width: pl.BlockSpec((tm, width), lambda i, j: (i, 0), pipeline_mode=pl.Buffered(1))
    return pl.pallas_call(
        _merge_kernel,
        grid=(m // tm, d // tn),
        in_specs=[resident(d),
                  pl.BlockSpec((None, nbr, d, tn), lambda i, j: (l, 0, 0, j))]
                 + [resident(y.shape[1]) for y in ys]
                 + [pl.BlockSpec((None, w_branch.shape[1], tn), lambda i, j: (l, 0, j))],
        out_specs=pl.BlockSpec((tm, tn), lambda i, j: (i, j)),
        out_shape=jax.ShapeDtypeStruct((m, d), BF16),
        compiler_params=_cparams(("parallel", "arbitrary")),
        name="gated_merge",
    )(h, w_gate, *ys, w_branch)


def _xattn_kernel(q_ref, kv_ref, o_ref):
    nt = (((1,), (1,)), ((), ()))
    scale = np.float32(MEM_HEADDIM ** -0.5)
    hw = MEM_HEADS * MEM_HEADDIM
    for h in range(MEM_HEADS):
        hs = slice(h * MEM_HEADDIM, (h + 1) * MEM_HEADDIM)
        s = lax.dot_general(q_ref[:, hs], kv_ref[:, hs], nt, preferred_element_type=F32) * scale
        mx = jnp.max(s, axis=-1, keepdims=True)
        e = jnp.exp(s - mx)
        den = jnp.sum(e, axis=-1, keepdims=True)
        vs = slice(hw + h * MEM_HEADDIM, hw + (h + 1) * MEM_HEADDIM)
        o = jnp.dot(e.astype(BF16), kv_ref[:, vs], preferred_element_type=F32)
        o_ref[:, hs] = (o / den).astype(o_ref.dtype)


def _xattn(q, kv, bsz, seq, mem_len):
    tq = min(512, seq)
    nq = seq // tq
    hw = MEM_HEADS * MEM_HEADDIM
    return pl.pallas_call(
        _xattn_kernel,
        grid=(bsz, nq),
        in_specs=[pl.BlockSpec((tq, hw), lambda b, i: (b * nq + i, 0)),
                  pl.BlockSpec((mem_len, 2 * hw), lambda b, i: (b, 0))],
        out_specs=pl.BlockSpec((tq, hw), lambda b, i: (b * nq + i, 0)),
        out_shape=jax.ShapeDtypeStruct((bsz * seq, hw), BF16),
        compiler_params=_cparams(("parallel", "arbitrary")),
        name="mem_xattn",
    )(q, kv)


def kernel(x, mem, g_ffn1, w_ffn1_in, w_ffn1_out, g_mix, w_in, pool_w, pool_scale, sg_ln_g, sg_ln_b, sg_w, sg_b, ssm_conv_w, ssm_conv_b, ssm_a_log, ssm_dt_bias, ssm_d, ssm_norm_g, w_branch, w_gate, w_out, g_mem, g_cross, w_mem_q, w_mem_kv, w_mem_o, g_ffn2, w_ffn2_in, w_ffn2_out, g_final):
    bsz, seq, d = x.shape
    mem_len = mem.shape[1]
    depth = w_in.shape[0]
    m = bsz * seq
    bf = lambda a: a.astype(BF16)

    w1i, w1o, w2i, w2o = bf(w_ffn1_in), bf(w_ffn1_out), w_ffn2_in, bf(w_ffn2_out)
    w_in_t = jnp.swapaxes(w_in, 1, 2)
    w_q = w_in_t[:, C_Q:C_KI]
    w_ki = w_in_t[:, C_KI:C_WI]
    w_s = jnp.concatenate(
        [w_ki, w_ki, w_in_t[:, C_DT:C_Q], w_in_t[:, C_WI:],
         jnp.zeros((depth, LANE - SSM_HEADS - IDX_HEADS, d), F32)], axis=1)
    wg, wb, wo = bf(w_gate), bf(w_branch), bf(w_out)
    wmkv, wmo = w_mem_kv, w_mem_o
    pw = bf(pool_w)

    expand = lambda v: jnp.repeat(v, SSM_HEADDIM, axis=-1).reshape(depth, 1, SSM_INNER)
    a_exp = expand(-jnp.exp(ssm_a_log))
    d_exp = expand(ssm_d)
    dtb_pad = jnp.pad(ssm_dt_bias, ((0, 0), (0, LANE - SSM_HEADS))).reshape(depth, 1, LANE)
    e_mat = (jnp.arange(LANE)[:, None] == (jnp.arange(SSM_INNER)[None, :] // SSM_HEADDIM)).astype(F32)
    bias_tile = jnp.repeat(jnp.swapaxes(sg_b, 1, 2), SG_GW, axis=2)
    tabs = _rope_tables(seq)

    x2 = x.reshape(m, d)
    mem_n = _rmsnorm(mem.reshape(bsz * mem_len, d), g_mem, BF16)

    for l in range(depth):
        x2 = _ffn(x2, g_ffn1[l], w1i, w1o, l, tm_in=2048)

        h = _rmsnorm(x2, g_mix[l], BF16)
        seg_a = _matmul_nt(h, w_in_t, l, SEG_A, out_dtype=BF16)
        seg_q = _matmul_nt(h, w_q, l, SEG_Q, out_dtype=F32)
        seg_s = _matmul_nt(h, w_s, l, SEG_S, tn=256, out_dtype=F32)
        y_a = _pool_mixer(seg_a, pw, pool_scale, l, bsz, seq)
        y_b = _sg_mixer(seg_a, sg_ln_g, sg_ln_b, sg_w, bias_tile, l, m)
        y_c = _ssd_mixer(seg_a, seg_s, ssm_conv_w, ssm_conv_b, dtb_pad, a_exp, d_exp,
                         ssm_norm_g, e_mat, l, bsz, seq)
        y_d = _dsa_mixer(seg_q.reshape(bsz, seq, SEG_Q), seg_s.reshape(bsz, seq, SEG_S), tabs, bsz, seq)
        merged = _gated_merge(h, (y_a, y_b, y_c, y_d.reshape(m, -1)), wg, wb, l)
        x2 = _matmul_res(merged, wo, (l,), x2, 1.0)

        q = _norm_matmul(x2, g_cross[l], w_mem_q, (l,))
        kv = _matmul(mem_n, wmkv, (l,), 2 * MEM_HEADS * MEM_HEADDIM, out_dtype=BF16)
        att = _xattn(q, kv, bsz, seq, mem_len)
        x2, h2 = _matmul_res_norm(att, wmo, (l,), x2, g_ffn2[l])

        x2 = _ffn(x2, g_ffn2[l], w2i, w2o, l, h=h2, tn_out=512)

    return _rmsnorm(x2, g_final, F32).reshape(bsz, seq, d)
```

```python
import functools

import jax
import jax.numpy as jnp
import numpy as np
from jax import lax
from jax.experimental import pallas as pl
from jax.experimental.pallas import tpu as pltpu

F32 = jnp.float32
BF16 = jnp.bfloat16

D_MODEL = 4096
FFN_DIM = 8192
CHUNK = 64
EPS = 1e-6
ROPE_THETA = 10000.0

POOL_WINDOWS = (2, 4, 8, 16)
POOL_GROUPS = 4
POOL_WIDTH = 2048
POOL_GW = POOL_WIDTH // POOL_GROUPS
POOL_PAD = 16

SG_WIDTH = 1024
SG_BLOCK = 128
SG_GROUPS = 4
SG_GW = SG_WIDTH // SG_GROUPS

SSM_HEADS = 16
SSM_HEADDIM = 64
SSM_INNER = SSM_HEADS * SSM_HEADDIM
SSM_GROUPS = 4
SSM_STATE = 128
SSM_CONV = 4
SSM_HG = SSM_HEADS // SSM_GROUPS
SSM_GW = SSM_INNER // SSM_GROUPS

ATT_HEADS = 8
ATT_KV_HEADS = 2
ATT_HEADDIM = 128
ATT_GRP = ATT_HEADS // ATT_KV_HEADS
IDX_HEADS = 8
IDX_HEADDIM = 64
IDX_TOPK = 256
Q_BLOCK = 128

MEM_HEADS = 4
MEM_HEADDIM = 128

C_POOL, C_U, C_V, C_Z, C_XBC = 0, 2048, 3072, 4096, 5120
C_DT, C_Q, C_K, C_VAL, C_QI, C_KI, C_WI = 7168, 7184, 8208, 8464, 8720, 9232, 9296
SEG_A = 7168
SEG_Q = 2048
SEG_S = 256
S_DT_LANE = 0
S_WI_LANE = 16

LANE = 128
VMEM_LIMIT = 56 * 1024 * 1024
NEG_BIG = -1e30
INT_MIN = -2147483648


def _cparams(sem):
    return pltpu.CompilerParams(dimension_semantics=sem, vmem_limit_bytes=VMEM_LIMIT)


def _sigmoid(x):
    return 1.0 / (1.0 + jnp.exp(-x))


def _silu(x):
    return x * _sigmoid(x)


def _gelu(x):
    return 0.5 * x * (1.0 + lax.erf(x * np.float32(1.0 / np.sqrt(2.0))))


def _softplus(x):
    return jnp.maximum(x, 0.0) + jnp.log1p(jnp.exp(-jnp.abs(x)))


def _rmsnorm_kernel(x_ref, g_ref, o_ref):
    x = x_ref[...]
    ms = jnp.mean(x * x, axis=-1, keepdims=True)
    o_ref[...] = (x * lax.rsqrt(ms + EPS) * g_ref[...]).astype(o_ref.dtype)


def _rmsnorm(x, g, out_dtype):
    m, d = x.shape
    tm = min(512, m)
    return pl.pallas_call(
        _rmsnorm_kernel,
        grid=(m // tm,),
        in_specs=[pl.BlockSpec((tm, d), lambda i: (i, 0)),
                  pl.BlockSpec((1, d), lambda i: (0, 0))],
        out_specs=pl.BlockSpec((tm, d), lambda i: (i, 0)),
        out_shape=jax.ShapeDtypeStruct((m, d), out_dtype),
        compiler_params=_cparams(("parallel",)),
        name="rmsnorm",
    )(x, g.reshape(1, d))


def _mm_kernel(x_ref, w_ref, o_ref):
    o_ref[...] = jnp.dot(x_ref[...], w_ref[...].astype(BF16), preferred_element_type=F32).astype(o_ref.dtype)


def _mm_res_kernel(x_ref, w_ref, r_ref, o_ref, *, alpha):
    acc = jnp.dot(x_ref[...], w_ref[...].astype(BF16), preferred_element_type=F32)
    o_ref[...] = r_ref[...] + alpha * acc


def _mm_res_acc_kernel(x_ref, w_ref, r_ref, o_ref, acc_ref, *, alpha, nk):
    k = pl.program_id(2)
    part = jnp.dot(x_ref[...], w_ref[...].astype(BF16), preferred_element_type=F32)

    @pl.when(k == 0)
    def _():
        acc_ref[...] = part

    @pl.when(k > 0)
    def _():
        acc_ref[...] += part

    @pl.when(k == nk - 1)
    def _():
        o_ref[...] = r_ref[...] + alpha * acc_ref[...]


def _w_spec(w, widx, kblk, tn, col_blk0, kgrid):
    lead = (None,) * len(widx)
    if kgrid:
        return pl.BlockSpec(lead + (kblk, tn), lambda i, j, k: (*widx, k, j + col_blk0))
    return pl.BlockSpec(lead + (kblk, tn), lambda i, j: (*widx, 0, j + col_blk0))


def _matmul(x, w, widx, n, *, col0=0, tm=1024, tn=512, out_dtype=BF16):
    m, kdim = x.shape
    tm, tn = min(tm, m), min(tn, n)
    return pl.pallas_call(
        _mm_kernel,
        grid=(m // tm, n // tn),
        in_specs=[pl.BlockSpec((tm, kdim), lambda i, j: (i, 0)),
                  _w_spec(w, widx, kdim, tn, col0 // tn, False)],
        out_specs=pl.BlockSpec((tm, tn), lambda i, j: (i, j)),
        out_shape=jax.ShapeDtypeStruct((m, n), out_dtype),
        compiler_params=_cparams(("parallel", "arbitrary")),
        name="matmul",
    )(x, w)


def _mm_nt_kernel(x_ref, wt_ref, o_ref):
    nt = (((1,), (1,)), ((), ()))
    o_ref[...] = lax.dot_general(x_ref[...], wt_ref[...].astype(BF16), nt,
                                 preferred_element_type=F32).astype(o_ref.dtype)


def _matmul_nt(x, wt, l, n, *, tm=1024, tn=512, out_dtype=BF16):
    m, kdim = x.shape
    tm, tn = min(tm, m), min(tn, n)
    return pl.pallas_call(
        _mm_nt_kernel,
        grid=(m // tm, n // tn),
        in_specs=[pl.BlockSpec((tm, kdim), lambda i, j: (i, 0)),
                  pl.BlockSpec((None, tn, kdim), lambda i, j: (l, j, 0))],
        out_specs=pl.BlockSpec((tm, tn), lambda i, j: (i, j)),
        out_shape=jax.ShapeDtypeStruct((m, n), out_dtype),
        compiler_params=_cparams(("parallel", "arbitrary")),
        name="matmul_nt",
    )(x, wt)


def _x_spec(tm, kdim, resident=False):
    if resident:
        return pl.BlockSpec((tm, kdim), lambda i, j: (i, 0), pipeline_mode=pl.Buffered(1))
    return pl.BlockSpec((tm, kdim), lambda i, j: (i, 0))


def _matmul_res(x, w, widx, res, alpha, *, tm=1024, tn=512, tk=None, resident_x=False):
    m, kdim = x.shape
    n = res.shape[1]
    tm, tn = min(tm, m), min(tn, n)
    if tk is None or tk >= kdim:
        return pl.pallas_call(
            functools.partial(_mm_res_kernel, alpha=alpha),
            grid=(m // tm, n // tn),
            in_specs=[_x_spec(tm, kdim, resident_x),
                      _w_spec(w, widx, kdim, tn, 0, False),
                      pl.BlockSpec((tm, tn), lambda i, j: (i, j))],
            out_specs=pl.BlockSpec((tm, tn), lambda i, j: (i, j)),
            out_shape=jax.ShapeDtypeStruct((m, n), F32),
            compiler_params=_cparams(("parallel", "arbitrary")),
            name="matmul_res",
        )(x, w, res)
    nk = kdim // tk
    return pl.pallas_call(
        functools.partial(_mm_res_acc_kernel, alpha=alpha, nk=nk),
        grid=(m // tm, n // tn, nk),
        in_specs=[pl.BlockSpec((tm, tk), lambda i, j, k: (i, k)),
                  _w_spec(w, widx, tk, tn, 0, True),
                  pl.BlockSpec((tm, tn), lambda i, j, k: (i, j))],
        out_specs=pl.BlockSpec((tm, tn), lambda i, j, k: (i, j)),
        out_shape=jax.ShapeDtypeStruct((m, n), F32),
        scratch_shapes=[pltpu.VMEM((tm, tn), F32)],
        compiler_params=_cparams(("parallel", "arbitrary", "arbitrary")),
        name="matmul_res_acc",
    )(x, w, res)


def _mm_res_norm_kernel(x_ref, w_ref, r_ref, g_ref, o_ref, h_ref):
    y = r_ref[...] + jnp.dot(x_ref[...], w_ref[...].astype(BF16), preferred_element_type=F32)
    o_ref[...] = y
    ms = jnp.mean(y * y, axis=-1, keepdims=True)
    h_ref[...] = (y * lax.rsqrt(ms + EPS) * g_ref[...]).astype(h_ref.dtype)


def _matmul_res_norm(x, w, widx, res, g, *, tm=256):
    m, kdim = x.shape
    n = res.shape[1]
    tm = min(tm, m)
    lead = (None,) * len(widx)
    row = pl.BlockSpec((tm, n), lambda i: (i, 0))
    return pl.pallas_call(
        _mm_res_norm_kernel,
        grid=(m // tm,),
        in_specs=[pl.BlockSpec((tm, kdim), lambda i: (i, 0)),
                  pl.BlockSpec(lead + (kdim, n), lambda i: (*widx, 0, 0)),
                  row,
                  pl.BlockSpec((1, n), lambda i: (0, 0))],
        out_specs=[row, row],
        out_shape=[jax.ShapeDtypeStruct((m, n), F32), jax.ShapeDtypeStruct((m, n), BF16)],
        compiler_params=_cparams(("parallel",)),
        name="matmul_res_norm",
    )(x, w, res, g.reshape(1, n))


def _norm_mm_kernel(x_ref, g_ref, w_ref, o_ref):
    x = x_ref[...]
    ms = jnp.mean(x * x, axis=-1, keepdims=True)
    xn = (x * lax.rsqrt(ms + EPS) * g_ref[...]).astype(BF16)
    o_ref[...] = jnp.dot(xn, w_ref[...].astype(BF16), preferred_element_type=F32).astype(o_ref.dtype)


def _norm_matmul(x, g, w, widx, *, tm=512, out_dtype=BF16):
    m, d = x.shape
    n = w.shape[-1]
    tm = min(tm, m)
    lead = (None,) * len(widx)
    return pl.pallas_call(
        _norm_mm_kernel,
        grid=(m // tm,),
        in_specs=[pl.BlockSpec((tm, d), lambda i: (i, 0)),
                  pl.BlockSpec((1, d), lambda i: (0, 0)),
                  pl.BlockSpec(lead + (d, n), lambda i: (*widx, 0, 0))],
        out_specs=pl.BlockSpec((tm, n), lambda i: (i, 0)),
        out_shape=jax.ShapeDtypeStruct((m, n), out_dtype),
        compiler_params=_cparams(("parallel",)),
        name="norm_matmul",
    )(x, g.reshape(1, d), w)


def _swiglu_kernel(x_ref, wg_ref, wu_ref, o_ref, w_ref):
    tn = wg_ref.shape[1]
    w_ref[:, 0:tn] = wg_ref[...].astype(BF16)
    w_ref[:, tn:2 * tn] = wu_ref[...].astype(BF16)
    gu = jnp.dot(x_ref[...], w_ref[...], preferred_element_type=F32)
    o_ref[...] = (_silu(gu[:, 0:tn]) * gu[:, tn:2 * tn]).astype(o_ref.dtype)


def _swiglu_in(x, w, l, *, tm=1024):
    m, kdim = x.shape
    f = w.shape[-1] // 2
    resident = tm > 1024 and m >= tm
    tm = min(tm, m)
    tn = 256 if (w.dtype == F32 or resident) else 512
    nb = f // tn
    return pl.pallas_call(
        _swiglu_kernel,
        grid=(m // tm, nb),
        in_specs=[_x_spec(tm, kdim, resident),
                  pl.BlockSpec((None, kdim, tn), lambda i, j: (l, 0, j)),
                  pl.BlockSpec((None, kdim, tn), lambda i, j: (l, 0, j + nb))],
        out_specs=pl.BlockSpec((tm, tn), lambda i, j: (i, j)),
        out_shape=jax.ShapeDtypeStruct((m, f), BF16),
        scratch_shapes=[pltpu.VMEM((kdim, 2 * tn), BF16)],
        compiler_params=_cparams(("parallel", "arbitrary")),
        name="swiglu_in",
    )(x, w, w)


def _ffn(x, g, w_in, w_out, l, *, h=None, tm_in=1024, tn_out=256):
    if h is None:
        h = _rmsnorm(x, g, BF16)
    act = _swiglu_in(h, w_in, l, tm=tm_in)
    return _matmul_res(act, w_out, (l,), x, 0.5, tm=1024, tn=tn_out, resident_x=tn_out > 256)


def _pool_kernel(a_ref, w_ref, s_ref, o_ref, pad_ref, *, seq, rows):
    g = pl.program_id(1)
    pad_ref[0:POOL_PAD, :] = jnp.zeros((POOL_PAD, POOL_GW), F32)
    pad_ref[POOL_PAD:POOL_PAD + seq, :] = a_ref[...].astype(F32)
    w = w_ref[...]
    scale = s_ref[...]
    for gi, win in enumerate(POOL_WINDOWS):

        @pl.when(g == gi)
        def _(win=win):
            for c in range(seq // rows):
                r0 = POOL_PAD + c * rows
                cur = pad_ref[r0:r0 + rows, :]
                tot = cur
                for k in range(1, win):
                    tot = tot + pad_ref[r0 - k:r0 - k + rows, :]
                t1 = lax.broadcasted_iota(jnp.int32, (rows, POOL_GW), 0) + (c * rows + 1)
                cnt = jnp.minimum(t1, win).astype(F32)
                mixed = (tot / cnt - cur).astype(BF16)
                y = jnp.dot(mixed, w, preferred_element_type=F32) * scale
                o_ref[c * rows:(c + 1) * rows, :] = y.astype(o_ref.dtype)


def _pool_mixer(seg_a, pool_w, pool_scale, l, bsz, seq):
    rows = min(256, seq)
    return pl.pallas_call(
        functools.partial(_pool_kernel, seq=seq, rows=rows),
        grid=(bsz, POOL_GROUPS),
        in_specs=[pl.BlockSpec((seq, POOL_GW), lambda b, g: (b, g)),
                  pl.BlockSpec((None, None, POOL_GW, POOL_GW), lambda b, g: (l, g, 0, 0)),
                  pl.BlockSpec((None, None, 1, POOL_GW), lambda b, g: (l, g, 0, 0))],
        out_specs=pl.BlockSpec((seq, POOL_GW), lambda b, g: (b, g)),
        out_shape=jax.ShapeDtypeStruct((bsz * seq, POOL_WIDTH), BF16),
        scratch_shapes=[pltpu.VMEM((POOL_PAD + seq, POOL_GW), F32)],
        compiler_params=_cparams(("parallel", "arbitrary")),
        name="pool_mixer",
    )(seg_a, pool_w, pool_scale.reshape(pool_scale.shape[0], POOL_GROUPS, 1, POOL_GW))


def _sg_kernel(u_ref, v_ref, g_ref, b_ref, w_ref, bias_ref, o_ref, *, nblk):
    ri = lax.broadcasted_iota(jnp.int32, (SG_BLOCK, SG_BLOCK), 0) // CHUNK
    ci = lax.broadcasted_iota(jnp.int32, (SG_BLOCK, SG_BLOCK), 1) // CHUNK
    causal = ri >= ci
    wm = [jnp.where(causal, w_ref[gi], 0.0).astype(BF16) for gi in range(SG_GROUPS)]
    bias = bias_ref[...]
    for n in range(nblk):
        rs = slice(n * SG_BLOCK, (n + 1) * SG_BLOCK)
        v = _gelu(v_ref[rs, :].astype(F32))
        mu = jnp.mean(v, axis=-1, keepdims=True)
        vc = v - mu
        var = jnp.mean(vc * vc, axis=-1, keepdims=True)
        vn = (vc * lax.rsqrt(var + EPS) * g_ref[...] + b_ref[...]).astype(BF16)
        u = _gelu(u_ref[rs, :].astype(F32))
        for gi in range(SG_GROUPS):
            cs = slice(gi * SG_GW, (gi + 1) * SG_GW)
            sv = jnp.dot(wm[gi], vn[:, cs], preferred_element_type=F32) + bias[:, cs]
            o_ref[rs, cs] = (u[:, cs] * sv).astype(o_ref.dtype)


def _sg_mixer(seg_a, ln_g, ln_b, sg_w, bias_tile, l, m):
    tb = min(512, m)
    ub, vb = C_U // SG_WIDTH, C_V // SG_WIDTH
    return pl.pallas_call(
        functools.partial(_sg_kernel, nblk=tb // SG_BLOCK),
        grid=(m // tb,),
        in_specs=[pl.BlockSpec((tb, SG_WIDTH), lambda i: (i, ub)),
                  pl.BlockSpec((tb, SG_WIDTH), lambda i: (i, vb)),
                  pl.BlockSpec((None, 1, SG_WIDTH), lambda i: (l, 0, 0)),
                  pl.BlockSpec((None, 1, SG_WIDTH), lambda i: (l, 0, 0)),
                  pl.BlockSpec((None, SG_GROUPS, SG_BLOCK, SG_BLOCK), lambda i: (l, 0, 0, 0)),
                  pl.BlockSpec((None, SG_BLOCK, SG_WIDTH), lambda i: (l, 0, 0))],
        out_specs=pl.BlockSpec((tb, SG_WIDTH), lambda i: (i, 0)),
        out_shape=jax.ShapeDtypeStruct((m, SG_WIDTH), BF16),
        compiler_params=_cparams(("parallel",)),
        name="sg_mixer",
    )(seg_a, seg_a, ln_g.reshape(-1, 1, SG_WIDTH), ln_b.reshape(-1, 1, SG_WIDTH), sg_w, bias_tile)


def _ssd_kernel(z_ref, xc_ref, xp_ref, bc_ref, bp_ref, dt_ref,
                cwx_ref, cwb_ref, cbx_ref, cbb_ref, dtb_ref, aexp_ref, dexp_ref, ng_ref, e_ref,
                o_ref, st_ref):
    c = pl.program_id(1)

    @pl.when(c == 0)
    def _():
        st_ref[...] = jnp.zeros(st_ref.shape, F32)

    has_prev = c > 0
    srow = lax.broadcasted_iota(jnp.int32, (3 * CHUNK, 2 * CHUNK), 0)
    scol = lax.broadcasted_iota(jnp.int32, (3 * CHUNK, 2 * CHUNK), 1)
    shift = jnp.where(scol == CHUNK + (srow % CHUNK) - (3 - srow // CHUNK), 1.0, 0.0).astype(BF16)

    def conv(cur_ref, prev_ref, w_ref, b_ref):
        cur = cur_ref[...]
        prev = jnp.where(has_prev, prev_ref[...], jnp.zeros_like(cur))
        both = jnp.concatenate([prev, cur], axis=0)
        sh = jnp.dot(shift, both, preferred_element_type=F32)
        w = w_ref[...]
        acc = cur.astype(F32) * w[3:4, :] + b_ref[...]
        for k in range(SSM_CONV - 1):
            acc = acc + sh[k * CHUNK:(k + 1) * CHUNK, :] * w[k:k + 1, :]
        return _silu(acc)

    xs = conv(xc_ref, xp_ref, cwx_ref, cbx_ref)
    bcv = conv(bc_ref, bp_ref, cwb_ref, cbb_ref)
    gn = SSM_GROUPS * SSM_STATE
    bm = bcv[:, :gn].astype(BF16)
    cm = bcv[:, gn:].astype(BF16)

    def split3(v):
        p1 = v.astype(BF16)
        r1 = v - p1.astype(F32)
        p2 = r1.astype(BF16)
        return p1, p2, (r1 - p2.astype(F32)).astype(BF16)

    def rows_times(m01, v):
        m3 = jnp.concatenate([m01.astype(BF16)] * 3, axis=1)
        return jnp.dot(m3, jnp.concatenate(split3(v), axis=0), preferred_element_type=F32)

    lane = lax.broadcasted_iota(jnp.int32, (CHUNK, LANE), 1)
    dt = jnp.where(lane < SSM_HEADS, _softplus(dt_ref[...] + dtb_ref[...]), 0.0)
    d1, d2, d3 = split3(dt)
    dt3 = (d1.astype(F32) + pltpu.roll(d2.astype(F32), SSM_HEADS, 1)
           + pltpu.roll(d3.astype(F32), 2 * SSM_HEADS, 1)).astype(BF16)
    dt_e = jnp.dot(dt3, e_ref[...], preferred_element_type=F32)
    a_e = dt_e * aexp_ref[...]
    r64 = lax.broadcasted_iota(jnp.int32, (CHUNK, CHUNK), 0)
    c64 = lax.broadcasted_iota(jnp.int32, (CHUNK, CHUNK), 1)
    a_cs = rows_times(jnp.where(c64 <= r64, 1.0, 0.0), a_e)
    rl = lax.broadcasted_iota(jnp.int32, (CHUNK, SSM_INNER), 0)
    cl = lax.broadcasted_iota(jnp.int32, (CHUNK, SSM_INNER), 1) % SSM_HEADDIM
    diag = jnp.where(rl == cl, a_cs, 0.0)
    a_row = rows_times(jnp.ones((CHUNK, CHUNK), F32), diag)
    decay = jnp.exp(jnp.where(rl >= cl, a_cs - a_row, NEG_BIG))
    a_last = a_cs[CHUNK - 1:CHUNK, :]
    xd = xs * dt_e
    xe = (xd * jnp.exp(a_last - a_cs)).astype(BF16)
    xdb = xd.astype(BF16)
    ea = jnp.exp(a_cs)
    cdec = jnp.exp(a_last)

    br = lax.broadcasted_iota(jnp.int32, (SSM_GW, SSM_GW), 0) // SSM_HEADDIM
    bc_ = lax.broadcasted_iota(jnp.int32, (SSM_GW, SSM_GW), 1) // SSM_HEADDIM
    blockdiag = br == bc_
    nt = (((1,), (1,)), ((), ()))
    tn = (((0,), (0,)), ((), ()))
    ys = []
    for g in range(SSM_GROUPS):
        ns = slice(g * SSM_STATE, (g + 1) * SSM_STATE)
        ls = slice(g * SSM_GW, (g + 1) * SSM_GW)
        cg, bg = cm[:, ns], bm[:, ns]
        b_t = jnp.concatenate([bg] * SSM_HG, axis=0)
        cb = lax.dot_general(cg, b_t, nt, preferred_element_type=F32)
        mg = (cb * decay[:, ls]).astype(BF16)
        xg = xdb[:, ls]
        bd = jnp.where(blockdiag, jnp.concatenate([xg] * SSM_HG, axis=0), jnp.zeros((), BF16))
        y_diag = jnp.dot(mg, bd, preferred_element_type=F32)
        st = st_ref[g]
        y_off = jnp.dot(cg, st.astype(BF16), preferred_element_type=F32) * ea[:, ls]
        upd = lax.dot_general(bg, xe[:, ls], tn, preferred_element_type=F32)
        st_ref[g] = st * cdec[:, ls] + upd
        ys.append(y_diag + y_off)
    y = jnp.concatenate(ys, axis=1) + xs * dexp_ref[...]
    y = y * _silu(z_ref[...].astype(F32))
    outs = []
    for g in range(SSM_GROUPS):
        yg = y[:, g * SSM_GW:(g + 1) * SSM_GW]
        ms = jnp.mean(yg * yg, axis=-1, keepdims=True)
        outs.append(yg * lax.rsqrt(ms + EPS))
    o_ref[...] = (jnp.concatenate(outs, axis=1) * ng_ref[...]).astype(o_ref.dtype)


def _head_expand_matrix():
    r = jnp.arange(LANE)[:, None]
    c = jnp.arange(SSM_INNER)[None, :]
    return ((r % SSM_HEADS == c // SSM_HEADDIM) & (r < 3 * SSM_HEADS)).astype(BF16)


def _ssd_mixer(seg_a, seg_s, conv_w, conv_b, dtb_pad, a_exp, d_exp, norm_g, e_mat, l, bsz, seq):
    nc = seq // CHUNK
    zb, xb, bb = C_Z // SSM_INNER, C_XBC // SSM_INNER, C_XBC // SSM_INNER + 1

    def row(b, c):
        return b * nc + c

    def prow(b, c):
        return b * nc + jnp.maximum(c - 1, 0)

    vec = lambda blk: pl.BlockSpec((None, 1, SSM_INNER), lambda b, c: (l, 0, blk))
    return pl.pallas_call(
        _ssd_kernel,
        grid=(bsz, nc),
        in_specs=[pl.BlockSpec((CHUNK, SSM_INNER), lambda b, c: (row(b, c), zb)),
                  pl.BlockSpec((CHUNK, SSM_INNER), lambda b, c: (row(b, c), xb)),
                  pl.BlockSpec((CHUNK, SSM_INNER), lambda b, c: (prow(b, c), xb)),
                  pl.BlockSpec((CHUNK, SSM_INNER), lambda b, c: (row(b, c), bb)),
                  pl.BlockSpec((CHUNK, SSM_INNER), lambda b, c: (prow(b, c), bb)),
                  pl.BlockSpec((CHUNK, LANE), lambda b, c: (row(b, c), 1)),
                  pl.BlockSpec((None, SSM_CONV, SSM_INNER), lambda b, c: (l, 0, 0)),
                  pl.BlockSpec((None, SSM_CONV, SSM_INNER), lambda b, c: (l, 0, 1)),
                  vec(0), vec(1),
                  pl.BlockSpec((None, 1, LANE), lambda b, c: (l, 0, 0)),
                  vec(0), vec(0), vec(0),
                  pl.BlockSpec((LANE, SSM_INNER), lambda b, c: (0, 0))],
        out_specs=pl.BlockSpec((CHUNK, SSM_INNER), lambda b, c: (row(b, c), 0)),
        out_shape=jax.ShapeDtypeStruct((bsz * seq, SSM_INNER), BF16),
        scratch_shapes=[pltpu.VMEM((SSM_GROUPS, SSM_STATE, SSM_GW), F32)],
        compiler_params=_cparams(("parallel", "arbitrary")),
        name="ssd_mixer",
    )(seg_a, seg_a, seg_a, seg_a, seg_a, seg_s, conv_w, conv_w,
      conv_b.reshape(-1, 1, 2 * SSM_INNER), conv_b.reshape(-1, 1, 2 * SSM_INNER),
      dtb_pad, a_exp, d_exp, norm_g.reshape(-1, 1, SSM_INNER), e_mat)


def _rope128(x, cos, sin_signed):
    return x * cos + pltpu.roll(x, ATT_HEADDIM // 2, 1) * sin_signed


def _rope64(x, cos, sin_signed):
    lane = lax.broadcasted_iota(jnp.int32, x.shape, 1)
    low = (lane % IDX_HEADDIM) < IDX_HEADDIM // 2
    rot = jnp.where(low, pltpu.roll(x, LANE - IDX_HEADDIM // 2, 1), pltpu.roll(x, IDX_HEADDIM // 2, 1))
    return x * cos + rot * sin_signed


def _dsa_kernel(q_ref, k_ref, v_ref, qi_ref, ki_ref, wi_ref,
                cq_ref, sq_ref, ck_ref, sk_ref, ciq_ref, siq_ref, cik_ref, sik_ref, *rest,
                klen, q0, topk):
    o_ref, kr_ref, vb_ref, kir_ref, key_ref, bias_ref, pos_ref = rest[-7:]
    i = pl.program_id(1)
    nt = (((1,), (1,)), ((), ()))
    nlc = klen // LANE

    @pl.when(i == 0)
    def _():
        for kv in range(ATT_KV_HEADS):
            hs = slice(kv * ATT_HEADDIM, (kv + 1) * ATT_HEADDIM)
            kr_ref[:, hs] = _rope128(k_ref[:, hs], ck_ref[...], sk_ref[...]).astype(BF16)
        vb_ref[...] = v_ref[...].astype(BF16)
        kx = _rope64(ki_ref[...], cik_ref[...], sik_ref[...])
        kx_hi = kx.astype(BF16)
        kir_ref[:, 0:LANE] = kx_hi
        kir_ref[:, LANE:2 * LANE] = (kx - kx_hi.astype(F32)).astype(BF16)

    lane_q = lax.broadcasted_iota(jnp.int32, (Q_BLOCK, LANE), 1)
    kir = kir_ref[...]
    wi = wi_ref[...] * np.float32(IDX_HEADS ** -0.5)
    iscore = jnp.zeros((Q_BLOCK, klen), F32)
    for quad in range(IDX_HEADS // 4):
        parts = []
        for pair in range(2 * quad, 2 * quad + 2):
            ps = slice(pair * LANE, (pair + 1) * LANE)
            qp = _rope64(qi_ref[:, ps], ciq_ref[...], siq_ref[...])
            q_hi = qp.astype(BF16).astype(F32)
            q_lo_swapped = pltpu.roll(qp - q_hi, IDX_HEADDIM, 1)
            for sub in range(2):
                own = (lane_q // IDX_HEADDIM) == sub
                parts.append(jnp.concatenate([jnp.where(own, q_hi, q_lo_swapped), jnp.where(own, q_hi, 0.0)],
                                             axis=1).astype(BF16))
        logits = lax.dot_general(jnp.concatenate(parts, axis=0), kir, nt,
                                 preferred_element_type=F32)
        for hh in range(4):
            h = 4 * quad + hh
            wcol = wi[:, S_WI_LANE + h:S_WI_LANE + h + 1]
            iscore = iscore + jnp.maximum(logits[hh * Q_BLOCK:(hh + 1) * Q_BLOCK, :], 0.0) * wcol

    qchunk = (lax.broadcasted_iota(jnp.int32, (Q_BLOCK, klen), 0) + (q0 + i) * Q_BLOCK) // CHUNK
    kchunk = lax.broadcasted_iota(jnp.int32, (Q_BLOCK, klen), 1) // CHUNK
    iscore = jnp.where(iscore == 0.0, 0.0, iscore)
    bits = pltpu.bitcast(iscore, jnp.int32)
    key = jnp.where(bits < 0, bits ^ jnp.int32(0x7FFFFFFF), bits)
    key = jnp.maximum(key, jnp.int32(INT_MIN + 1))
    key_ref[...] = jnp.where(kchunk <= qchunk, key, jnp.int32(INT_MIN))

    def row_count(pred, rows=slice(0, Q_BLOCK)):
        nrow = rows.stop - rows.start
        acc = jnp.zeros((nrow, LANE), F32)
        for cidx in range(nlc):
            acc = acc + jnp.where(pred(key_ref[rows, cidx * LANE:(cidx + 1) * LANE], cidx), 1.0, 0.0)
        return jnp.broadcast_to(jnp.sum(acc, axis=-1, keepdims=True), (nrow, LANE))

    halves = (slice(0, Q_BLOCK // 2), slice(Q_BLOCK // 2, Q_BLOCK))

    def thr_step(it, t_us):
        bit = jnp.left_shift(jnp.int32(1), 31 - it)
        out = []
        for rows, t_u in zip(halves, t_us):
            cand_u = t_u | bit
            cand = cand_u ^ jnp.int32(INT_MIN)
            out.append(jnp.where(row_count(lambda kc, _: kc >= cand, rows) >= topk, cand_u, t_u))
        return tuple(out)

    t_us = lax.fori_loop(0, 32, thr_step, tuple(jnp.zeros((Q_BLOCK // 2, LANE), jnp.int32) for _ in halves),
                         unroll=8)
    thr = jnp.concatenate(t_us, axis=0) ^ jnp.int32(INT_MIN)
    cnt_ge = row_count(lambda kc, _: kc >= thr)
    cnt_gt = row_count(lambda kc, _: kc > thr)
    need = topk - cnt_gt
    excess = jnp.where(thr > INT_MIN, cnt_ge - topk, 0.0)
    pos_ref[...] = jnp.full((Q_BLOCK, LANE), klen, jnp.int32)
    lane_pos = lax.broadcasted_iota(jnp.int32, (Q_BLOCK, LANE), 1)

    @pl.when(jnp.max(excess) > 0.0)
    def _():
        nbits = int(klen - 1).bit_length()
        never = jnp.int32(1 << 30)

        def pos_step(it, bound):
            cand = bound | jnp.left_shift(jnp.int32(1), nbits - 1 - it)
            ties = row_count(lambda kc, cidx: jnp.where(kc == thr, lane_pos + cidx * LANE, never) < cand)
            return jnp.where(ties < need, cand, bound)

        bound = lax.fori_loop(0, nbits, pos_step, jnp.zeros((Q_BLOCK, LANE), jnp.int32))
        pos_ref[...] = bound + 1

    pos = pos_ref[...]
    for cidx in range(nlc):
        cs = slice(cidx * LANE, (cidx + 1) * LANE)
        kc = key_ref[:, cs]
        tie = jnp.where(lane_pos + cidx * LANE < pos, 0.0, NEG_BIG)
        sel = jnp.where(kc > thr, 0.0, jnp.where(kc == thr, tie, NEG_BIG))
        bias_ref[:, cs] = jnp.where(kc == INT_MIN, NEG_BIG, sel)

    scale = np.float32(ATT_HEADDIM ** -0.5)
    for kv in range(ATT_KV_HEADS):
        hs = slice(kv * ATT_HEADDIM, (kv + 1) * ATT_HEADDIM)
        krh = kr_ref[:, hs]
        vh = vb_ref[:, hs]
        heads = [slice((kv * ATT_GRP + gq) * ATT_HEADDIM, (kv * ATT_GRP + gq + 1) * ATT_HEADDIM)
                 for gq in range(ATT_GRP)]
        qg = jnp.concatenate([(_rope128(q_ref[:, qs], cq_ref[...], sq_ref[...]) * scale).astype(BF16)
                              for qs in heads], axis=0)
        s = lax.dot_general(qg, krh, nt, preferred_element_type=F32)
        es, dens = [], []
        for gq in range(ATT_GRP):
            sg = s[gq * Q_BLOCK:(gq + 1) * Q_BLOCK, :] + bias_ref[...]
            e = jnp.exp(sg - jnp.max(sg, axis=-1, keepdims=True))
            dens.append(jnp.sum(e, axis=-1, keepdims=True))
            es.append(e.astype(BF16))
        o = jnp.dot(jnp.concatenate(es, axis=0), vh, preferred_element_type=F32)
        for gq, qs in enumerate(heads):
            o_ref[:, qs] = (o[gq * Q_BLOCK:(gq + 1) * Q_BLOCK, :] / dens[gq]).astype(o_ref.dtype)


DSA_BUCKETS = 4


def _dsa_mixer(seg_q, seg_s, tabs, bsz, seq):
    nqb = seq // Q_BLOCK
    topk = min(IDX_TOPK, seq // 4)
    cos128, sin128, cos64, sin64 = tabs
    kb, vb_, qib = 1024 // 256, 1280 // 256, 1536 // 512
    nbk = DSA_BUCKETS if nqb % DSA_BUCKETS == 0 else 1
    qpb = nqb // nbk
    width = ATT_HEADS * ATT_HEADDIM
    kvw = ATT_KV_HEADS * ATT_HEADDIM
    out = None
    for u in range(nbk):
        q0 = u * qpb
        klen = (u + 1) * qpb * Q_BLOCK
        qtab = pl.BlockSpec((Q_BLOCK, LANE), lambda b, i, q0=q0: (q0 + i, 0))
        ktab = pl.BlockSpec((klen, LANE), lambda b, i: (0, 0))
        qrow = lambda blk, q0=q0: (lambda b, i: (b, q0 + i, blk))
        in_specs = [pl.BlockSpec((None, Q_BLOCK, width), qrow(0)),
                    pl.BlockSpec((None, klen, kvw), lambda b, i: (b, 0, kb)),
                    pl.BlockSpec((None, klen, kvw), lambda b, i: (b, 0, vb_)),
                    pl.BlockSpec((None, Q_BLOCK, IDX_HEADS * IDX_HEADDIM), qrow(qib)),
                    pl.BlockSpec((None, klen, LANE), lambda b, i: (b, 0, 0)),
                    pl.BlockSpec((None, Q_BLOCK, LANE), qrow(1)),
                    qtab, qtab, ktab, ktab, qtab, qtab, ktab, ktab]
        args = [seg_q, seg_q, seg_q, seg_q, seg_s, seg_s,
                cos128, sin128, cos128, sin128, cos64, sin64, cos64, sin64]
        aliases = {}
        if out is not None:
            in_specs.append(pl.BlockSpec(memory_space=pl.ANY))
            args.append(out)
            aliases = {len(args) - 1: 0}
        out = pl.pallas_call(
            functools.partial(_dsa_kernel, klen=klen, q0=q0, topk=topk),
            grid=(bsz, qpb),
            in_specs=in_specs,
            out_specs=pl.BlockSpec((None, Q_BLOCK, width), qrow(0)),
            out_shape=jax.ShapeDtypeStruct((bsz, seq, width), BF16),
            scratch_shapes=[pltpu.VMEM((klen, kvw), BF16),
                            pltpu.VMEM((klen, kvw), BF16),
                            pltpu.VMEM((klen, 2 * LANE), BF16),
                            pltpu.VMEM((Q_BLOCK, klen), jnp.int32),
                            pltpu.VMEM((Q_BLOCK, klen), F32),
                            pltpu.VMEM((Q_BLOCK, LANE), jnp.int32)],
            input_output_aliases=aliases,
            compiler_params=_cparams(("parallel", "arbitrary")),
            name="dsa_mixer",
        )(*args)
    return out


def _rope_tables(seq):
    pos = jnp.arange(seq, dtype=F32)[:, None]

    def tab(half, reps):
        inv = ROPE_THETA ** (-jnp.arange(half, dtype=F32) / half)
        ang = pos * inv[None, :]
        cos, sin = jnp.cos(ang), jnp.sin(ang)
        return (jnp.tile(jnp.concatenate([cos, cos], axis=1), (1, reps)),
                jnp.tile(jnp.concatenate([-sin, sin], axis=1), (1, reps)))

    cos128, sin128 = tab(ATT_HEADDIM // 2, 1)
    cos64, sin64 = tab(IDX_HEADDIM // 2, 2)
    return cos128, sin128, cos64, sin64


def _merge_kernel(h_ref, wg_ref, ya_ref, yb_ref, yc_ref, yd_ref, p_ref, o_ref, wcat_ref):
    nbr, _, tn = wg_ref.shape
    for i in range(nbr):
        wcat_ref[:, i * tn:(i + 1) * tn] = wg_ref[i]
    gates = _sigmoid(jnp.dot(h_ref[...], wcat_ref[...], preferred_element_type=F32))
    acc = None
    row = 0
    for i, y_ref in enumerate((ya_ref, yb_ref, yc_ref, yd_ref)):
        width = y_ref.shape[1]
        term = gates[:, i * tn:(i + 1) * tn] * jnp.dot(y_ref[...], p_ref[row:row + width, :],
                                                      preferred_element_type=F32)
        acc = term if acc is None else acc + term
        row += width
    o_ref[...] = acc.astype(o_ref.dtype)


def _gated_merge(h, ys, w_gate, w_branch, l, *, tm=1024, tn=256):
    m, d = h.shape
    tm = min(tm, m)
    nbr = len(ys)
    resident = lambda width: pl.BlockSpec((tm, width), lambda i, j: (i, 0), pipeline_mode=pl.Buffered(1))
    return pl.pallas_call(
        _merge_kernel,
        grid=(m // tm, d // tn),
        in_specs=[resident(d),
                  pl.BlockSpec((None, nbr, d, tn), lambda i, j: (l, 0, 0, j))]
                 + [resident(y.shape[1]) for y in ys]
                 + [pl.BlockSpec((None, w_branch.shape[1], tn), lambda i, j: (l, 0, j))],
        out_specs=pl.BlockSpec((tm, tn), lambda i, j: (i, j)),
        out_shape=jax.ShapeDtypeStruct((m, d), BF16),
        scratch_shapes=[pltpu.VMEM((d, nbr * tn), BF16)],
        compiler_params=_cparams(("parallel", "arbitrary")),
        name="gated_merge",
    )(h, w_gate, *ys, w_branch)


def _xattn_kernel(q_ref, kv_ref, o_ref):
    nt = (((1,), (1,)), ((), ()))
    scale = np.float32(MEM_HEADDIM ** -0.5)
    hw = MEM_HEADS * MEM_HEADDIM
    for h in range(MEM_HEADS):
        hs = slice(h * MEM_HEADDIM, (h + 1) * MEM_HEADDIM)
        s = lax.dot_general(q_ref[:, hs], kv_ref[:, hs], nt, preferred_element_type=F32) * scale
        mx = jnp.max(s, axis=-1, keepdims=True)
        e = jnp.exp(s - mx)
        den = jnp.sum(e, axis=-1, keepdims=True)
        vs = slice(hw + h * MEM_HEADDIM, hw + (h + 1) * MEM_HEADDIM)
        o = jnp.dot(e.astype(BF16), kv_ref[:, vs], preferred_element_type=F32)
        o_ref[:, hs] = (o / den).astype(o_ref.dtype)


def _xattn(q, kv, bsz, seq, mem_len):
    tq = min(512, seq)
    nq = seq // tq
    hw = MEM_HEADS * MEM_HEADDIM
    return pl.pallas_call(
        _xattn_kernel,
        grid=(bsz, nq),
        in_specs=[pl.BlockSpec((tq, hw), lambda b, i: (b * nq + i, 0)),
                  pl.BlockSpec((mem_len, 2 * hw), lambda b, i: (b, 0))],
        out_specs=pl.BlockSpec((tq, hw), lambda b, i: (b * nq + i, 0)),
        out_shape=jax.ShapeDtypeStruct((bsz * seq, hw), BF16),
        compiler_params=_cparams(("parallel", "arbitrary")),
        name="mem_xattn",
    )(q, kv)


def kernel(x, mem, g_ffn1, w_ffn1_in, w_ffn1_out, g_mix, w_in, pool_w, pool_scale, sg_ln_g, sg_ln_b, sg_w, sg_b, ssm_conv_w, ssm_conv_b, ssm_a_log, ssm_dt_bias, ssm_d, ssm_norm_g, w_branch, w_gate, w_out, g_mem, g_cross, w_mem_q, w_mem_kv, w_mem_o, g_ffn2, w_ffn2_in, w_ffn2_out, g_final):
    bsz, seq, d = x.shape
    mem_len = mem.shape[1]
    depth = w_in.shape[0]
    m = bsz * seq
    bf = lambda a: a.astype(BF16)

    w1i, w1o, w2i, w2o = w_ffn1_in, bf(w_ffn1_out), w_ffn2_in, bf(w_ffn2_out)
    w_in_t = jnp.swapaxes(w_in, 1, 2)
    w_q = w_in_t[:, C_Q:C_KI]
    w_ki = w_in_t[:, C_KI:C_WI]
    w_s = jnp.concatenate(
        [w_ki, w_ki, w_in_t[:, C_DT:C_Q], w_in_t[:, C_WI:],
         jnp.zeros((depth, LANE - SSM_HEADS - IDX_HEADS, d), F32)], axis=1)
    wg, wb, wo = bf(w_gate), bf(w_branch), bf(w_out)
    wmkv, wmo = w_mem_kv, w_mem_o
    pw = bf(pool_w)

    expand = lambda v: jnp.repeat(v, SSM_HEADDIM, axis=-1).reshape(depth, 1, SSM_INNER)
    a_exp = expand(-jnp.exp(ssm_a_log))
    d_exp = expand(ssm_d)
    dtb_pad = jnp.pad(ssm_dt_bias, ((0, 0), (0, LANE - SSM_HEADS))).reshape(depth, 1, LANE)
    e_mat = _head_expand_matrix()
    bias_tile = jnp.repeat(jnp.swapaxes(sg_b, 1, 2), SG_GW, axis=2)
    tabs = _rope_tables(seq)

    x2 = x.reshape(m, d)
    mem_n = _rmsnorm(mem.reshape(bsz * mem_len, d), g_mem, BF16)

    for l in range(depth):
        x2 = _ffn(x2, g_ffn1[l], w1i, w1o, l)

        h = _rmsnorm(x2, g_mix[l], BF16)
        seg_a = _matmul_nt(h, w_in_t, l, SEG_A, out_dtype=BF16)
        seg_q = _matmul_nt(h, w_q, l, SEG_Q, out_dtype=F32)
        seg_s = _matmul_nt(h, w_s, l, SEG_S, tn=256, out_dtype=F32)
        y_a = _pool_mixer(seg_a, pw, pool_scale, l, bsz, seq)
        y_b = _sg_mixer(seg_a, sg_ln_g, sg_ln_b, sg_w, bias_tile, l, m)
        y_c = _ssd_mixer(seg_a, seg_s, ssm_conv_w, ssm_conv_b, dtb_pad, a_exp, d_exp,
                         ssm_norm_g, e_mat, l, bsz, seq)
        y_d = _dsa_mixer(seg_q.reshape(bsz, seq, SEG_Q), seg_s.reshape(bsz, seq, SEG_S), tabs, bsz, seq)
        merged = _gated_merge(h, (y_a, y_b, y_c, y_d.reshape(m, -1)), wg, wb, l)
        x2 = _matmul_res(merged, wo, (l,), x2, 1.0)

        q = _norm_matmul(x2, g_cross[l], w_mem_q, (l,))
        kv = _matmul(mem_n, wmkv, (l,), 2 * MEM_HEADS * MEM_HEADDIM, out_dtype=BF16)
        att = _xattn(q, kv, bsz, seq, mem_len)
        x2, h2 = _matmul_res_norm(att, wmo, (l,), x2, g_ffn2[l])

        x2 = _ffn(x2, g_ffn2[l], w2i, w2o, l, h=h2)

    return _rmsnorm(x2, g_final, F32).reshape(bsz, seq, d)
```

```python
import functools

import jax
import jax.numpy as jnp
import numpy as np
from jax import lax
from jax.experimental import pallas as pl
from jax.experimental.pallas import tpu as pltpu

F32 = jnp.float32
BF16 = jnp.bfloat16

D_MODEL = 4096
FFN_DIM = 8192
CHUNK = 64
EPS = 1e-6
ROPE_THETA = 10000.0

POOL_WINDOWS = (2, 4, 8, 16)
POOL_GROUPS = 4
POOL_WIDTH = 2048
POOL_GW = POOL_WIDTH // POOL_GROUPS
POOL_PAD = 16

SG_WIDTH = 1024
SG_BLOCK = 128
SG_GROUPS = 4
SG_GW = SG_WIDTH // SG_GROUPS

SSM_HEADS = 16
SSM_HEADDIM = 64
SSM_INNER = SSM_HEADS * SSM_HEADDIM
SSM_GROUPS = 4
SSM_STATE = 128
SSM_CONV = 4
SSM_HG = SSM_HEADS // SSM_GROUPS
SSM_GW = SSM_INNER // SSM_GROUPS

ATT_HEADS = 8
ATT_KV_HEADS = 2
ATT_HEADDIM = 128
ATT_GRP = ATT_HEADS // ATT_KV_HEADS
IDX_HEADS = 8
IDX_HEADDIM = 64
IDX_TOPK = 256
Q_BLOCK = 128

MEM_HEADS = 4
MEM_HEADDIM = 128

C_POOL, C_U, C_V, C_Z, C_XBC = 0, 2048, 3072, 4096, 5120
C_DT, C_Q, C_K, C_VAL, C_QI, C_KI, C_WI = 7168, 7184, 8208, 8464, 8720, 9232, 9296
SEG_A = 7168
SEG_Q = 2048
SEG_S = 256
S_DT_LANE = 0
S_WI_LANE = 16

LANE = 128
VMEM_LIMIT = 56 * 1024 * 1024
NEG_BIG = -1e30
INT_MIN = -2147483648


def _cparams(sem):
    return pltpu.CompilerParams(dimension_semantics=sem, vmem_limit_bytes=VMEM_LIMIT)


def _sigmoid(x):
    return 1.0 / (1.0 + jnp.exp(-x))


def _silu(x):
    return x * _sigmoid(x)


def _gelu(x):
    return 0.5 * x * (1.0 + lax.erf(x * np.float32(1.0 / np.sqrt(2.0))))


def _softplus(x):
    return jnp.maximum(x, 0.0) + jnp.log1p(jnp.exp(-jnp.abs(x)))


def _rmsnorm_kernel(x_ref, g_ref, o_ref):
    x = x_ref[...]
    ms = jnp.mean(x * x, axis=-1, keepdims=True)
    o_ref[...] = (x * lax.rsqrt(ms + EPS) * g_ref[...]).astype(o_ref.dtype)


def _rmsnorm(x, g, out_dtype):
    m, d = x.shape
    tm = min(512, m)
    return pl.pallas_call(
        _rmsnorm_kernel,
        grid=(m // tm,),
        in_specs=[pl.BlockSpec((tm, d), lambda i: (i, 0)),
                  pl.BlockSpec((1, d), lambda i: (0, 0))],
        out_specs=pl.BlockSpec((tm, d), lambda i: (i, 0)),
        out_shape=jax.ShapeDtypeStruct((m, d), out_dtype),
        compiler_params=_cparams(("parallel",)),
        name="rmsnorm",
    )(x, g.reshape(1, d))


def _mm_kernel(x_ref, w_ref, o_ref):
    o_ref[...] = jnp.dot(x_ref[...], w_ref[...].astype(BF16), preferred_element_type=F32).astype(o_ref.dtype)


def _cast_tiles(src_ref, dst_ref):
    tc = dst_ref.shape[2]
    for t in range(dst_ref.shape[0]):
        dst_ref[t] = src_ref[:, t * tc:(t + 1) * tc].astype(BF16)


def _mm_res_kernel(x_ref, w_ref, r_ref, *rest, alpha, nside):
    o_ref = rest[nside]
    acc = jnp.dot(x_ref[...], w_ref[...].astype(BF16), preferred_element_type=F32)
    o_ref[...] = r_ref[...] + alpha * acc
    for k in range(nside):
        _cast_tiles(rest[k], rest[nside + 1 + k])


def _side_cast_specs(sides, gm, gn):
    in_specs, out_specs, out_shapes, args = [], [], [], []
    for src, l, tc in sides:
        _, rows, cols = src.shape
        rps = rows // (gm * gn)
        assert rps * gm * gn == rows and rps % 16 == 0 and cols % tc == 0, (src.shape, gm, gn)
        in_specs.append(pl.BlockSpec((None, rps, cols), lambda i, j, l=l: (l, i * gn + j, 0)))
        out_specs.append(pl.BlockSpec((cols // tc, rps, tc), lambda i, j: (0, i * gn + j, 0)))
        out_shapes.append(jax.ShapeDtypeStruct((cols // tc, rows, tc), BF16))
        args.append(src)
    return in_specs, out_specs, out_shapes, args


def _w_spec(widx, kdim, tn, col_blk0=0):
    if widx is None:
        return pl.BlockSpec((None, kdim, tn), lambda i, j: (j, 0, 0))
    lead = (None,) * len(widx)
    return pl.BlockSpec(lead + (kdim, tn), lambda i, j: (*widx, 0, j + col_blk0))


def _matmul(x, w, widx, n, *, col0=0, tm=1024, tn=512, out_dtype=BF16):
    m, kdim = x.shape
    tm, tn = min(tm, m), min(tn, n)
    return pl.pallas_call(
        _mm_kernel,
        grid=(m // tm, n // tn),
        in_specs=[pl.BlockSpec((tm, kdim), lambda i, j: (i, 0)),
                  _w_spec(widx, kdim, tn, col0 // tn)],
        out_specs=pl.BlockSpec((tm, tn), lambda i, j: (i, j)),
        out_shape=jax.ShapeDtypeStruct((m, n), out_dtype),
        compiler_params=_cparams(("parallel", "arbitrary")),
        name="matmul",
    )(x, w)


def _mm_nt_kernel(x_ref, wt_ref, o_ref):
    nt = (((1,), (1,)), ((), ()))
    o_ref[...] = lax.dot_general(x_ref[...], wt_ref[...].astype(BF16), nt,
                                 preferred_element_type=F32).astype(o_ref.dtype)


def _matmul_nt(x, wt, l, n, *, tm=1024, tn=512, out_dtype=BF16):
    m, kdim = x.shape
    tm, tn = min(tm, m), min(tn, n)
    return pl.pallas_call(
        _mm_nt_kernel,
        grid=(m // tm, n // tn),
        in_specs=[pl.BlockSpec((tm, kdim), lambda i, j: (i, 0)),
                  pl.BlockSpec((None, tn, kdim), lambda i, j: (l, j, 0))],
        out_specs=pl.BlockSpec((tm, tn), lambda i, j: (i, j)),
        out_shape=jax.ShapeDtypeStruct((m, n), out_dtype),
        compiler_params=_cparams(("parallel", "arbitrary")),
        name="matmul_nt",
    )(x, wt)


def _x_spec(tm, kdim):
    return pl.BlockSpec((tm, kdim), lambda i, j: (i, 0))


def _matmul_res(x, w, widx, res, alpha, *, tm=1024, tn=512, sides=()):
    m, kdim = x.shape
    n = res.shape[1]
    tm, tn = min(tm, m), min(tn, n)
    gm, gn = m // tm, n // tn
    s_in, s_out, s_shapes, s_args = _side_cast_specs(sides, gm, gn)
    tile = pl.BlockSpec((tm, tn), lambda i, j: (i, j))
    return pl.pallas_call(
        functools.partial(_mm_res_kernel, alpha=alpha, nside=len(sides)),
        grid=(gm, gn),
        in_specs=[_x_spec(tm, kdim), _w_spec(widx, kdim, tn), tile] + s_in,
        out_specs=[tile] + s_out,
        out_shape=[jax.ShapeDtypeStruct((m, n), F32)] + s_shapes,
        compiler_params=_cparams(("parallel", "arbitrary")),
        name="matmul_res",
    )(x, w, res, *s_args)


def _mm_res_norm_kernel(x_ref, w_ref, r_ref, g_ref, o_ref, h_ref):
    y = r_ref[...] + jnp.dot(x_ref[...], w_ref[...].astype(BF16), preferred_element_type=F32)
    o_ref[...] = y
    ms = jnp.mean(y * y, axis=-1, keepdims=True)
    h_ref[...] = (y * lax.rsqrt(ms + EPS) * g_ref[...]).astype(h_ref.dtype)


def _matmul_res_norm(x, w, widx, res, g, *, tm=256):
    m, kdim = x.shape
    n = res.shape[1]
    tm = min(tm, m)
    lead = (None,) * len(widx)
    row = pl.BlockSpec((tm, n), lambda i: (i, 0))
    return pl.pallas_call(
        _mm_res_norm_kernel,
        grid=(m // tm,),
        in_specs=[pl.BlockSpec((tm, kdim), lambda i: (i, 0)),
                  pl.BlockSpec(lead + (kdim, n), lambda i: (*widx, 0, 0)),
                  row,
                  pl.BlockSpec((1, n), lambda i: (0, 0))],
        out_specs=[row, row],
        out_shape=[jax.ShapeDtypeStruct((m, n), F32), jax.ShapeDtypeStruct((m, n), BF16)],
        compiler_params=_cparams(("parallel",)),
        name="matmul_res_norm",
    )(x, w, res, g.reshape(1, n))


def _norm_mm_kernel(x_ref, g_ref, w_ref, o_ref):
    x = x_ref[...]
    ms = jnp.mean(x * x, axis=-1, keepdims=True)
    xn = (x * lax.rsqrt(ms + EPS) * g_ref[...]).astype(BF16)
    o_ref[...] = jnp.dot(xn, w_ref[...].astype(BF16), preferred_element_type=F32).astype(o_ref.dtype)


def _norm_matmul(x, g, w, widx, *, tm=512, out_dtype=BF16):
    m, d = x.shape
    n = w.shape[-1]
    tm = min(tm, m)
    lead = (None,) * len(widx)
    return pl.pallas_call(
        _norm_mm_kernel,
        grid=(m // tm,),
        in_specs=[pl.BlockSpec((tm, d), lambda i: (i, 0)),
                  pl.BlockSpec((1, d), lambda i: (0, 0)),
                  pl.BlockSpec(lead + (d, n), lambda i: (*widx, 0, 0))],
        out_specs=pl.BlockSpec((tm, n), lambda i: (i, 0)),
        out_shape=jax.ShapeDtypeStruct((m, n), out_dtype),
        compiler_params=_cparams(("parallel",)),
        name="norm_matmul",
    )(x, g.reshape(1, d), w)


def _swiglu_kernel(x_ref, wg_ref, wu_ref, *rest, nside):
    o_ref, w_ref = rest[nside], rest[-1]
    tn = wg_ref.shape[1]
    w_ref[:, 0:tn] = wg_ref[...].astype(BF16)
    w_ref[:, tn:2 * tn] = wu_ref[...].astype(BF16)
    gu = jnp.dot(x_ref[...], w_ref[...], preferred_element_type=F32)
    o_ref[...] = (_silu(gu[:, 0:tn]) * gu[:, tn:2 * tn]).astype(o_ref.dtype)
    for k in range(nside):
        _cast_tiles(rest[k], rest[nside + 1 + k])


FFN_IN_TILE = 256
FFN_OUT_TILE = 256


def _swiglu_in(x, w, l, *, tm=1024, sides=()):
    m, kdim = x.shape
    f = w.shape[-1] // 2
    tm, tn = min(tm, m), FFN_IN_TILE
    gm, nb = m // tm, f // tn
    s_in, s_out, s_shapes, s_args = _side_cast_specs(sides, gm, nb)
    return pl.pallas_call(
        functools.partial(_swiglu_kernel, nside=len(sides)),
        grid=(gm, nb),
        in_specs=[_x_spec(tm, kdim),
                  pl.BlockSpec((None, kdim, tn), lambda i, j: (l, 0, j)),
                  pl.BlockSpec((None, kdim, tn), lambda i, j: (l, 0, j + nb))] + s_in,
        out_specs=[pl.BlockSpec((tm, tn), lambda i, j: (i, j))] + s_out,
        out_shape=[jax.ShapeDtypeStruct((m, f), BF16)] + s_shapes,
        scratch_shapes=[pltpu.VMEM((kdim, 2 * tn), BF16)],
        compiler_params=_cparams(("parallel", "arbitrary")),
        name="swiglu_in",
    )(x, w, w, *s_args)


def _ffn(x, h, w_in, w_out, l, *, sides=()):
    act, w_out_t = _swiglu_in(h, w_in, l, sides=[(w_out, l, FFN_OUT_TILE)])
    return _matmul_res(act, w_out_t, None, x, 0.5, tm=1024, tn=FFN_OUT_TILE, sides=sides)


def _pool_kernel(a_ref, w_ref, s_ref, o_ref, pad_ref, *, seq, rows):
    g = pl.program_id(1)
    pad_ref[0:POOL_PAD, :] = jnp.zeros((POOL_PAD, POOL_GW), F32)
    pad_ref[POOL_PAD:POOL_PAD + seq, :] = a_ref[...].astype(F32)
    w = w_ref[...]
    scale = s_ref[...]
    for gi, win in enumerate(POOL_WINDOWS):

        @pl.when(g == gi)
        def _(win=win):
            for c in range(seq // rows):
                r0 = POOL_PAD + c * rows
                cur = pad_ref[r0:r0 + rows, :]
                tot = cur
                for k in range(1, win):
                    tot = tot + pad_ref[r0 - k:r0 - k + rows, :]
                t1 = lax.broadcasted_iota(jnp.int32, (rows, POOL_GW), 0) + (c * rows + 1)
                cnt = jnp.minimum(t1, win).astype(F32)
                mixed = (tot / cnt - cur).astype(BF16)
                y = jnp.dot(mixed, w, preferred_element_type=F32) * scale
                o_ref[c * rows:(c + 1) * rows, :] = y.astype(o_ref.dtype)


def _pool_mixer(seg_a, pool_w, pool_scale, l, bsz, seq):
    rows = min(256, seq)
    return pl.pallas_call(
        functools.partial(_pool_kernel, seq=seq, rows=rows),
        grid=(bsz, POOL_GROUPS),
        in_specs=[pl.BlockSpec((seq, POOL_GW), lambda b, g: (b, g)),
                  pl.BlockSpec((None, None, POOL_GW, POOL_GW), lambda b, g: (l, g, 0, 0)),
                  pl.BlockSpec((None, None, 1, POOL_GW), lambda b, g: (l, g, 0, 0))],
        out_specs=pl.BlockSpec((seq, POOL_GW), lambda b, g: (b, g)),
        out_shape=jax.ShapeDtypeStruct((bsz * seq, POOL_WIDTH), BF16),
        scratch_shapes=[pltpu.VMEM((POOL_PAD + seq, POOL_GW), F32)],
        compiler_params=_cparams(("parallel", "arbitrary")),
        name="pool_mixer",
    )(seg_a, pool_w, pool_scale.reshape(pool_scale.shape[0], POOL_GROUPS, 1, POOL_GW))


def _sg_kernel(u_ref, v_ref, g_ref, b_ref, w_ref, bias_ref, o_ref, *, nblk):
    ri = lax.broadcasted_iota(jnp.int32, (SG_BLOCK, SG_BLOCK), 0) // CHUNK
    ci = lax.broadcasted_iota(jnp.int32, (SG_BLOCK, SG_BLOCK), 1) // CHUNK
    causal = ri >= ci
    wm = [jnp.where(causal, w_ref[gi], 0.0).astype(BF16) for gi in range(SG_GROUPS)]
    bias = bias_ref[...]
    for n in range(nblk):
        rs = slice(n * SG_BLOCK, (n + 1) * SG_BLOCK)
        v = _gelu(v_ref[rs, :].astype(F32))
        mu = jnp.mean(v, axis=-1, keepdims=True)
        vc = v - mu
        var = jnp.mean(vc * vc, axis=-1, keepdims=True)
        vn = (vc * lax.rsqrt(var + EPS) * g_ref[...] + b_ref[...]).astype(BF16)
        u = _gelu(u_ref[rs, :].astype(F32))
        for gi in range(SG_GROUPS):
            cs = slice(gi * SG_GW, (gi + 1) * SG_GW)
            sv = jnp.dot(wm[gi], vn[:, cs], preferred_element_type=F32) + bias[:, cs]
            o_ref[rs, cs] = (u[:, cs] * sv).astype(o_ref.dtype)


def _sg_mixer(seg_a, ln_g, ln_b, sg_w, bias_tile, l, m):
    tb = min(512, m)
    ub, vb = C_U // SG_WIDTH, C_V // SG_WIDTH
    return pl.pallas_call(
        functools.partial(_sg_kernel, nblk=tb // SG_BLOCK),
        grid=(m // tb,),
        in_specs=[pl.BlockSpec((tb, SG_WIDTH), lambda i: (i, ub)),
                  pl.BlockSpec((tb, SG_WIDTH), lambda i: (i, vb)),
                  pl.BlockSpec((None, 1, SG_WIDTH), lambda i: (l, 0, 0)),
                  pl.BlockSpec((None, 1, SG_WIDTH), lambda i: (l, 0, 0)),
                  pl.BlockSpec((None, SG_GROUPS, SG_BLOCK, SG_BLOCK), lambda i: (l, 0, 0, 0)),
                  pl.BlockSpec((None, SG_BLOCK, SG_WIDTH), lambda i: (l, 0, 0))],
        out_specs=pl.BlockSpec((tb, SG_WIDTH), lambda i: (i, 0)),
        out_shape=jax.ShapeDtypeStruct((m, SG_WIDTH), BF16),
        compiler_params=_cparams(("parallel",)),
        name="sg_mixer",
    )(seg_a, seg_a, ln_g.reshape(-1, 1, SG_WIDTH), ln_b.reshape(-1, 1, SG_WIDTH), sg_w, bias_tile)


def _ssd_kernel(z_ref, xc_ref, xp_ref, bc_ref, bp_ref, dt_ref,
                cwx_ref, cwb_ref, cbx_ref, cbb_ref, dtb_ref, aexp_ref, dexp_ref, ng_ref, e_ref,
                o_ref, st_ref):
    c = pl.program_id(1)

    @pl.when(c == 0)
    def _():
        st_ref[...] = jnp.zeros(st_ref.shape, F32)

    has_prev = c > 0
    srow = lax.broadcasted_iota(jnp.int32, (3 * CHUNK, 2 * CHUNK), 0)
    scol = lax.broadcasted_iota(jnp.int32, (3 * CHUNK, 2 * CHUNK), 1)
    shift = jnp.where(scol == CHUNK + (srow % CHUNK) - (3 - srow // CHUNK), 1.0, 0.0).astype(BF16)

    def conv(cur_ref, prev_ref, w_ref, b_ref):
        cur = cur_ref[...]
        prev = jnp.where(has_prev, prev_ref[...], jnp.zeros_like(cur))
        both = jnp.concatenate([prev, cur], axis=0)
        sh = jnp.dot(shift, both, preferred_element_type=F32)
        w = w_ref[...]
        acc = cur.astype(F32) * w[3:4, :] + b_ref[...]
        for k in range(SSM_CONV - 1):
            acc = acc + sh[k * CHUNK:(k + 1) * CHUNK, :] * w[k:k + 1, :]
        return _silu(acc)

    xs = conv(xc_ref, xp_ref, cwx_ref, cbx_ref)
    bcv = conv(bc_ref, bp_ref, cwb_ref, cbb_ref)
    gn = SSM_GROUPS * SSM_STATE
    bm = bcv[:, :gn].astype(BF16)
    cm = bcv[:, gn:].astype(BF16)

    def split3(v):
        p1 = v.astype(BF16)
        r1 = v - p1.astype(F32)
        p2 = r1.astype(BF16)
        return p1, p2, (r1 - p2.astype(F32)).astype(BF16)

    def rows_times(m01, v):
        m3 = jnp.concatenate([m01.astype(BF16)] * 3, axis=1)
        return jnp.dot(m3, jnp.concatenate(split3(v), axis=0), preferred_element_type=F32)

    lane = lax.broadcasted_iota(jnp.int32, (CHUNK, LANE), 1)
    dt = jnp.where(lane < SSM_HEADS, _softplus(dt_ref[...] + dtb_ref[...]), 0.0)
    d1, d2, d3 = split3(dt)
    dt3 = (d1.astype(F32) + pltpu.roll(d2.astype(F32), SSM_HEADS, 1)
           + pltpu.roll(d3.astype(F32), 2 * SSM_HEADS, 1)).astype(BF16)
    dt_e = jnp.dot(dt3, e_ref[...], preferred_element_type=F32)
    a_e = dt_e * aexp_ref[...]
    r64 = lax.broadcasted_iota(jnp.int32, (CHUNK, CHUNK), 0)
    c64 = lax.broadcasted_iota(jnp.int32, (CHUNK, CHUNK), 1)
    a_cs = rows_times(jnp.where(c64 <= r64, 1.0, 0.0), a_e)
    rl = lax.broadcasted_iota(jnp.int32, (CHUNK, SSM_INNER), 0)
    cl = lax.broadcasted_iota(jnp.int32, (CHUNK, SSM_INNER), 1) % SSM_HEADDIM
    diag = jnp.where(rl == cl, a_cs, 0.0)
    a_row = rows_times(jnp.ones((CHUNK, CHUNK), F32), diag)
    decay = jnp.exp(jnp.where(rl >= cl, a_cs - a_row, NEG_BIG))
    a_last = a_cs[CHUNK - 1:CHUNK, :]
    xd = xs * dt_e
    xe = (xd * jnp.exp(a_last - a_cs)).astype(BF16)
    xdb = xd.astype(BF16)
    ea = jnp.exp(a_cs)
    cdec = jnp.exp(a_last)

    br = lax.broadcasted_iota(jnp.int32, (SSM_GW, SSM_GW), 0) // SSM_HEADDIM
    bc_ = lax.broadcasted_iota(jnp.int32, (SSM_GW, SSM_GW), 1) // SSM_HEADDIM
    blockdiag = br == bc_
    nt = (((1,), (1,)), ((), ()))
    tn = (((0,), (0,)), ((), ()))
    ys = []
    for g in range(SSM_GROUPS):
        ns = slice(g * SSM_STATE, (g + 1) * SSM_STATE)
        ls = slice(g * SSM_GW, (g + 1) * SSM_GW)
        cg, bg = cm[:, ns], bm[:, ns]
        b_t = jnp.concatenate([bg] * SSM_HG, axis=0)
        cb = lax.dot_general(cg, b_t, nt, preferred_element_type=F32)
        mg = (cb * decay[:, ls]).astype(BF16)
        xg = xdb[:, ls]
        bd = jnp.where(blockdiag, jnp.concatenate([xg] * SSM_HG, axis=0), jnp.zeros((), BF16))
        y_diag = jnp.dot(mg, bd, preferred_element_type=F32)
        st = st_ref[g]
        y_off = jnp.dot(cg, st.astype(BF16), preferred_element_type=F32) * ea[:, ls]
        upd = lax.dot_general(bg, xe[:, ls], tn, preferred_element_type=F32)
        st_ref[g] = st * cdec[:, ls] + upd
        ys.append(y_diag + y_off)
    y = jnp.concatenate(ys, axis=1) + xs * dexp_ref[...]
    y = y * _silu(z_ref[...].astype(F32))
    outs = []
    for g in range(SSM_GROUPS):
        yg = y[:, g * SSM_GW:(g + 1) * SSM_GW]
        ms = jnp.mean(yg * yg, axis=-1, keepdims=True)
        outs.append(yg * lax.rsqrt(ms + EPS))
    o_ref[...] = (jnp.concatenate(outs, axis=1) * ng_ref[...]).astype(o_ref.dtype)


def _head_expand_matrix():
    r = jnp.arange(LANE)[:, None]
    c = jnp.arange(SSM_INNER)[None, :]
    return ((r % SSM_HEADS == c // SSM_HEADDIM) & (r < 3 * SSM_HEADS)).astype(BF16)


def _ssd_mixer(seg_a, seg_s, conv_w, conv_b, dtb_pad, a_exp, d_exp, norm_g, e_mat, l, bsz, seq):
    nc = seq // CHUNK
    zb, xb, bb = C_Z // SSM_INNER, C_XBC // SSM_INNER, C_XBC // SSM_INNER + 1

    def row(b, c):
        return b * nc + c

    def prow(b, c):
        return b * nc + jnp.maximum(c - 1, 0)

    vec = lambda blk: pl.BlockSpec((None, 1, SSM_INNER), lambda b, c: (l, 0, blk))
    return pl.pallas_call(
        _ssd_kernel,
        grid=(bsz, nc),
        in_specs=[pl.BlockSpec((CHUNK, SSM_INNER), lambda b, c: (row(b, c), zb)),
                  pl.BlockSpec((CHUNK, SSM_INNER), lambda b, c: (row(b, c), xb)),
                  pl.BlockSpec((CHUNK, SSM_INNER), lambda b, c: (prow(b, c), xb)),
                  pl.BlockSpec((CHUNK, SSM_INNER), lambda b, c: (row(b, c), bb)),
                  pl.BlockSpec((CHUNK, SSM_INNER), lambda b, c: (prow(b, c), bb)),
                  pl.BlockSpec((CHUNK, LANE), lambda b, c: (row(b, c), 1)),
                  pl.BlockSpec((None, SSM_CONV, SSM_INNER), lambda b, c: (l, 0, 0)),
                  pl.BlockSpec((None, SSM_CONV, SSM_INNER), lambda b, c: (l, 0, 1)),
                  vec(0), vec(1),
                  pl.BlockSpec((None, 1, LANE), lambda b, c: (l, 0, 0)),
                  vec(0), vec(0), vec(0),
                  pl.BlockSpec((LANE, SSM_INNER), lambda b, c: (0, 0))],
        out_specs=pl.BlockSpec((CHUNK, SSM_INNER), lambda b, c: (row(b, c), 0)),
        out_shape=jax.ShapeDtypeStruct((bsz * seq, SSM_INNER), BF16),
        scratch_shapes=[pltpu.VMEM((SSM_GROUPS, SSM_STATE, SSM_GW), F32)],
        compiler_params=_cparams(("parallel", "arbitrary")),
        name="ssd_mixer",
    )(seg_a, seg_a, seg_a, seg_a, seg_a, seg_s, conv_w, conv_w,
      conv_b.reshape(-1, 1, 2 * SSM_INNER), conv_b.reshape(-1, 1, 2 * SSM_INNER),
      dtb_pad, a_exp, d_exp, norm_g.reshape(-1, 1, SSM_INNER), e_mat)


def _rope128(x, cos, sin_signed):
    return x * cos + pltpu.roll(x, ATT_HEADDIM // 2, 1) * sin_signed


def _rope64(x, cos, sin_signed):
    lane = lax.broadcasted_iota(jnp.int32, x.shape, 1)
    low = (lane % IDX_HEADDIM) < IDX_HEADDIM // 2
    rot = jnp.where(low, pltpu.roll(x, LANE - IDX_HEADDIM // 2, 1), pltpu.roll(x, IDX_HEADDIM // 2, 1))
    return x * cos + rot * sin_signed


def _dsa_kernel(q_ref, k_ref, v_ref, qi_ref, ki_ref, wi_ref,
                cq_ref, sq_ref, ck_ref, sk_ref, ciq_ref, siq_ref, cik_ref, sik_ref, *rest,
                klen, q0, topk):
    o_ref, kr_ref, vb_ref, kir_ref, key_ref, bias_ref, pos_ref = rest[-7:]
    i = pl.program_id(1)
    nt = (((1,), (1,)), ((), ()))
    nlc = klen // LANE

    @pl.when(i == 0)
    def _():
        for kv in range(ATT_KV_HEADS):
            hs = slice(kv * ATT_HEADDIM, (kv + 1) * ATT_HEADDIM)
            kr_ref[:, hs] = _rope128(k_ref[:, hs], ck_ref[...], sk_ref[...]).astype(BF16)
        vb_ref[...] = v_ref[...].astype(BF16)
        kx = _rope64(ki_ref[...], cik_ref[...], sik_ref[...])
        kx_hi = kx.astype(BF16)
        kir_ref[:, 0:LANE] = kx_hi
        kir_ref[:, LANE:2 * LANE] = (kx - kx_hi.astype(F32)).astype(BF16)

    lane_q = lax.broadcasted_iota(jnp.int32, (Q_BLOCK, LANE), 1)
    kir = kir_ref[...]
    wi = wi_ref[...] * np.float32(IDX_HEADS ** -0.5)
    iscore = jnp.zeros((Q_BLOCK, klen), F32)
    for quad in range(IDX_HEADS // 4):
        parts = []
        for pair in range(2 * quad, 2 * quad + 2):
            ps = slice(pair * LANE, (pair + 1) * LANE)
            qp = _rope64(qi_ref[:, ps], ciq_ref[...], siq_ref[...])
            q_hi = qp.astype(BF16).astype(F32)
            q_lo_swapped = pltpu.roll(qp - q_hi, IDX_HEADDIM, 1)
            for sub in range(2):
                own = (lane_q // IDX_HEADDIM) == sub
                parts.append(jnp.concatenate([jnp.where(own, q_hi, q_lo_swapped), jnp.where(own, q_hi, 0.0)],
                                             axis=1).astype(BF16))
        logits = lax.dot_general(jnp.concatenate(parts, axis=0), kir, nt,
                                 preferred_element_type=F32)
        for hh in range(4):
            h = 4 * quad + hh
            wcol = wi[:, S_WI_LANE + h:S_WI_LANE + h + 1]
            iscore = iscore + jnp.maximum(logits[hh * Q_BLOCK:(hh + 1) * Q_BLOCK, :], 0.0) * wcol

    qchunk = (lax.broadcasted_iota(jnp.int32, (Q_BLOCK, klen), 0) + (q0 + i) * Q_BLOCK) // CHUNK
    kchunk = lax.broadcasted_iota(jnp.int32, (Q_BLOCK, klen), 1) // CHUNK
    iscore = jnp.where(iscore == 0.0, 0.0, iscore)
    bits = pltpu.bitcast(iscore, jnp.int32)
    key = jnp.where(bits < 0, bits ^ jnp.int32(0x7FFFFFFF), bits)
    key = jnp.maximum(key, jnp.int32(INT_MIN + 1))
    key_ref[...] = jnp.where(kchunk <= qchunk, key, jnp.int32(INT_MIN))

    def row_count(pred, rows=slice(0, Q_BLOCK)):
        nrow = rows.stop - rows.start
        acc = jnp.zeros((nrow, LANE), F32)
        for cidx in range(nlc):
            acc = acc + jnp.where(pred(key_ref[rows, cidx * LANE:(cidx + 1) * LANE], cidx), 1.0, 0.0)
        return jnp.broadcast_to(jnp.sum(acc, axis=-1, keepdims=True), (nrow, LANE))

    halves = (slice(0, Q_BLOCK // 2), slice(Q_BLOCK // 2, Q_BLOCK))

    def thr_step(it, t_us):
        bit = jnp.left_shift(jnp.int32(1), 31 - it)
        out = []
        for rows, t_u in zip(halves, t_us):
            cand_u = t_u | bit
            cand = cand_u ^ jnp.int32(INT_MIN)
            out.append(jnp.where(row_count(lambda kc, _: kc >= cand, rows) >= topk, cand_u, t_u))
        return tuple(out)

    t_us = lax.fori_loop(0, 32, thr_step, tuple(jnp.zeros((Q_BLOCK // 2, LANE), jnp.int32) for _ in halves),
                         unroll=8)
    thr = jnp.concatenate(t_us, axis=0) ^ jnp.int32(INT_MIN)
    cnt_ge = row_count(lambda kc, _: kc >= thr)
    cnt_gt = row_count(lambda kc, _: kc > thr)
    need = topk - cnt_gt
    excess = jnp.where(thr > INT_MIN, cnt_ge - topk, 0.0)
    pos_ref[...] = jnp.full((Q_BLOCK, LANE), klen, jnp.int32)
    lane_pos = lax.broadcasted_iota(jnp.int32, (Q_BLOCK, LANE), 1)

    @pl.when(jnp.max(excess) > 0.0)
    def _():
        nbits = int(klen - 1).bit_length()
        never = jnp.int32(1 << 30)

        def pos_step(it, bound):
            cand = bound | jnp.left_shift(jnp.int32(1), nbits - 1 - it)
            ties = row_count(lambda kc, cidx: jnp.where(kc == thr, lane_pos + cidx * LANE, never) < cand)
            return jnp.where(ties < need, cand, bound)

        bound = lax.fori_loop(0, nbits, pos_step, jnp.zeros((Q_BLOCK, LANE), jnp.int32))
        pos_ref[...] = bound + 1

    pos = pos_ref[...]
    for cidx in range(nlc):
        cs = slice(cidx * LANE, (cidx + 1) * LANE)
        kc = key_ref[:, cs]
        tie = jnp.where(lane_pos + cidx * LANE < pos, 0.0, NEG_BIG)
        sel = jnp.where(kc > thr, 0.0, jnp.where(kc == thr, tie, NEG_BIG))
        bias_ref[:, cs] = jnp.where(kc == INT_MIN, NEG_BIG, sel)

    scale = np.float32(ATT_HEADDIM ** -0.5)
    for kv in range(ATT_KV_HEADS):
        hs = slice(kv * ATT_HEADDIM, (kv + 1) * ATT_HEADDIM)
        krh = kr_ref[:, hs]
        vh = vb_ref[:, hs]
        heads = [slice((kv * ATT_GRP + gq) * ATT_HEADDIM, (kv * ATT_GRP + gq + 1) * ATT_HEADDIM)
                 for gq in range(ATT_GRP)]
        qg = jnp.concatenate([(_rope128(q_ref[:, qs], cq_ref[...], sq_ref[...]) * scale).astype(BF16)
                              for qs in heads], axis=0)
        s = lax.dot_general(qg, krh, nt, preferred_element_type=F32)
        es, dens = [], []
        for gq in range(ATT_GRP):
            sg = s[gq * Q_BLOCK:(gq + 1) * Q_BLOCK, :] + bias_ref[...]
            e = jnp.exp(sg - jnp.max(sg, axis=-1, keepdims=True))
            dens.append(jnp.sum(e, axis=-1, keepdims=True))
            es.append(e.astype(BF16))
        o = jnp.dot(jnp.concatenate(es, axis=0), vh, preferred_element_type=F32)
        for gq, qs in enumerate(heads):
            o_ref[:, qs] = (o[gq * Q_BLOCK:(gq + 1) * Q_BLOCK, :] / dens[gq]).astype(o_ref.dtype)


DSA_BUCKETS = 4


def _dsa_mixer(seg_q, seg_s, tabs, bsz, seq):
    nqb = seq // Q_BLOCK
    topk = min(IDX_TOPK, seq // 4)
    cos128, sin128, cos64, sin64 = tabs
    kb, vb_, qib = 1024 // 256, 1280 // 256, 1536 // 512
    nbk = DSA_BUCKETS if nqb % DSA_BUCKETS == 0 else 1
    qpb = nqb // nbk
    width = ATT_HEADS * ATT_HEADDIM
    kvw = ATT_KV_HEADS * ATT_HEADDIM
    out = None
    for u in range(nbk):
        q0 = u * qpb
        klen = (u + 1) * qpb * Q_BLOCK
        qtab = pl.BlockSpec((Q_BLOCK, LANE), lambda b, i, q0=q0: (q0 + i, 0))
        ktab = pl.BlockSpec((klen, LANE), lambda b, i: (0, 0))
        qrow = lambda blk, q0=q0: (lambda b, i: (b, q0 + i, blk))
        in_specs = [pl.BlockSpec((None, Q_BLOCK, width), qrow(0)),
                    pl.BlockSpec((None, klen, kvw), lambda b, i: (b, 0, kb)),
                    pl.BlockSpec((None, klen, kvw), lambda b, i: (b, 0, vb_)),
                    pl.BlockSpec((None, Q_BLOCK, IDX_HEADS * IDX_HEADDIM), qrow(qib)),
                    pl.BlockSpec((None, klen, LANE), lambda b, i: (b, 0, 0)),
                    pl.BlockSpec((None, Q_BLOCK, LANE), qrow(1)),
                    qtab, qtab, ktab, ktab, qtab, qtab, ktab, ktab]
        args = [seg_q, seg_q, seg_q, seg_q, seg_s, seg_s,
                cos128, sin128, cos128, sin128, cos64, sin64, cos64, sin64]
        aliases = {}
        if out is not None:
            in_specs.append(pl.BlockSpec(memory_space=pl.ANY))
            args.append(out)
            aliases = {len(args) - 1: 0}
        out = pl.pallas_call(
            functools.partial(_dsa_kernel, klen=klen, q0=q0, topk=topk),
            grid=(bsz, qpb),
            in_specs=in_specs,
            out_specs=pl.BlockSpec((None, Q_BLOCK, width), qrow(0)),
            out_shape=jax.ShapeDtypeStruct((bsz, seq, width), BF16),
            scratch_shapes=[pltpu.VMEM((klen, kvw), BF16),
                            pltpu.VMEM((klen, kvw), BF16),
                            pltpu.VMEM((klen, 2 * LANE), BF16),
                            pltpu.VMEM((Q_BLOCK, klen), jnp.int32),
                            pltpu.VMEM((Q_BLOCK, klen), F32),
                            pltpu.VMEM((Q_BLOCK, LANE), jnp.int32)],
            input_output_aliases=aliases,
            compiler_params=_cparams(("parallel", "arbitrary")),
            name="dsa_mixer",
        )(*args)
    return out


def _rope_tables(seq):
    pos = jnp.arange(seq, dtype=F32)[:, None]

    def tab(half, reps):
        inv = ROPE_THETA ** (-jnp.arange(half, dtype=F32) / half)
        ang = pos * inv[None, :]
        cos, sin = jnp.cos(ang), jnp.sin(ang)
        return (jnp.tile(jnp.concatenate([cos, cos], axis=1), (1, reps)),
                jnp.tile(jnp.concatenate([-sin, sin], axis=1), (1, reps)))

    cos128, sin128 = tab(ATT_HEADDIM // 2, 1)
    cos64, sin64 = tab(IDX_HEADDIM // 2, 2)
    return cos128, sin128, cos64, sin64


MERGE_TILE = 256


def _merge_kernel(h_ref, wg_ref, ya_ref, yb_ref, yc_ref, yd_ref, p_ref, o_ref):
    h = h_ref[...]
    d = h.shape[1]
    acc = None
    row = 0
    for i, y_ref in enumerate((ya_ref, yb_ref, yc_ref, yd_ref)):
        width = y_ref.shape[1]
        gate = _sigmoid(jnp.dot(h, wg_ref[i * d:(i + 1) * d, :], preferred_element_type=F32))
        term = gate * jnp.dot(y_ref[...], p_ref[row:row + width, :], preferred_element_type=F32)
        acc = term if acc is None else acc + term
        row += width
    o_ref[...] = acc.astype(o_ref.dtype)


def _gated_merge(h, ys, w_gate_t, w_branch, l, *, tm=1024):
    m, d = h.shape
    tm, tn = min(tm, m), MERGE_TILE
    resident = lambda width: pl.BlockSpec((tm, width), lambda i, j: (i, 0), pipeline_mode=pl.Buffered(1))
    return pl.pallas_call(
        _merge_kernel,
        grid=(m // tm, d // tn),
        in_specs=[resident(d),
                  pl.BlockSpec((None, len(ys) * d, tn), lambda i, j: (j, 0, 0))]
                 + [resident(y.shape[1]) for y in ys]
                 + [pl.BlockSpec((None, w_branch.shape[1], tn), lambda i, j: (l, 0, j))],
        out_specs=pl.BlockSpec((tm, tn), lambda i, j: (i, j)),
        out_shape=jax.ShapeDtypeStruct((m, d), BF16),
        compiler_params=_cparams(("parallel", "arbitrary")),
        name="gated_merge",
    )(h, w_gate_t, *ys, w_branch)


def _xattn_kernel(q_ref, kv_ref, o_ref):
    nt = (((1,), (1,)), ((), ()))
    scale = np.float32(MEM_HEADDIM ** -0.5)
    hw = MEM_HEADS * MEM_HEADDIM
    for h in range(MEM_HEADS):
        hs = slice(h * MEM_HEADDIM, (h + 1) * MEM_HEADDIM)
        s = lax.dot_general(q_ref[:, hs], kv_ref[:, hs], nt, preferred_element_type=F32) * scale
        mx = jnp.max(s, axis=-1, keepdims=True)
        e = jnp.exp(s - mx)
        den = jnp.sum(e, axis=-1, keepdims=True)
        vs = slice(hw + h * MEM_HEADDIM, hw + (h + 1) * MEM_HEADDIM)
        o = jnp.dot(e.astype(BF16), kv_ref[:, vs], preferred_element_type=F32)
        o_ref[:, hs] = (o / den).astype(o_ref.dtype)


def _xattn(q, kv, bsz, seq, mem_len):
    tq = min(512, seq)
    nq = seq // tq
    hw = MEM_HEADS * MEM_HEADDIM
    return pl.pallas_call(
        _xattn_kernel,
        grid=(bsz, nq),
        in_specs=[pl.BlockSpec((tq, hw), lambda b, i: (b * nq + i, 0)),
                  pl.BlockSpec((mem_len, 2 * hw), lambda b, i: (b, 0))],
        out_specs=pl.BlockSpec((tq, hw), lambda b, i: (b * nq + i, 0)),
        out_shape=jax.ShapeDtypeStruct((bsz * seq, hw), BF16),
        compiler_params=_cparams(("parallel", "arbitrary")),
        name="mem_xattn",
    )(q, kv)


def kernel(x, mem, g_ffn1, w_ffn1_in, w_ffn1_out, g_mix, w_in, pool_w, pool_scale, sg_ln_g, sg_ln_b, sg_w, sg_b, ssm_conv_w, ssm_conv_b, ssm_a_log, ssm_dt_bias, ssm_d, ssm_norm_g, w_branch, w_gate, w_out, g_mem, g_cross, w_mem_q, w_mem_kv, w_mem_o, g_ffn2, w_ffn2_in, w_ffn2_out, g_final):
    bsz, seq, d = x.shape
    mem_len = mem.shape[1]
    depth = w_in.shape[0]
    m = bsz * seq
    bf = lambda a: a.astype(BF16)

    w_in_t = jnp.swapaxes(w_in, 1, 2)
    w_q = w_in_t[:, C_Q:C_KI]
    w_ki = w_in_t[:, C_KI:C_WI]
    w_s = jnp.concatenate(
        [w_ki, w_ki, w_in_t[:, C_DT:C_Q], w_in_t[:, C_WI:],
         jnp.zeros((depth, LANE - SSM_HEADS - IDX_HEADS, d), F32)], axis=1)
    w_gate_rows = w_gate.reshape(depth, -1, d)
    wb = bf(w_branch)
    pw = bf(pool_w)

    expand = lambda v: jnp.repeat(v, SSM_HEADDIM, axis=-1).reshape(depth, 1, SSM_INNER)
    a_exp = expand(-jnp.exp(ssm_a_log))
    d_exp = expand(ssm_d)
    dtb_pad = jnp.pad(ssm_dt_bias, ((0, 0), (0, LANE - SSM_HEADS))).reshape(depth, 1, LANE)
    e_mat = _head_expand_matrix()
    bias_tile = jnp.repeat(jnp.swapaxes(sg_b, 1, 2), SG_GW, axis=2)
    tabs = _rope_tables(seq)

    x2 = x.reshape(m, d)
    mem_n = _rmsnorm(mem.reshape(bsz * mem_len, d), g_mem, BF16)

    for l in range(depth):
        x2, wg_t, wo_t = _ffn(x2, _rmsnorm(x2, g_ffn1[l], BF16), w_ffn1_in, w_ffn1_out, l,
                              sides=[(w_gate_rows, l, MERGE_TILE), (w_out, l, 512)])

        h = _rmsnorm(x2, g_mix[l], BF16)
        seg_a = _matmul_nt(h, w_in_t, l, SEG_A, out_dtype=BF16)
        seg_q = _matmul_nt(h, w_q, l, SEG_Q, out_dtype=F32)
        seg_s = _matmul_nt(h, w_s, l, SEG_S, tn=256, out_dtype=F32)
        y_a = _pool_mixer(seg_a, pw, pool_scale, l, bsz, seq)
        y_b = _sg_mixer(seg_a, sg_ln_g, sg_ln_b, sg_w, bias_tile, l, m)
        y_c = _ssd_mixer(seg_a, seg_s, ssm_conv_w, ssm_conv_b, dtb_pad, a_exp, d_exp,
                         ssm_norm_g, e_mat, l, bsz, seq)
        y_d = _dsa_mixer(seg_q.reshape(bsz, seq, SEG_Q), seg_s.reshape(bsz, seq, SEG_S), tabs, bsz, seq)
        merged = _gated_merge(h, (y_a, y_b, y_c, y_d.reshape(m, -1)), wg_t, wb, l)
        x2, = _matmul_res(merged, wo_t, None, x2, 1.0)

        q = _norm_matmul(x2, g_cross[l], w_mem_q, (l,))
        kv = _matmul(mem_n, w_mem_kv, (l,), 2 * MEM_HEADS * MEM_HEADDIM, out_dtype=BF16)
        att = _xattn(q, kv, bsz, seq, mem_len)
        x2, h2 = _matmul_res_norm(att, w_mem_o, (l,), x2, g_ffn2[l])

        x2, = _ffn(x2, h2, w_ffn2_in, w_ffn2_out, l)

    return _rmsnorm(x2, g_final, F32).reshape(bsz, seq, d)
```

```python
import functools

import jax
import jax.numpy as jnp
import numpy as np
from jax import lax
from jax.experimental import pallas as pl
from jax.experimental.pallas import tpu as pltpu

F32 = jnp.float32
BF16 = jnp.bfloat16

D_MODEL = 4096
FFN_DIM = 8192
CHUNK = 64
EPS = 1e-6
ROPE_THETA = 10000.0

POOL_WINDOWS = (2, 4, 8, 16)
POOL_GROUPS = 4
POOL_WIDTH = 2048
POOL_GW = POOL_WIDTH // POOL_GROUPS
POOL_PAD = 16

SG_WIDTH = 1024
SG_BLOCK = 128
SG_GROUPS = 4
SG_GW = SG_WIDTH // SG_GROUPS

SSM_HEADS = 16
SSM_HEADDIM = 64
SSM_INNER = SSM_HEADS * SSM_HEADDIM
SSM_GROUPS = 4
SSM_STATE = 128
SSM_CONV = 4
SSM_HG = SSM_HEADS // SSM_GROUPS
SSM_GW = SSM_INNER // SSM_GROUPS

ATT_HEADS = 8
ATT_KV_HEADS = 2
ATT_HEADDIM = 128
ATT_GRP = ATT_HEADS // ATT_KV_HEADS
IDX_HEADS = 8
IDX_HEADDIM = 64
IDX_TOPK = 256
Q_BLOCK = 128

MEM_HEADS = 4
MEM_HEADDIM = 128

C_POOL, C_U, C_V, C_Z, C_XBC = 0, 2048, 3072, 4096, 5120
C_DT, C_Q, C_K, C_VAL, C_QI, C_KI, C_WI = 7168, 7184, 8208, 8464, 8720, 9232, 9296
SEG_A = 7168
SEG_Q = 2048
SEG_S = 256
S_DT_LANE = 0
S_WI_LANE = 16

LANE = 128
VMEM_LIMIT = 56 * 1024 * 1024
NEG_BIG = -1e30
INT_MIN = -2147483648


def _cparams(sem):
    return pltpu.CompilerParams(dimension_semantics=sem, vmem_limit_bytes=VMEM_LIMIT)


def _sigmoid(x):
    return 1.0 / (1.0 + jnp.exp(-x))


def _silu(x):
    return x * _sigmoid(x)


def _gelu(x):
    return 0.5 * x * (1.0 + lax.erf(x * np.float32(1.0 / np.sqrt(2.0))))


def _softplus(x):
    return jnp.maximum(x, 0.0) + jnp.log1p(jnp.exp(-jnp.abs(x)))


def _rmsnorm_kernel(x_ref, g_ref, o_ref):
    x = x_ref[...]
    ms = jnp.mean(x * x, axis=-1, keepdims=True)
    o_ref[...] = (x * lax.rsqrt(ms + EPS) * g_ref[...]).astype(o_ref.dtype)


def _rmsnorm(x, g, out_dtype):
    m, d = x.shape
    tm = min(512, m)
    return pl.pallas_call(
        _rmsnorm_kernel,
        grid=(m // tm,),
        in_specs=[pl.BlockSpec((tm, d), lambda i: (i, 0)),
                  pl.BlockSpec((1, d), lambda i: (0, 0))],
        out_specs=pl.BlockSpec((tm, d), lambda i: (i, 0)),
        out_shape=jax.ShapeDtypeStruct((m, d), out_dtype),
        compiler_params=_cparams(("parallel",)),
        name="rmsnorm",
    )(x, g.reshape(1, d))


def _mm_kernel(x_ref, w_ref, o_ref):
    o_ref[...] = jnp.dot(x_ref[...], w_ref[...].astype(BF16), preferred_element_type=F32).astype(o_ref.dtype)


def _cast_tiles(src_ref, dst_ref):
    tc = dst_ref.shape[2]
    for t in range(dst_ref.shape[0]):
        dst_ref[t] = src_ref[:, t * tc:(t + 1) * tc].astype(BF16)


def _mm_res_kernel(x_ref, w_ref, r_ref, *rest, alpha, nside):
    o_ref = rest[nside]
    acc = jnp.dot(x_ref[...], w_ref[...].astype(BF16), preferred_element_type=F32)
    o_ref[...] = r_ref[...] + alpha * acc
    for k in range(nside):
        _cast_tiles(rest[k], rest[nside + 1 + k])


def _side_cast_specs(sides, gm, gn):
    in_specs, out_specs, out_shapes, args = [], [], [], []
    for src, l, tc in sides:
        _, rows, cols = src.shape
        rps = rows // (gm * gn)
        assert rps * gm * gn == rows and rps % 16 == 0 and cols % tc == 0, (src.shape, gm, gn)
        in_specs.append(pl.BlockSpec((None, rps, cols), lambda i, j, l=l: (l, i * gn + j, 0)))
        out_specs.append(pl.BlockSpec((cols // tc, rps, tc), lambda i, j: (0, i * gn + j, 0)))
        out_shapes.append(jax.ShapeDtypeStruct((cols // tc, rows, tc), BF16))
        args.append(src)
    return in_specs, out_specs, out_shapes, args


def _w_spec(widx, kdim, tn, col_blk0=0):
    if widx is None:
        return pl.BlockSpec((None, kdim, tn), lambda i, j: (j, 0, 0))
    lead = (None,) * len(widx)
    return pl.BlockSpec(lead + (kdim, tn), lambda i, j: (*widx, 0, j + col_blk0))


def _matmul(x, w, widx, n, *, col0=0, tm=1024, tn=512, out_dtype=BF16):
    m, kdim = x.shape
    tm, tn = min(tm, m), min(tn, n)
    return pl.pallas_call(
        _mm_kernel,
        grid=(m // tm, n // tn),
        in_specs=[pl.BlockSpec((tm, kdim), lambda i, j: (i, 0)),
                  _w_spec(widx, kdim, tn, col0 // tn)],
        out_specs=pl.BlockSpec((tm, tn), lambda i, j: (i, j)),
        out_shape=jax.ShapeDtypeStruct((m, n), out_dtype),
        compiler_params=_cparams(("parallel", "arbitrary")),
        name="matmul",
    )(x, w)


def _mm_nt_kernel(x_ref, wt_ref, o_ref):
    nt = (((1,), (1,)), ((), ()))
    o_ref[...] = lax.dot_general(x_ref[...], wt_ref[...].astype(BF16), nt,
                                 preferred_element_type=F32).astype(o_ref.dtype)


def _matmul_nt(x, wt, l, n, *, tm=1024, tn=512, out_dtype=BF16):
    m, kdim = x.shape
    tm, tn = min(tm, m), min(tn, n)
    return pl.pallas_call(
        _mm_nt_kernel,
        grid=(m // tm, n // tn),
        in_specs=[pl.BlockSpec((tm, kdim), lambda i, j: (i, 0)),
                  pl.BlockSpec((None, tn, kdim), lambda i, j: (l, j, 0))],
        out_specs=pl.BlockSpec((tm, tn), lambda i, j: (i, j)),
        out_shape=jax.ShapeDtypeStruct((m, n), out_dtype),
        compiler_params=_cparams(("parallel", "arbitrary")),
        name="matmul_nt",
    )(x, wt)


def _x_spec(tm, kdim):
    return pl.BlockSpec((tm, kdim), lambda i, j: (i, 0))


def _matmul_res(x, w, widx, res, alpha, *, tm=1024, tn=512, sides=()):
    m, kdim = x.shape
    n = res.shape[1]
    tm, tn = min(tm, m), min(tn, n)
    gm, gn = m // tm, n // tn
    s_in, s_out, s_shapes, s_args = _side_cast_specs(sides, gm, gn)
    tile = pl.BlockSpec((tm, tn), lambda i, j: (i, j))
    return pl.pallas_call(
        functools.partial(_mm_res_kernel, alpha=alpha, nside=len(sides)),
        grid=(gm, gn),
        in_specs=[_x_spec(tm, kdim), _w_spec(widx, kdim, tn), tile] + s_in,
        out_specs=[tile] + s_out,
        out_shape=[jax.ShapeDtypeStruct((m, n), F32)] + s_shapes,
        compiler_params=_cparams(("parallel", "arbitrary")),
        name="matmul_res",
    )(x, w, res, *s_args)


def _mm_res_norm_kernel(x_ref, w_ref, r_ref, g_ref, o_ref, h_ref):
    y = r_ref[...] + jnp.dot(x_ref[...], w_ref[...].astype(BF16), preferred_element_type=F32)
    o_ref[...] = y
    ms = jnp.mean(y * y, axis=-1, keepdims=True)
    h_ref[...] = (y * lax.rsqrt(ms + EPS) * g_ref[...]).astype(h_ref.dtype)


def _matmul_res_norm(x, w, widx, res, g, *, tm=256):
    m, kdim = x.shape
    n = res.shape[1]
    tm = min(tm, m)
    lead = (None,) * len(widx)
    row = pl.BlockSpec((tm, n), lambda i: (i, 0))
    return pl.pallas_call(
        _mm_res_norm_kernel,
        grid=(m // tm,),
        in_specs=[pl.BlockSpec((tm, kdim), lambda i: (i, 0)),
                  pl.BlockSpec(lead + (kdim, n), lambda i: (*widx, 0, 0)),
                  row,
                  pl.BlockSpec((1, n), lambda i: (0, 0))],
        out_specs=[row, row],
        out_shape=[jax.ShapeDtypeStruct((m, n), F32), jax.ShapeDtypeStruct((m, n), BF16)],
        compiler_params=_cparams(("parallel",)),
        name="matmul_res_norm",
    )(x, w, res, g.reshape(1, n))


def _norm_mm_kernel(x_ref, g_ref, w_ref, o_ref):
    x = x_ref[...]
    ms = jnp.mean(x * x, axis=-1, keepdims=True)
    xn = (x * lax.rsqrt(ms + EPS) * g_ref[...]).astype(BF16)
    o_ref[...] = jnp.dot(xn, w_ref[...].astype(BF16), preferred_element_type=F32).astype(o_ref.dtype)


def _norm_matmul(x, g, w, widx, *, tm=512, out_dtype=BF16):
    m, d = x.shape
    n = w.shape[-1]
    tm = min(tm, m)
    lead = (None,) * len(widx)
    return pl.pallas_call(
        _norm_mm_kernel,
        grid=(m // tm,),
        in_specs=[pl.BlockSpec((tm, d), lambda i: (i, 0)),
                  pl.BlockSpec((1, d), lambda i: (0, 0)),
                  pl.BlockSpec(lead + (d, n), lambda i: (*widx, 0, 0))],
        out_specs=pl.BlockSpec((tm, n), lambda i: (i, 0)),
        out_shape=jax.ShapeDtypeStruct((m, n), out_dtype),
        compiler_params=_cparams(("parallel",)),
        name="norm_matmul",
    )(x, g.reshape(1, d), w)


def _swiglu_kernel(x_ref, wg_ref, wu_ref, *rest, nside):
    o_ref, w_ref = rest[nside], rest[-1]
    tn = wg_ref.shape[1]
    w_ref[:, 0:tn] = wg_ref[...].astype(BF16)
    w_ref[:, tn:2 * tn] = wu_ref[...].astype(BF16)
    gu = jnp.dot(x_ref[...], w_ref[...], preferred_element_type=F32)
    o_ref[...] = (_silu(gu[:, 0:tn]) * gu[:, tn:2 * tn]).astype(o_ref.dtype)
    for k in range(nside):
        _cast_tiles(rest[k], rest[nside + 1 + k])


def _swiglu_bf16_kernel(x_ref, wg_ref, wu_ref, *rest, nside):
    x = x_ref[...]
    g = jnp.dot(x, wg_ref[...], preferred_element_type=F32)
    u = jnp.dot(x, wu_ref[...], preferred_element_type=F32)
    rest[nside][...] = (_silu(g) * u).astype(BF16)
    for k in range(nside):
        _cast_tiles(rest[k], rest[nside + 1 + k])


FFN_IN_TILE = 256
FFN_IN_TILE_BF16 = 512
FFN_OUT_TILE = 256


def _swiglu_in(x, w, l, *, tm=1024, sides=()):
    m, kdim = x.shape
    tm = min(tm, m)
    if l is None:
        tn = w.shape[2]
        f = w.shape[0] * tn // 2
        nb = f // tn
        w_specs = [pl.BlockSpec((None, kdim, tn), lambda i, j: (j, 0, 0)),
                   pl.BlockSpec((None, kdim, tn), lambda i, j: (j + nb, 0, 0))]
        body, scratch = _swiglu_bf16_kernel, []
    else:
        tn = FFN_IN_TILE
        f = w.shape[-1] // 2
        nb = f // tn
        w_specs = [pl.BlockSpec((None, kdim, tn), lambda i, j: (l, 0, j)),
                   pl.BlockSpec((None, kdim, tn), lambda i, j: (l, 0, j + nb))]
        body, scratch = _swiglu_kernel, [pltpu.VMEM((kdim, 2 * tn), BF16)]
    gm = m // tm
    s_in, s_out, s_shapes, s_args = _side_cast_specs(sides, gm, nb)
    return pl.pallas_call(
        functools.partial(body, nside=len(sides)),
        grid=(gm, nb),
        in_specs=[_x_spec(tm, kdim)] + w_specs + s_in,
        out_specs=[pl.BlockSpec((tm, tn), lambda i, j: (i, j))] + s_out,
        out_shape=[jax.ShapeDtypeStruct((m, f), BF16)] + s_shapes,
        scratch_shapes=scratch,
        compiler_params=_cparams(("parallel", "arbitrary")),
        name="swiglu_in",
    )(x, w, w, *s_args)


def _ffn(x, h, w_in, l_in, w_out, l, *, sides=()):
    act, w_out_t = _swiglu_in(h, w_in, l_in, sides=[(w_out, l, FFN_OUT_TILE)])
    return _matmul_res(act, w_out_t, None, x, 0.5, tm=1024, tn=FFN_OUT_TILE, sides=sides)


def _pool_kernel(a_ref, w_ref, s_ref, o_ref, pad_ref, *, seq, rows):
    g = pl.program_id(1)
    pad_ref[0:POOL_PAD, :] = jnp.zeros((POOL_PAD, POOL_GW), F32)
    pad_ref[POOL_PAD:POOL_PAD + seq, :] = a_ref[...].astype(F32)
    w = w_ref[...]
    scale = s_ref[...]
    for gi, win in enumerate(POOL_WINDOWS):

        @pl.when(g == gi)
        def _(win=win):
            for c in range(seq // rows):
                r0 = POOL_PAD + c * rows
                cur = pad_ref[r0:r0 + rows, :]
                tot = cur
                for k in range(1, win):
                    tot = tot + pad_ref[r0 - k:r0 - k + rows, :]
                t1 = lax.broadcasted_iota(jnp.int32, (rows, POOL_GW), 0) + (c * rows + 1)
                cnt = jnp.minimum(t1, win).astype(F32)
                mixed = (tot / cnt - cur).astype(BF16)
                y = jnp.dot(mixed, w, preferred_element_type=F32) * scale
                o_ref[c * rows:(c + 1) * rows, :] = y.astype(o_ref.dtype)


def _pool_mixer(seg_a, pool_w, pool_scale, l, bsz, seq):
    rows = min(256, seq)
    return pl.pallas_call(
        functools.partial(_pool_kernel, seq=seq, rows=rows),
        grid=(bsz, POOL_GROUPS),
        in_specs=[pl.BlockSpec((seq, POOL_GW), lambda b, g: (b, g)),
                  pl.BlockSpec((None, None, POOL_GW, POOL_GW), lambda b, g: (l, g, 0, 0)),
                  pl.BlockSpec((None, None, 1, POOL_GW), lambda b, g: (l, g, 0, 0))],
        out_specs=pl.BlockSpec((seq, POOL_GW), lambda b, g: (b, g)),
        out_shape=jax.ShapeDtypeStruct((bsz * seq, POOL_WIDTH), BF16),
        scratch_shapes=[pltpu.VMEM((POOL_PAD + seq, POOL_GW), F32)],
        compiler_params=_cparams(("parallel", "arbitrary")),
        name="pool_mixer",
    )(seg_a, pool_w, pool_scale.reshape(pool_scale.shape[0], POOL_GROUPS, 1, POOL_GW))


def _sg_kernel(u_ref, v_ref, g_ref, b_ref, w_ref, bias_ref, o_ref, *, nblk):
    ri = lax.broadcasted_iota(jnp.int32, (SG_BLOCK, SG_BLOCK), 0) // CHUNK
    ci = lax.broadcasted_iota(jnp.int32, (SG_BLOCK, SG_BLOCK), 1) // CHUNK
    causal = ri >= ci
    wm = [jnp.where(causal, w_ref[gi], 0.0).astype(BF16) for gi in range(SG_GROUPS)]
    bias = bias_ref[...]
    for n in range(nblk):
        rs = slice(n * SG_BLOCK, (n + 1) * SG_BLOCK)
        v = _gelu(v_ref[rs, :].astype(F32))
        mu = jnp.mean(v, axis=-1, keepdims=True)
        vc = v - mu
        var = jnp.mean(vc * vc, axis=-1, keepdims=True)
        vn = (vc * lax.rsqrt(var + EPS) * g_ref[...] + b_ref[...]).astype(BF16)
        u = _gelu(u_ref[rs, :].astype(F32))
        for gi in range(SG_GROUPS):
            cs = slice(gi * SG_GW, (gi + 1) * SG_GW)
            sv = jnp.dot(wm[gi], vn[:, cs], preferred_element_type=F32) + bias[:, cs]
            o_ref[rs, cs] = (u[:, cs] * sv).astype(o_ref.dtype)


def _sg_mixer(seg_a, ln_g, ln_b, sg_w, bias_tile, l, m):
    tb = min(512, m)
    ub, vb = C_U // SG_WIDTH, C_V // SG_WIDTH
    return pl.pallas_call(
        functools.partial(_sg_kernel, nblk=tb // SG_BLOCK),
        grid=(m // tb,),
        in_specs=[pl.BlockSpec((tb, SG_WIDTH), lambda i: (i, ub)),
                  pl.BlockSpec((tb, SG_WIDTH), lambda i: (i, vb)),
                  pl.BlockSpec((None, 1, SG_WIDTH), lambda i: (l, 0, 0)),
                  pl.BlockSpec((None, 1, SG_WIDTH), lambda i: (l, 0, 0)),
                  pl.BlockSpec((None, SG_GROUPS, SG_BLOCK, SG_BLOCK), lambda i: (l, 0, 0, 0)),
                  pl.BlockSpec((None, SG_BLOCK, SG_WIDTH), lambda i: (l, 0, 0))],
        out_specs=pl.BlockSpec((tb, SG_WIDTH), lambda i: (i, 0)),
        out_shape=jax.ShapeDtypeStruct((m, SG_WIDTH), BF16),
        compiler_params=_cparams(("parallel",)),
        name="sg_mixer",
    )(seg_a, seg_a, ln_g.reshape(-1, 1, SG_WIDTH), ln_b.reshape(-1, 1, SG_WIDTH), sg_w, bias_tile)


def _ssd_kernel(z_ref, xc_ref, xp_ref, bc_ref, bp_ref, dt_ref,
                cwx_ref, cwb_ref, cbx_ref, cbb_ref, dtb_ref, aexp_ref, dexp_ref, ng_ref, e_ref,
                o_ref, st_ref):
    c = pl.program_id(1)

    @pl.when(c == 0)
    def _():
        st_ref[...] = jnp.zeros(st_ref.shape, F32)

    has_prev = c > 0
    srow = lax.broadcasted_iota(jnp.int32, (3 * CHUNK, 2 * CHUNK), 0)
    scol = lax.broadcasted_iota(jnp.int32, (3 * CHUNK, 2 * CHUNK), 1)
    shift = jnp.where(scol == CHUNK + (srow % CHUNK) - (3 - srow // CHUNK), 1.0, 0.0).astype(BF16)

    def conv(cur_ref, prev_ref, w_ref, b_ref):
        cur = cur_ref[...]
        prev = jnp.where(has_prev, prev_ref[...], jnp.zeros_like(cur))
        both = jnp.concatenate([prev, cur], axis=0)
        sh = jnp.dot(shift, both, preferred_element_type=F32)
        w = w_ref[...]
        acc = cur.astype(F32) * w[3:4, :] + b_ref[...]
        for k in range(SSM_CONV - 1):
            acc = acc + sh[k * CHUNK:(k + 1) * CHUNK, :] * w[k:k + 1, :]
        return _silu(acc)

    xs = conv(xc_ref, xp_ref, cwx_ref, cbx_ref)
    bcv = conv(bc_ref, bp_ref, cwb_ref, cbb_ref)
    gn = SSM_GROUPS * SSM_STATE
    bm = bcv[:, :gn].astype(BF16)
    cm = bcv[:, gn:].astype(BF16)

    def split3(v):
        p1 = v.astype(BF16)
        r1 = v - p1.astype(F32)
        p2 = r1.astype(BF16)
        return p1, p2, (r1 - p2.astype(F32)).astype(BF16)

    def rows_times(m01, v):
        m3 = jnp.concatenate([m01.astype(BF16)] * 3, axis=1)
        return jnp.dot(m3, jnp.concatenate(split3(v), axis=0), preferred_element_type=F32)

    lane = lax.broadcasted_iota(jnp.int32, (CHUNK, LANE), 1)
    dt = jnp.where(lane < SSM_HEADS, _softplus(dt_ref[...] + dtb_ref[...]), 0.0)
    d1, d2, d3 = split3(dt)
    dt3 = (d1.astype(F32) + pltpu.roll(d2.astype(F32), SSM_HEADS, 1)
           + pltpu.roll(d3.astype(F32), 2 * SSM_HEADS, 1)).astype(BF16)
    dt_e = jnp.dot(dt3, e_ref[...], preferred_element_type=F32)
    a_e = dt_e * aexp_ref[...]
    r64 = lax.broadcasted_iota(jnp.int32, (CHUNK, CHUNK), 0)
    c64 = lax.broadcasted_iota(jnp.int32, (CHUNK, CHUNK), 1)
    a_cs = rows_times(jnp.where(c64 <= r64, 1.0, 0.0), a_e)
    rl = lax.broadcasted_iota(jnp.int32, (CHUNK, SSM_INNER), 0)
    cl = lax.broadcasted_iota(jnp.int32, (CHUNK, SSM_INNER), 1) % SSM_HEADDIM
    diag = jnp.where(rl == cl, a_cs, 0.0)
    a_row = rows_times(jnp.ones((CHUNK, CHUNK), F32), diag)
    decay = jnp.exp(jnp.where(rl >= cl, a_cs - a_row, NEG_BIG))
    a_last = a_cs[CHUNK - 1:CHUNK, :]
    xd = xs * dt_e
    xe = (xd * jnp.exp(a_last - a_cs)).astype(BF16)
    xdb = xd.astype(BF16)
    ea = jnp.exp(a_cs)
    cdec = jnp.exp(a_last)

    br = lax.broadcasted_iota(jnp.int32, (SSM_GW, SSM_GW), 0) // SSM_HEADDIM
    bc_ = lax.broadcasted_iota(jnp.int32, (SSM_GW, SSM_GW), 1) // SSM_HEADDIM
    blockdiag = br == bc_
    nt = (((1,), (1,)), ((), ()))
    tn = (((0,), (0,)), ((), ()))
    ys = []
    for g in range(SSM_GROUPS):
        ns = slice(g * SSM_STATE, (g + 1) * SSM_STATE)
        ls = slice(g * SSM_GW, (g + 1) * SSM_GW)
        cg, bg = cm[:, ns], bm[:, ns]
        b_t = jnp.concatenate([bg] * SSM_HG, axis=0)
        cb = lax.dot_general(cg, b_t, nt, preferred_element_type=F32)
        mg = (cb * decay[:, ls]).astype(BF16)
        xg = xdb[:, ls]
        bd = jnp.where(blockdiag, jnp.concatenate([xg] * SSM_HG, axis=0), jnp.zeros((), BF16))
        y_diag = jnp.dot(mg, bd, preferred_element_type=F32)
        st = st_ref[g]
        y_off = jnp.dot(cg, st.astype(BF16), preferred_element_type=F32) * ea[:, ls]
        upd = lax.dot_general(bg, xe[:, ls], tn, preferred_element_type=F32)
        st_ref[g] = st * cdec[:, ls] + upd
        ys.append(y_diag + y_off)
    y = jnp.concatenate(ys, axis=1) + xs * dexp_ref[...]
    y = y * _silu(z_ref[...].astype(F32))
    outs = []
    for g in range(SSM_GROUPS):
        yg = y[:, g * SSM_GW:(g + 1) * SSM_GW]
        ms = jnp.mean(yg * yg, axis=-1, keepdims=True)
        outs.append(yg * lax.rsqrt(ms + EPS))
    o_ref[...] = (jnp.concatenate(outs, axis=1) * ng_ref[...]).astype(o_ref.dtype)


def _head_expand_matrix():
    r = jnp.arange(LANE)[:, None]
    c = jnp.arange(SSM_INNER)[None, :]
    return ((r % SSM_HEADS == c // SSM_HEADDIM) & (r < 3 * SSM_HEADS)).astype(BF16)


def _ssd_mixer(seg_a, seg_s, conv_w, conv_b, dtb_pad, a_exp, d_exp, norm_g, e_mat, l, bsz, seq):
    nc = seq // CHUNK
    zb, xb, bb = C_Z // SSM_INNER, C_XBC // SSM_INNER, C_XBC // SSM_INNER + 1

    def row(b, c):
        return b * nc + c

    def prow(b, c):
        return b * nc + jnp.maximum(c - 1, 0)

    vec = lambda blk: pl.BlockSpec((None, 1, SSM_INNER), lambda b, c: (l, 0, blk))
    return pl.pallas_call(
        _ssd_kernel,
        grid=(bsz, nc),
        in_specs=[pl.BlockSpec((CHUNK, SSM_INNER), lambda b, c: (row(b, c), zb)),
                  pl.BlockSpec((CHUNK, SSM_INNER), lambda b, c: (row(b, c), xb)),
                  pl.BlockSpec((CHUNK, SSM_INNER), lambda b, c: (prow(b, c), xb)),
                  pl.BlockSpec((CHUNK, SSM_INNER), lambda b, c: (row(b, c), bb)),
                  pl.BlockSpec((CHUNK, SSM_INNER), lambda b, c: (prow(b, c), bb)),
                  pl.BlockSpec((CHUNK, LANE), lambda b, c: (row(b, c), 1)),
                  pl.BlockSpec((None, SSM_CONV, SSM_INNER), lambda b, c: (l, 0, 0)),
                  pl.BlockSpec((None, SSM_CONV, SSM_INNER), lambda b, c: (l, 0, 1)),
                  vec(0), vec(1),
                  pl.BlockSpec((None, 1, LANE), lambda b, c: (l, 0, 0)),
                  vec(0), vec(0), vec(0),
                  pl.BlockSpec((LANE, SSM_INNER), lambda b, c: (0, 0))],
        out_specs=pl.BlockSpec((CHUNK, SSM_INNER), lambda b, c: (row(b, c), 0)),
        out_shape=jax.ShapeDtypeStruct((bsz * seq, SSM_INNER), BF16),
        scratch_shapes=[pltpu.VMEM((SSM_GROUPS, SSM_STATE, SSM_GW), F32)],
        compiler_params=_cparams(("parallel", "arbitrary")),
        name="ssd_mixer",
    )(seg_a, seg_a, seg_a, seg_a, seg_a, seg_s, conv_w, conv_w,
      conv_b.reshape(-1, 1, 2 * SSM_INNER), conv_b.reshape(-1, 1, 2 * SSM_INNER),
      dtb_pad, a_exp, d_exp, norm_g.reshape(-1, 1, SSM_INNER), e_mat)


def _rope128(x, cos, sin_signed):
    return x * cos + pltpu.roll(x, ATT_HEADDIM // 2, 1) * sin_signed


def _rope64(x, cos, sin_signed):
    lane = lax.broadcasted_iota(jnp.int32, x.shape, 1)
    low = (lane % IDX_HEADDIM) < IDX_HEADDIM // 2
    rot = jnp.where(low, pltpu.roll(x, LANE - IDX_HEADDIM // 2, 1), pltpu.roll(x, IDX_HEADDIM // 2, 1))
    return x * cos + rot * sin_signed


def _dsa_kernel(q_ref, k_ref, v_ref, qi_ref, ki_ref, wi_ref,
                cq_ref, sq_ref, ck_ref, sk_ref, ciq_ref, siq_ref, cik_ref, sik_ref, *rest,
                klen, q0, topk):
    o_ref, kr_ref, vb_ref, kir_ref, key_ref, bias_ref, pos_ref = rest[-7:]
    i = pl.program_id(1)
    nt = (((1,), (1,)), ((), ()))
    nlc = klen // LANE

    @pl.when(i == 0)
    def _():
        for kv in range(ATT_KV_HEADS):
            hs = slice(kv * ATT_HEADDIM, (kv + 1) * ATT_HEADDIM)
            kr_ref[:, hs] = _rope128(k_ref[:, hs], ck_ref[...], sk_ref[...]).astype(BF16)
        vb_ref[...] = v_ref[...].astype(BF16)
        kx = _rope64(ki_ref[...], cik_ref[...], sik_ref[...])
        kx_hi = kx.astype(BF16)
        kir_ref[:, 0:LANE] = kx_hi
        kir_ref[:, LANE:2 * LANE] = (kx - kx_hi.astype(F32)).astype(BF16)

    lane_q = lax.broadcasted_iota(jnp.int32, (Q_BLOCK, LANE), 1)
    kir = kir_ref[...]
    wi = wi_ref[...] * np.float32(IDX_HEADS ** -0.5)
    iscore = jnp.zeros((Q_BLOCK, klen), F32)
    for quad in range(IDX_HEADS // 4):
        parts = []
        for pair in range(2 * quad, 2 * quad + 2):
            ps = slice(pair * LANE, (pair + 1) * LANE)
            qp = _rope64(qi_ref[:, ps], ciq_ref[...], siq_ref[...])
            q_hi = qp.astype(BF16).astype(F32)
            q_lo_swapped = pltpu.roll(qp - q_hi, IDX_HEADDIM, 1)
            for sub in range(2):
                own = (lane_q // IDX_HEADDIM) == sub
                parts.append(jnp.concatenate([jnp.where(own, q_hi, q_lo_swapped), jnp.where(own, q_hi, 0.0)],
                                             axis=1).astype(BF16))
        logits = lax.dot_general(jnp.concatenate(parts, axis=0), kir, nt,
                                 preferred_element_type=F32)
        for hh in range(4):
            h = 4 * quad + hh
            wcol = wi[:, S_WI_LANE + h:S_WI_LANE + h + 1]
            iscore = iscore + jnp.maximum(logits[hh * Q_BLOCK:(hh + 1) * Q_BLOCK, :], 0.0) * wcol

    qchunk = (lax.broadcasted_iota(jnp.int32, (Q_BLOCK, klen), 0) + (q0 + i) * Q_BLOCK) // CHUNK
    kchunk = lax.broadcasted_iota(jnp.int32, (Q_BLOCK, klen), 1) // CHUNK
    iscore = jnp.where(iscore == 0.0, 0.0, iscore)
    bits = pltpu.bitcast(iscore, jnp.int32)
    key = jnp.where(bits < 0, bits ^ jnp.int32(0x7FFFFFFF), bits)
    key = jnp.maximum(key, jnp.int32(INT_MIN + 1))
    key_ref[...] = jnp.where(kchunk <= qchunk, key, jnp.int32(INT_MIN))

    def row_count(pred, rows=slice(0, Q_BLOCK)):
        nrow = rows.stop - rows.start
        acc = jnp.zeros((nrow, LANE), F32)
        for cidx in range(nlc):
            acc = acc + jnp.where(pred(key_ref[rows, cidx * LANE:(cidx + 1) * LANE], cidx), 1.0, 0.0)
        return jnp.broadcast_to(jnp.sum(acc, axis=-1, keepdims=True), (nrow, LANE))

    halves = (slice(0, Q_BLOCK // 2), slice(Q_BLOCK // 2, Q_BLOCK))

    def thr_step(it, t_us):
        bit = jnp.left_shift(jnp.int32(1), 31 - it)
        out = []
        for rows, t_u in zip(halves, t_us):
            cand_u = t_u | bit
            cand = cand_u ^ jnp.int32(INT_MIN)
            out.append(jnp.where(row_count(lambda kc, _: kc >= cand, rows) >= topk, cand_u, t_u))
        return tuple(out)

    t_us = lax.fori_loop(0, 32, thr_step, tuple(jnp.zeros((Q_BLOCK // 2, LANE), jnp.int32) for _ in halves),
                         unroll=8)
    thr = jnp.concatenate(t_us, axis=0) ^ jnp.int32(INT_MIN)
    cnt_ge = row_count(lambda kc, _: kc >= thr)
    cnt_gt = row_count(lambda kc, _: kc > thr)
    need = topk - cnt_gt
    excess = jnp.where(thr > INT_MIN, cnt_ge - topk, 0.0)
    pos_ref[...] = jnp.full((Q_BLOCK, LANE), klen, jnp.int32)
    lane_pos = lax.broadcasted_iota(jnp.int32, (Q_BLOCK, LANE), 1)

    @pl.when(jnp.max(excess) > 0.0)
    def _():
        nbits = int(klen - 1).bit_length()
        never = jnp.int32(1 << 30)

        def pos_step(it, bound):
            cand = bound | jnp.left_shift(jnp.int32(1), nbits - 1 - it)
            ties = row_count(lambda kc, cidx: jnp.where(kc == thr, lane_pos + cidx * LANE, never) < cand)
            return jnp.where(ties < need, cand, bound)

        bound = lax.fori_loop(0, nbits, pos_step, jnp.zeros((Q_BLOCK, LANE), jnp.int32))
        pos_ref[...] = bound + 1

    pos = pos_ref[...]
    for cidx in range(nlc):
        cs = slice(cidx * LANE, (cidx + 1) * LANE)
        kc = key_ref[:, cs]
        tie = jnp.where(lane_pos + cidx * LANE < pos, 0.0, NEG_BIG)
        sel = jnp.where(kc > thr, 0.0, jnp.where(kc == thr, tie, NEG_BIG))
        bias_ref[:, cs] = jnp.where(kc == INT_MIN, NEG_BIG, sel)

    scale = np.float32(ATT_HEADDIM ** -0.5)
    for kv in range(ATT_KV_HEADS):
        hs = slice(kv * ATT_HEADDIM, (kv + 1) * ATT_HEADDIM)
        krh = kr_ref[:, hs]
        vh = vb_ref[:, hs]
        heads = [slice((kv * ATT_GRP + gq) * ATT_HEADDIM, (kv * ATT_GRP + gq + 1) * ATT_HEADDIM)
                 for gq in range(ATT_GRP)]
        qg = jnp.concatenate([(_rope128(q_ref[:, qs], cq_ref[...], sq_ref[...]) * scale).astype(BF16)
                              for qs in heads], axis=0)
        s = lax.dot_general(qg, krh, nt, preferred_element_type=F32)
        es, dens = [], []
        for gq in range(ATT_GRP):
            sg = s[gq * Q_BLOCK:(gq + 1) * Q_BLOCK, :] + bias_ref[...]
            e = jnp.exp(sg - jnp.max(sg, axis=-1, keepdims=True))
            dens.append(jnp.sum(e, axis=-1, keepdims=True))
            es.append(e.astype(BF16))
        o = jnp.dot(jnp.concatenate(es, axis=0), vh, preferred_element_type=F32)
        for gq, qs in enumerate(heads):
            o_ref[:, qs] = (o[gq * Q_BLOCK:(gq + 1) * Q_BLOCK, :] / dens[gq]).astype(o_ref.dtype)


DSA_BUCKETS = 8


def _dsa_mixer(seg_q, seg_s, tabs, bsz, seq):
    nqb = seq // Q_BLOCK
    topk = min(IDX_TOPK, seq // 4)
    cos128, sin128, cos64, sin64 = tabs
    kb, vb_, qib = 1024 // 256, 1280 // 256, 1536 // 512
    nbk = DSA_BUCKETS if nqb % DSA_BUCKETS == 0 else 1
    qpb = nqb // nbk
    width = ATT_HEADS * ATT_HEADDIM
    kvw = ATT_KV_HEADS * ATT_HEADDIM
    out = None
    for u in range(nbk):
        q0 = u * qpb
        klen = (u + 1) * qpb * Q_BLOCK
        qtab = pl.BlockSpec((Q_BLOCK, LANE), lambda b, i, q0=q0: (q0 + i, 0))
        ktab = pl.BlockSpec((klen, LANE), lambda b, i: (0, 0))
        qrow = lambda blk, q0=q0: (lambda b, i: (b, q0 + i, blk))
        in_specs = [pl.BlockSpec((None, Q_BLOCK, width), qrow(0)),
                    pl.BlockSpec((None, klen, kvw), lambda b, i: (b, 0, kb)),
                    pl.BlockSpec((None, klen, kvw), lambda b, i: (b, 0, vb_)),
                    pl.BlockSpec((None, Q_BLOCK, IDX_HEADS * IDX_HEADDIM), qrow(qib)),
                    pl.BlockSpec((None, klen, LANE), lambda b, i: (b, 0, 0)),
                    pl.BlockSpec((None, Q_BLOCK, LANE), qrow(1)),
                    qtab, qtab, ktab, ktab, qtab, qtab, ktab, ktab]
        args = [seg_q, seg_q, seg_q, seg_q, seg_s, seg_s,
                cos128, sin128, cos128, sin128, cos64, sin64, cos64, sin64]
        aliases = {}
        if out is not None:
            in_specs.append(pl.BlockSpec(memory_space=pl.ANY))
            args.append(out)
            aliases = {len(args) - 1: 0}
        out = pl.pallas_call(
            functools.partial(_dsa_kernel, klen=klen, q0=q0, topk=topk),
            grid=(bsz, qpb),
            in_specs=in_specs,
            out_specs=pl.BlockSpec((None, Q_BLOCK, width), qrow(0)),
            out_shape=jax.ShapeDtypeStruct((bsz, seq, width), BF16),
            scratch_shapes=[pltpu.VMEM((klen, kvw), BF16),
                            pltpu.VMEM((klen, kvw), BF16),
                            pltpu.VMEM((klen, 2 * LANE), BF16),
                            pltpu.VMEM((Q_BLOCK, klen), jnp.int32),
                            pltpu.VMEM((Q_BLOCK, klen), F32),
                            pltpu.VMEM((Q_BLOCK, LANE), jnp.int32)],
            input_output_aliases=aliases,
            compiler_params=_cparams(("parallel", "arbitrary")),
            name="dsa_mixer",
        )(*args)
    return out


def _rope_tables(seq):
    pos = jnp.arange(seq, dtype=F32)[:, None]

    def tab(half, reps):
        inv = ROPE_THETA ** (-jnp.arange(half, dtype=F32) / half)
        ang = pos * inv[None, :]
        cos, sin = jnp.cos(ang), jnp.sin(ang)
        return (jnp.tile(jnp.concatenate([cos, cos], axis=1), (1, reps)),
                jnp.tile(jnp.concatenate([-sin, sin], axis=1), (1, reps)))

    cos128, sin128 = tab(ATT_HEADDIM // 2, 1)
    cos64, sin64 = tab(IDX_HEADDIM // 2, 2)
    return cos128, sin128, cos64, sin64


MERGE_TILE = 256


def _merge_kernel(h_ref, wg_ref, ya_ref, yb_ref, yc_ref, yd_ref, p_ref, *rest, nside):
    h = h_ref[...]
    d = h.shape[1]
    acc = None
    row = 0
    for i, y_ref in enumerate((ya_ref, yb_ref, yc_ref, yd_ref)):
        width = y_ref.shape[1]
        gate = _sigmoid(jnp.dot(h, wg_ref[i * d:(i + 1) * d, :], preferred_element_type=F32))
        term = gate * jnp.dot(y_ref[...], p_ref[row:row + width, :], preferred_element_type=F32)
        acc = term if acc is None else acc + term
        row += width
    rest[nside][...] = acc.astype(BF16)
    for k in range(nside):
        _cast_tiles(rest[k], rest[nside + 1 + k])


def _gated_merge(h, ys, w_gate_t, w_branch, l, *, tm=1024, sides=()):
    m, d = h.shape
    tm, tn = min(tm, m), MERGE_TILE
    gm, gn = m // tm, d // tn
    s_in, s_out, s_shapes, s_args = _side_cast_specs(sides, gm, gn)
    resident = lambda width: pl.BlockSpec((tm, width), lambda i, j: (i, 0), pipeline_mode=pl.Buffered(1))
    return pl.pallas_call(
        functools.partial(_merge_kernel, nside=len(sides)),
        grid=(gm, gn),
        in_specs=[resident(d),
                  pl.BlockSpec((None, len(ys) * d, tn), lambda i, j: (j, 0, 0))]
                 + [resident(y.shape[1]) for y in ys]
                 + [pl.BlockSpec((None, w_branch.shape[1], tn), lambda i, j: (l, 0, j))] + s_in,
        out_specs=[pl.BlockSpec((tm, tn), lambda i, j: (i, j))] + s_out,
        out_shape=[jax.ShapeDtypeStruct((m, d), BF16)] + s_shapes,
        compiler_params=_cparams(("parallel", "arbitrary")),
        name="gated_merge",
    )(h, w_gate_t, *ys, w_branch, *s_args)


def _xattn_kernel(q_ref, kv_ref, o_ref):
    nt = (((1,), (1,)), ((), ()))
    scale = np.float32(MEM_HEADDIM ** -0.5)
    hw = MEM_HEADS * MEM_HEADDIM
    for h in range(MEM_HEADS):
        hs = slice(h * MEM_HEADDIM, (h + 1) * MEM_HEADDIM)
        s = lax.dot_general(q_ref[:, hs], kv_ref[:, hs], nt, preferred_element_type=F32) * scale
        mx = jnp.max(s, axis=-1, keepdims=True)
        e = jnp.exp(s - mx)
        den = jnp.sum(e, axis=-1, keepdims=True)
        vs = slice(hw + h * MEM_HEADDIM, hw + (h + 1) * MEM_HEADDIM)
        o = jnp.dot(e.astype(BF16), kv_ref[:, vs], preferred_element_type=F32)
        o_ref[:, hs] = (o / den).astype(o_ref.dtype)


def _xattn(q, kv, bsz, seq, mem_len):
    tq = min(512, seq)
    nq = seq // tq
    hw = MEM_HEADS * MEM_HEADDIM
    return pl.pallas_call(
        _xattn_kernel,
        grid=(bsz, nq),
        in_specs=[pl.BlockSpec((tq, hw), lambda b, i: (b * nq + i, 0)),
                  pl.BlockSpec((mem_len, 2 * hw), lambda b, i: (b, 0))],
        out_specs=pl.BlockSpec((tq, hw), lambda b, i: (b * nq + i, 0)),
        out_shape=jax.ShapeDtypeStruct((bsz * seq, hw), BF16),
        compiler_params=_cparams(("parallel", "arbitrary")),
        name="mem_xattn",
    )(q, kv)


def kernel(x, mem, g_ffn1, w_ffn1_in, w_ffn1_out, g_mix, w_in, pool_w, pool_scale, sg_ln_g, sg_ln_b, sg_w, sg_b, ssm_conv_w, ssm_conv_b, ssm_a_log, ssm_dt_bias, ssm_d, ssm_norm_g, w_branch, w_gate, w_out, g_mem, g_cross, w_mem_q, w_mem_kv, w_mem_o, g_ffn2, w_ffn2_in, w_ffn2_out, g_final):
    bsz, seq, d = x.shape
    mem_len = mem.shape[1]
    depth = w_in.shape[0]
    m = bsz * seq
    bf = lambda a: a.astype(BF16)

    w_in_t = jnp.swapaxes(w_in, 1, 2)
    w_q = w_in_t[:, C_Q:C_KI]
    w_ki = w_in_t[:, C_KI:C_WI]
    w_s = jnp.concatenate(
        [w_ki, w_ki, w_in_t[:, C_DT:C_Q], w_in_t[:, C_WI:],
         jnp.zeros((depth, LANE - SSM_HEADS - IDX_HEADS, d), F32)], axis=1)
    w_gate_rows = w_gate.reshape(depth, -1, d)
    wb = bf(w_branch)
    pw = bf(pool_w)

    expand = lambda v: jnp.repeat(v, SSM_HEADDIM, axis=-1).reshape(depth, 1, SSM_INNER)
    a_exp = expand(-jnp.exp(ssm_a_log))
    d_exp = expand(ssm_d)
    dtb_pad = jnp.pad(ssm_dt_bias, ((0, 0), (0, LANE - SSM_HEADS))).reshape(depth, 1, LANE)
    e_mat = _head_expand_matrix()
    bias_tile = jnp.repeat(jnp.swapaxes(sg_b, 1, 2), SG_GW, axis=2)
    tabs = _rope_tables(seq)

    x2 = x.reshape(m, d)
    mem_n = _rmsnorm(mem.reshape(bsz * mem_len, d), g_mem, BF16)

    w1_in, l1_in = w_ffn1_in, 0
    for l in range(depth):
        x2, wg_t, wo_t = _ffn(x2, _rmsnorm(x2, g_ffn1[l], BF16), w1_in, l1_in, w_ffn1_out, l,
                              sides=[(w_gate_rows, l, MERGE_TILE), (w_out, l, 512)])

        h = _rmsnorm(x2, g_mix[l], BF16)
        seg_a = _matmul_nt(h, w_in_t, l, SEG_A, out_dtype=BF16)
        seg_q = _matmul_nt(h, w_q, l, SEG_Q, out_dtype=F32)
        seg_s = _matmul_nt(h, w_s, l, SEG_S, tn=256, out_dtype=F32)
        y_a = _pool_mixer(seg_a, pw, pool_scale, l, bsz, seq)
        y_b = _sg_mixer(seg_a, sg_ln_g, sg_ln_b, sg_w, bias_tile, l, m)
        y_c = _ssd_mixer(seg_a, seg_s, ssm_conv_w, ssm_conv_b, dtb_pad, a_exp, d_exp,
                         ssm_norm_g, e_mat, l, bsz, seq)
        y_d = _dsa_mixer(seg_q.reshape(bsz, seq, SEG_Q), seg_s.reshape(bsz, seq, SEG_S), tabs, bsz, seq)
        merged, w2_in_t = _gated_merge(h, (y_a, y_b, y_c, y_d.reshape(m, -1)), wg_t, wb, l,
                                       sides=[(w_ffn2_in, l, FFN_IN_TILE_BF16)])
        x2, = _matmul_res(merged, wo_t, None, x2, 1.0)

        q = _norm_matmul(x2, g_cross[l], w_mem_q, (l,))
        kv = _matmul(mem_n, w_mem_kv, (l,), 2 * MEM_HEADS * MEM_HEADDIM, out_dtype=BF16)
        att = _xattn(q, kv, bsz, seq, mem_len)
        x2, h2 = _matmul_res_norm(att, w_mem_o, (l,), x2, g_ffn2[l])

        nxt = [(w_ffn1_in, l + 1, FFN_IN_TILE_BF16)] if l + 1 < depth else []
        x2, *cast = _ffn(x2, h2, w2_in_t, None, w_ffn2_out, l, sides=nxt)
        if cast:
            w1_in, l1_in = cast[0], None

    return _rmsnorm(x2, g_final, F32).reshape(bsz, seq, d)
```

```python
import functools

import jax
import jax.numpy as jnp
import numpy as np
from jax import lax
from jax.experimental import pallas as pl
from jax.experimental.pallas import tpu as pltpu

F32 = jnp.float32
BF16 = jnp.bfloat16

D_MODEL = 4096
FFN_DIM = 8192
CHUNK = 64
EPS = 1e-6
ROPE_THETA = 10000.0

POOL_WINDOWS = (2, 4, 8, 16)
POOL_GROUPS = 4
POOL_WIDTH = 2048
POOL_GW = POOL_WIDTH // POOL_GROUPS
POOL_PAD = 16

SG_WIDTH = 1024
SG_BLOCK = 128
SG_GROUPS = 4
SG_GW = SG_WIDTH // SG_GROUPS

SSM_HEADS = 16
SSM_HEADDIM = 64
SSM_INNER = SSM_HEADS * SSM_HEADDIM
SSM_GROUPS = 4
SSM_STATE = 128
SSM_CONV = 4
SSM_HG = SSM_HEADS // SSM_GROUPS
SSM_GW = SSM_INNER // SSM_GROUPS

ATT_HEADS = 8
ATT_KV_HEADS = 2
ATT_HEADDIM = 128
ATT_GRP = ATT_HEADS // ATT_KV_HEADS
IDX_HEADS = 8
IDX_HEADDIM = 64
IDX_TOPK = 256
Q_BLOCK = 128

MEM_HEADS = 4
MEM_HEADDIM = 128

C_POOL, C_U, C_V, C_Z, C_XBC = 0, 2048, 3072, 4096, 5120
C_DT, C_Q, C_K, C_VAL, C_QI, C_KI, C_WI = 7168, 7184, 8208, 8464, 8720, 9232, 9296
SEG_A = 7168
SEG_Q = 2048
SEG_S = 256
S_DT_LANE = 0
S_WI_LANE = 16

LANE = 128
VMEM_LIMIT = 56 * 1024 * 1024
NEG_BIG = -1e30
INT_MIN = -2147483648


def _cparams(sem):
    return pltpu.CompilerParams(dimension_semantics=sem, vmem_limit_bytes=VMEM_LIMIT)


def _sigmoid(x):
    return 1.0 / (1.0 + jnp.exp(-x))


def _silu(x):
    return x * _sigmoid(x)


def _gelu(x):
    return 0.5 * x * (1.0 + lax.erf(x * np.float32(1.0 / np.sqrt(2.0))))


def _softplus(x):
    return jnp.maximum(x, 0.0) + jnp.log1p(jnp.exp(-jnp.abs(x)))


def _rmsnorm_kernel(x_ref, g_ref, o_ref):
    x = x_ref[...]
    ms = jnp.mean(x * x, axis=-1, keepdims=True)
    o_ref[...] = (x * lax.rsqrt(ms + EPS) * g_ref[...]).astype(o_ref.dtype)


def _rmsnorm(x, g, out_dtype):
    m, d = x.shape
    tm = min(512, m)
    return pl.pallas_call(
        _rmsnorm_kernel,
        grid=(m // tm,),
        in_specs=[pl.BlockSpec((tm, d), lambda i: (i, 0)),
                  pl.BlockSpec((1, d), lambda i: (0, 0))],
        out_specs=pl.BlockSpec((tm, d), lambda i: (i, 0)),
        out_shape=jax.ShapeDtypeStruct((m, d), out_dtype),
        compiler_params=_cparams(("parallel",)),
        name="rmsnorm",
    )(x, g.reshape(1, d))


def _mm_kernel(x_ref, w_ref, o_ref):
    o_ref[...] = jnp.dot(x_ref[...], w_ref[...].astype(BF16), preferred_element_type=F32).astype(o_ref.dtype)


def _cast_tiles(src_ref, dst_ref):
    tc = dst_ref.shape[2]
    for t in range(dst_ref.shape[0]):
        dst_ref[t] = src_ref[:, t * tc:(t + 1) * tc].astype(BF16)


def _mm_res_kernel(x_ref, w_ref, r_ref, *rest, alpha, nside):
    o_ref = rest[nside]
    acc = jnp.dot(x_ref[...], w_ref[...].astype(BF16), preferred_element_type=F32)
    o_ref[...] = r_ref[...] + alpha * acc
    for k in range(nside):
        _cast_tiles(rest[k], rest[nside + 1 + k])


def _side_cast_specs(sides, gm, gn):
    in_specs, out_specs, out_shapes, args = [], [], [], []
    for src, l, tc in sides:
        _, rows, cols = src.shape
        rps = rows // (gm * gn)
        assert rps * gm * gn == rows and rps % 16 == 0 and cols % tc == 0, (src.shape, gm, gn)
        in_specs.append(pl.BlockSpec((None, rps, cols), lambda i, j, l=l: (l, i * gn + j, 0)))
        out_specs.append(pl.BlockSpec((cols // tc, rps, tc), lambda i, j: (0, i * gn + j, 0)))
        out_shapes.append(jax.ShapeDtypeStruct((cols // tc, rows, tc), BF16))
        args.append(src)
    return in_specs, out_specs, out_shapes, args


def _w_spec(widx, kdim, tn, col_blk0=0):
    if widx is None:
        return pl.BlockSpec((None, kdim, tn), lambda i, j: (j, 0, 0))
    lead = (None,) * len(widx)
    return pl.BlockSpec(lead + (kdim, tn), lambda i, j: (*widx, 0, j + col_blk0))


def _matmul(x, w, widx, n, *, col0=0, tm=1024, tn=512, out_dtype=BF16):
    m, kdim = x.shape
    tm, tn = min(tm, m), min(tn, n)
    return pl.pallas_call(
        _mm_kernel,
        grid=(m // tm, n // tn),
        in_specs=[pl.BlockSpec((tm, kdim), lambda i, j: (i, 0)),
                  _w_spec(widx, kdim, tn, col0 // tn)],
        out_specs=pl.BlockSpec((tm, tn), lambda i, j: (i, j)),
        out_shape=jax.ShapeDtypeStruct((m, n), out_dtype),
        compiler_params=_cparams(("parallel", "arbitrary")),
        name="matmul",
    )(x, w)


def _mm_nt_kernel(x_ref, wt_ref, o_ref):
    nt = (((1,), (1,)), ((), ()))
    o_ref[...] = lax.dot_general(x_ref[...], wt_ref[...].astype(BF16), nt,
                                 preferred_element_type=F32).astype(o_ref.dtype)


def _matmul_nt(x, wt, l, n, *, tm=1024, tn=512, out_dtype=BF16):
    m, kdim = x.shape
    tm, tn = min(tm, m), min(tn, n)
    return pl.pallas_call(
        _mm_nt_kernel,
        grid=(m // tm, n // tn),
        in_specs=[pl.BlockSpec((tm, kdim), lambda i, j: (i, 0)),
                  pl.BlockSpec((None, tn, kdim), lambda i, j: (l, j, 0))],
        out_specs=pl.BlockSpec((tm, tn), lambda i, j: (i, j)),
        out_shape=jax.ShapeDtypeStruct((m, n), out_dtype),
        compiler_params=_cparams(("parallel", "arbitrary")),
        name="matmul_nt",
    )(x, wt)


def _x_spec(tm, kdim):
    return pl.BlockSpec((tm, kdim), lambda i, j: (i, 0))


def _matmul_res(x, w, widx, res, alpha, *, tm=1024, tn=512, sides=()):
    m, kdim = x.shape
    n = res.shape[1]
    tm, tn = min(tm, m), min(tn, n)
    gm, gn = m // tm, n // tn
    s_in, s_out, s_shapes, s_args = _side_cast_specs(sides, gm, gn)
    tile = pl.BlockSpec((tm, tn), lambda i, j: (i, j))
    return pl.pallas_call(
        functools.partial(_mm_res_kernel, alpha=alpha, nside=len(sides)),
        grid=(gm, gn),
        in_specs=[_x_spec(tm, kdim), _w_spec(widx, kdim, tn), tile] + s_in,
        out_specs=[tile] + s_out,
        out_shape=[jax.ShapeDtypeStruct((m, n), F32)] + s_shapes,
        compiler_params=_cparams(("parallel", "arbitrary")),
        name="matmul_res",
    )(x, w, res, *s_args)


def _swiglu_kernel(x_ref, wg_ref, wu_ref, *rest, nside):
    o_ref, w_ref = rest[nside], rest[-1]
    tn = wg_ref.shape[1]
    w_ref[:, 0:tn] = wg_ref[...].astype(BF16)
    w_ref[:, tn:2 * tn] = wu_ref[...].astype(BF16)
    gu = jnp.dot(x_ref[...], w_ref[...], preferred_element_type=F32)
    o_ref[...] = (_silu(gu[:, 0:tn]) * gu[:, tn:2 * tn]).astype(o_ref.dtype)
    for k in range(nside):
        _cast_tiles(rest[k], rest[nside + 1 + k])


def _swiglu_bf16_kernel(x_ref, wg_ref, wu_ref, *rest, nside):
    x = x_ref[...]
    g = jnp.dot(x, wg_ref[...], preferred_element_type=F32)
    u = jnp.dot(x, wu_ref[...], preferred_element_type=F32)
    rest[nside][...] = (_silu(g) * u).astype(BF16)
    for k in range(nside):
        _cast_tiles(rest[k], rest[nside + 1 + k])


FFN_IN_TILE = 256
FFN_IN_TILE_BF16 = 512
FFN_OUT_TILE = 256


def _swiglu_in(x, w, l, *, tm=1024, sides=()):
    m, kdim = x.shape
    tm = min(tm, m)
    if l is None:
        tn = w.shape[2]
        f = w.shape[0] * tn // 2
        nb = f // tn
        w_specs = [pl.BlockSpec((None, kdim, tn), lambda i, j: (j, 0, 0)),
                   pl.BlockSpec((None, kdim, tn), lambda i, j: (j + nb, 0, 0))]
        body, scratch = _swiglu_bf16_kernel, []
    else:
        tn = FFN_IN_TILE
        f = w.shape[-1] // 2
        nb = f // tn
        w_specs = [pl.BlockSpec((None, kdim, tn), lambda i, j: (l, 0, j)),
                   pl.BlockSpec((None, kdim, tn), lambda i, j: (l, 0, j + nb))]
        body, scratch = _swiglu_kernel, [pltpu.VMEM((kdim, 2 * tn), BF16)]
    gm = m // tm
    s_in, s_out, s_shapes, s_args = _side_cast_specs(sides, gm, nb)
    return pl.pallas_call(
        functools.partial(body, nside=len(sides)),
        grid=(gm, nb),
        in_specs=[_x_spec(tm, kdim)] + w_specs + s_in,
        out_specs=[pl.BlockSpec((tm, tn), lambda i, j: (i, j))] + s_out,
        out_shape=[jax.ShapeDtypeStruct((m, f), BF16)] + s_shapes,
        scratch_shapes=scratch,
        compiler_params=_cparams(("parallel", "arbitrary")),
        name="swiglu_in",
    )(x, w, w, *s_args)


def _ffn(x, h, w_in, l_in, w_out, l, *, sides=()):
    act, w_out_t = _swiglu_in(h, w_in, l_in, sides=[(w_out, l, FFN_OUT_TILE)])
    return _matmul_res(act, w_out_t, None, x, 0.5, tm=1024, tn=FFN_OUT_TILE, sides=sides)


def _pool_kernel(a_ref, w_ref, s_ref, o_ref, pad_ref, *, seq, rows):
    g = pl.program_id(1)
    pad_ref[0:POOL_PAD, :] = jnp.zeros((POOL_PAD, POOL_GW), F32)
    pad_ref[POOL_PAD:POOL_PAD + seq, :] = a_ref[...].astype(F32)
    w = w_ref[...]
    scale = s_ref[...]
    for gi, win in enumerate(POOL_WINDOWS):

        @pl.when(g == gi)
        def _(win=win):
            for c in range(seq // rows):
                r0 = POOL_PAD + c * rows
                cur = pad_ref[r0:r0 + rows, :]
                tot = cur
                for k in range(1, win):
                    tot = tot + pad_ref[r0 - k:r0 - k + rows, :]
                t1 = lax.broadcasted_iota(jnp.int32, (rows, POOL_GW), 0) + (c * rows + 1)
                cnt = jnp.minimum(t1, win).astype(F32)
                mixed = (tot / cnt - cur).astype(BF16)
                y = jnp.dot(mixed, w, preferred_element_type=F32) * scale
                o_ref[c * rows:(c + 1) * rows, :] = y.astype(o_ref.dtype)


def _pool_mixer(seg_a, pool_w, pool_scale, l, bsz, seq):
    rows = min(256, seq)
    return pl.pallas_call(
        functools.partial(_pool_kernel, seq=seq, rows=rows),
        grid=(bsz, POOL_GROUPS),
        in_specs=[pl.BlockSpec((seq, POOL_GW), lambda b, g: (b, g)),
                  pl.BlockSpec((None, None, POOL_GW, POOL_GW), lambda b, g: (l, g, 0, 0)),
                  pl.BlockSpec((None, None, 1, POOL_GW), lambda b, g: (l, g, 0, 0))],
        out_specs=pl.BlockSpec((seq, POOL_GW), lambda b, g: (b, g)),
        out_shape=jax.ShapeDtypeStruct((bsz * seq, POOL_WIDTH), BF16),
        scratch_shapes=[pltpu.VMEM((POOL_PAD + seq, POOL_GW), F32)],
        compiler_params=_cparams(("parallel", "arbitrary")),
        name="pool_mixer",
    )(seg_a, pool_w, pool_scale.reshape(pool_scale.shape[0], POOL_GROUPS, 1, POOL_GW))


def _sg_kernel(u_ref, v_ref, g_ref, b_ref, w_ref, bias_ref, o_ref, *, nblk):
    ri = lax.broadcasted_iota(jnp.int32, (SG_BLOCK, SG_BLOCK), 0) // CHUNK
    ci = lax.broadcasted_iota(jnp.int32, (SG_BLOCK, SG_BLOCK), 1) // CHUNK
    causal = ri >= ci
    wm = [jnp.where(causal, w_ref[gi], 0.0).astype(BF16) for gi in range(SG_GROUPS)]
    bias = bias_ref[...]
    for n in range(nblk):
        rs = slice(n * SG_BLOCK, (n + 1) * SG_BLOCK)
        v = _gelu(v_ref[rs, :].astype(F32))
        mu = jnp.mean(v, axis=-1, keepdims=True)
        vc = v - mu
        var = jnp.mean(vc * vc, axis=-1, keepdims=True)
        vn = (vc * lax.rsqrt(var + EPS) * g_ref[...] + b_ref[...]).astype(BF16)
        u = _gelu(u_ref[rs, :].astype(F32))
        for gi in range(SG_GROUPS):
            cs = slice(gi * SG_GW, (gi + 1) * SG_GW)
            sv = jnp.dot(wm[gi], vn[:, cs], preferred_element_type=F32) + bias[:, cs]
            o_ref[rs, cs] = (u[:, cs] * sv).astype(o_ref.dtype)


def _sg_mixer(seg_a, ln_g, ln_b, sg_w, bias_tile, l, m):
    tb = min(512, m)
    ub, vb = C_U // SG_WIDTH, C_V // SG_WIDTH
    return pl.pallas_call(
        functools.partial(_sg_kernel, nblk=tb // SG_BLOCK),
        grid=(m // tb,),
        in_specs=[pl.BlockSpec((tb, SG_WIDTH), lambda i: (i, ub)),
                  pl.BlockSpec((tb, SG_WIDTH), lambda i: (i, vb)),
                  pl.BlockSpec((None, 1, SG_WIDTH), lambda i: (l, 0, 0)),
                  pl.BlockSpec((None, 1, SG_WIDTH), lambda i: (l, 0, 0)),
                  pl.BlockSpec((None, SG_GROUPS, SG_BLOCK, SG_BLOCK), lambda i: (l, 0, 0, 0)),
                  pl.BlockSpec((None, SG_BLOCK, SG_WIDTH), lambda i: (l, 0, 0))],
        out_specs=pl.BlockSpec((tb, SG_WIDTH), lambda i: (i, 0)),
        out_shape=jax.ShapeDtypeStruct((m, SG_WIDTH), BF16),
        compiler_params=_cparams(("parallel",)),
        name="sg_mixer",
    )(seg_a, seg_a, ln_g.reshape(-1, 1, SG_WIDTH), ln_b.reshape(-1, 1, SG_WIDTH), sg_w, bias_tile)


def _ssd_kernel(z_ref, xc_ref, xp_ref, bc_ref, bp_ref, dt_ref,
                cwx_ref, cwb_ref, cbx_ref, cbb_ref, dtb_ref, aexp_ref, dexp_ref, ng_ref, e_ref,
                o_ref, st_ref):
    c = pl.program_id(1)

    @pl.when(c == 0)
    def _():
        st_ref[...] = jnp.zeros(st_ref.shape, F32)

    has_prev = c > 0
    srow = lax.broadcasted_iota(jnp.int32, (3 * CHUNK, 2 * CHUNK), 0)
    scol = lax.broadcasted_iota(jnp.int32, (3 * CHUNK, 2 * CHUNK), 1)
    shift = jnp.where(scol == CHUNK + (srow % CHUNK) - (3 - srow // CHUNK), 1.0, 0.0).astype(BF16)

    def conv(cur_ref, prev_ref, w_ref, b_ref):
        cur = cur_ref[...]
        prev = jnp.where(has_prev, prev_ref[...], jnp.zeros_like(cur))
        both = jnp.concatenate([prev, cur], axis=0)
        sh = jnp.dot(shift, both, preferred_element_type=F32)
        w = w_ref[...]
        acc = cur.astype(F32) * w[3:4, :] + b_ref[...]
        for k in range(SSM_CONV - 1):
            acc = acc + sh[k * CHUNK:(k + 1) * CHUNK, :] * w[k:k + 1, :]
        return _silu(acc)

    xs = conv(xc_ref, xp_ref, cwx_ref, cbx_ref)
    bcv = conv(bc_ref, bp_ref, cwb_ref, cbb_ref)
    gn = SSM_GROUPS * SSM_STATE
    bm = bcv[:, :gn].astype(BF16)
    cm = bcv[:, gn:].astype(BF16)

    def split3(v):
        p1 = v.astype(BF16)
        r1 = v - p1.astype(F32)
        p2 = r1.astype(BF16)
        return p1, p2, (r1 - p2.astype(F32)).astype(BF16)

    def rows_times(m01, v):
        m3 = jnp.concatenate([m01.astype(BF16)] * 3, axis=1)
        return jnp.dot(m3, jnp.concatenate(split3(v), axis=0), preferred_element_type=F32)

    lane = lax.broadcasted_iota(jnp.int32, (CHUNK, LANE), 1)
    dt = jnp.where(lane < SSM_HEADS, _softplus(dt_ref[...] + dtb_ref[...]), 0.0)
    d1, d2, d3 = split3(dt)
    dt3 = (d1.astype(F32) + pltpu.roll(d2.astype(F32), SSM_HEADS, 1)
           + pltpu.roll(d3.astype(F32), 2 * SSM_HEADS, 1)).astype(BF16)
    dt_e = jnp.dot(dt3, e_ref[...], preferred_element_type=F32)
    a_e = dt_e * aexp_ref[...]
    r64 = lax.broadcasted_iota(jnp.int32, (CHUNK, CHUNK), 0)
    c64 = lax.broadcasted_iota(jnp.int32, (CHUNK, CHUNK), 1)
    a_cs = rows_times(jnp.where(c64 <= r64, 1.0, 0.0), a_e)
    rl = lax.broadcasted_iota(jnp.int32, (CHUNK, SSM_INNER), 0)
    cl = lax.broadcasted_iota(jnp.int32, (CHUNK, SSM_INNER), 1) % SSM_HEADDIM
    diag = jnp.where(rl == cl, a_cs, 0.0)
    a_row = rows_times(jnp.ones((CHUNK, CHUNK), F32), diag)
    decay = jnp.exp(jnp.where(rl >= cl, a_cs - a_row, NEG_BIG))
    a_last = a_cs[CHUNK - 1:CHUNK, :]
    xd = xs * dt_e
    xe = (xd * jnp.exp(a_last - a_cs)).astype(BF16)
    xdb = xd.astype(BF16)
    ea = jnp.exp(a_cs)
    cdec = jnp.exp(a_last)

    br = lax.broadcasted_iota(jnp.int32, (SSM_GW, SSM_GW), 0) // SSM_HEADDIM
    bc_ = lax.broadcasted_iota(jnp.int32, (SSM_GW, SSM_GW), 1) // SSM_HEADDIM
    blockdiag = br == bc_
    nt = (((1,), (1,)), ((), ()))
    tn = (((0,), (0,)), ((), ()))
    ys = []
    for g in range(SSM_GROUPS):
        ns = slice(g * SSM_STATE, (g + 1) * SSM_STATE)
        ls = slice(g * SSM_GW, (g + 1) * SSM_GW)
        cg, bg = cm[:, ns], bm[:, ns]
        b_t = jnp.concatenate([bg] * SSM_HG, axis=0)
        cb = lax.dot_general(cg, b_t, nt, preferred_element_type=F32)
        mg = (cb * decay[:, ls]).astype(BF16)
        xg = xdb[:, ls]
        bd = jnp.where(blockdiag, jnp.concatenate([xg] * SSM_HG, axis=0), jnp.zeros((), BF16))
        y_diag = jnp.dot(mg, bd, preferred_element_type=F32)
        st = st_ref[g]
        y_off = jnp.dot(cg, st.astype(BF16), preferred_element_type=F32) * ea[:, ls]
        upd = lax.dot_general(bg, xe[:, ls], tn, preferred_element_type=F32)
        st_ref[g] = st * cdec[:, ls] + upd
        ys.append(y_diag + y_off)
    y = jnp.concatenate(ys, axis=1) + xs * dexp_ref[...]
    y = y * _silu(z_ref[...].astype(F32))
    outs = []
    for g in range(SSM_GROUPS):
        yg = y[:, g * SSM_GW:(g + 1) * SSM_GW]
        ms = jnp.mean(yg * yg, axis=-1, keepdims=True)
        outs.append(yg * lax.rsqrt(ms + EPS))
    o_ref[...] = (jnp.concatenate(outs, axis=1) * ng_ref[...]).astype(o_ref.dtype)


def _head_expand_matrix():
    r = jnp.arange(LANE)[:, None]
    c = jnp.arange(SSM_INNER)[None, :]
    return ((r % SSM_HEADS == c // SSM_HEADDIM) & (r < 3 * SSM_HEADS)).astype(BF16)


def _ssd_mixer(seg_a, seg_s, conv_w, conv_b, dtb_pad, a_exp, d_exp, norm_g, e_mat, l, bsz, seq):
    nc = seq // CHUNK
    zb, xb, bb = C_Z // SSM_INNER, C_XBC // SSM_INNER, C_XBC // SSM_INNER + 1

    def row(b, c):
        return b * nc + c

    def prow(b, c):
        return b * nc + jnp.maximum(c - 1, 0)

    vec = lambda blk: pl.BlockSpec((None, 1, SSM_INNER), lambda b, c: (l, 0, blk))
    return pl.pallas_call(
        _ssd_kernel,
        grid=(bsz, nc),
        in_specs=[pl.BlockSpec((CHUNK, SSM_INNER), lambda b, c: (row(b, c), zb)),
                  pl.BlockSpec((CHUNK, SSM_INNER), lambda b, c: (row(b, c), xb)),
                  pl.BlockSpec((CHUNK, SSM_INNER), lambda b, c: (prow(b, c), xb)),
                  pl.BlockSpec((CHUNK, SSM_INNER), lambda b, c: (row(b, c), bb)),
                  pl.BlockSpec((CHUNK, SSM_INNER), lambda b, c: (prow(b, c), bb)),
                  pl.BlockSpec((CHUNK, LANE), lambda b, c: (row(b, c), 1)),
                  pl.BlockSpec((None, SSM_CONV, SSM_INNER), lambda b, c: (l, 0, 0)),
                  pl.BlockSpec((None, SSM_CONV, SSM_INNER), lambda b, c: (l, 0, 1)),
                  vec(0), vec(1),
                  pl.BlockSpec((None, 1, LANE), lambda b, c: (l, 0, 0)),
                  vec(0), vec(0), vec(0),
                  pl.BlockSpec((LANE, SSM_INNER), lambda b, c: (0, 0))],
        out_specs=pl.BlockSpec((CHUNK, SSM_INNER), lambda b, c: (row(b, c), 0)),
        out_shape=jax.ShapeDtypeStruct((bsz * seq, SSM_INNER), BF16),
        scratch_shapes=[pltpu.VMEM((SSM_GROUPS, SSM_STATE, SSM_GW), F32)],
        compiler_params=_cparams(("parallel", "arbitrary")),
        name="ssd_mixer",
    )(seg_a, seg_a, seg_a, seg_a, seg_a, seg_s, conv_w, conv_w,
      conv_b.reshape(-1, 1, 2 * SSM_INNER), conv_b.reshape(-1, 1, 2 * SSM_INNER),
      dtb_pad, a_exp, d_exp, norm_g.reshape(-1, 1, SSM_INNER), e_mat)


def _rope128(x, cos, sin_signed):
    return x * cos + pltpu.roll(x, ATT_HEADDIM // 2, 1) * sin_signed


def _rope64(x, cos, sin_signed):
    lane = lax.broadcasted_iota(jnp.int32, x.shape, 1)
    low = (lane % IDX_HEADDIM) < IDX_HEADDIM // 2
    rot = jnp.where(low, pltpu.roll(x, LANE - IDX_HEADDIM // 2, 1), pltpu.roll(x, IDX_HEADDIM // 2, 1))
    return x * cos + rot * sin_signed


def _dsa_kernel(q_ref, k_ref, v_ref, qi_ref, ki_ref, wi_ref,
                cq_ref, sq_ref, ck_ref, sk_ref, ciq_ref, siq_ref, cik_ref, sik_ref, *rest,
                klen, q0, topk):
    o_ref, kr_ref, vb_ref, kir_ref, key_ref, bias_ref, pos_ref = rest[-7:]
    i = pl.program_id(1)
    nt = (((1,), (1,)), ((), ()))
    nlc = klen // LANE

    @pl.when(i == 0)
    def _():
        for kv in range(ATT_KV_HEADS):
            hs = slice(kv * ATT_HEADDIM, (kv + 1) * ATT_HEADDIM)
            kr_ref[:, hs] = _rope128(k_ref[:, hs], ck_ref[...], sk_ref[...]).astype(BF16)
        vb_ref[...] = v_ref[...].astype(BF16)
        kx = _rope64(ki_ref[...], cik_ref[...], sik_ref[...])
        kx_hi = kx.astype(BF16)
        kir_ref[:, 0:LANE] = kx_hi
        kir_ref[:, LANE:2 * LANE] = (kx - kx_hi.astype(F32)).astype(BF16)

    lane_q = lax.broadcasted_iota(jnp.int32, (Q_BLOCK, LANE), 1)
    kir = kir_ref[...]
    wi = wi_ref[...] * np.float32(IDX_HEADS ** -0.5)
    iscore = jnp.zeros((Q_BLOCK, klen), F32)
    for quad in range(IDX_HEADS // 4):
        parts = []
        for pair in range(2 * quad, 2 * quad + 2):
            ps = slice(pair * LANE, (pair + 1) * LANE)
            qp = _rope64(qi_ref[:, ps], ciq_ref[...], siq_ref[...])
            q_hi = qp.astype(BF16).astype(F32)
            q_lo_swapped = pltpu.roll(qp - q_hi, IDX_HEADDIM, 1)
            for sub in range(2):
                own = (lane_q // IDX_HEADDIM) == sub
                parts.append(jnp.concatenate([jnp.where(own, q_hi, q_lo_swapped), jnp.where(own, q_hi, 0.0)],
                                             axis=1).astype(BF16))
        logits = lax.dot_general(jnp.concatenate(parts, axis=0), kir, nt,
                                 preferred_element_type=F32)
        for hh in range(4):
            h = 4 * quad + hh
            wcol = wi[:, S_WI_LANE + h:S_WI_LANE + h + 1]
            iscore = iscore + jnp.maximum(logits[hh * Q_BLOCK:(hh + 1) * Q_BLOCK, :], 0.0) * wcol

    qchunk = (lax.broadcasted_iota(jnp.int32, (Q_BLOCK, klen), 0) + (q0 + i) * Q_BLOCK) // CHUNK
    kchunk = lax.broadcasted_iota(jnp.int32, (Q_BLOCK, klen), 1) // CHUNK
    iscore = jnp.where(iscore == 0.0, 0.0, iscore)
    bits = pltpu.bitcast(iscore, jnp.int32)
    key = jnp.where(bits < 0, bits ^ jnp.int32(0x7FFFFFFF), bits)
    key = jnp.maximum(key, jnp.int32(INT_MIN + 1))
    key_ref[...] = jnp.where(kchunk <= qchunk, key, jnp.int32(INT_MIN))

    def row_count(pred, rows=slice(0, Q_BLOCK)):
        nrow = rows.stop - rows.start
        acc = jnp.zeros((nrow, LANE), F32)
        for cidx in range(nlc):
            acc = acc + jnp.where(pred(key_ref[rows, cidx * LANE:(cidx + 1) * LANE], cidx), 1.0, 0.0)
        return jnp.broadcast_to(jnp.sum(acc, axis=-1, keepdims=True), (nrow, LANE))

    halves = (slice(0, Q_BLOCK // 2), slice(Q_BLOCK // 2, Q_BLOCK))

    def thr_step(it, t_us):
        bit = jnp.left_shift(jnp.int32(1), 31 - it)
        out = []
        for rows, t_u in zip(halves, t_us):
            cand_u = t_u | bit
            cand = cand_u ^ jnp.int32(INT_MIN)
            out.append(jnp.where(row_count(lambda kc, _: kc >= cand, rows) >= topk, cand_u, t_u))
        return tuple(out)

    t_us = lax.fori_loop(0, 32, thr_step, tuple(jnp.zeros((Q_BLOCK // 2, LANE), jnp.int32) for _ in halves),
                         unroll=8)
    thr = jnp.concatenate(t_us, axis=0) ^ jnp.int32(INT_MIN)
    cnt_ge = row_count(lambda kc, _: kc >= thr)
    cnt_gt = row_count(lambda kc, _: kc > thr)
    need = topk - cnt_gt
    excess = jnp.where(thr > INT_MIN, cnt_ge - topk, 0.0)
    pos_ref[...] = jnp.full((Q_BLOCK, LANE), klen, jnp.int32)
    lane_pos = lax.broadcasted_iota(jnp.int32, (Q_BLOCK, LANE), 1)

    @pl.when(jnp.max(excess) > 0.0)
    def _():
        nbits = int(klen - 1).bit_length()
        never = jnp.int32(1 << 30)

        def pos_step(it, bound):
            cand = bound | jnp.left_shift(jnp.int32(1), nbits - 1 - it)
            ties = row_count(lambda kc, cidx: jnp.where(kc == thr, lane_pos + cidx * LANE, never) < cand)
            return jnp.where(ties < need, cand, bound)

        bound = lax.fori_loop(0, nbits, pos_step, jnp.zeros((Q_BLOCK, LANE), jnp.int32))
        pos_ref[...] = bound + 1

    pos = pos_ref[...]
    for cidx in range(nlc):
        cs = slice(cidx * LANE, (cidx + 1) * LANE)
        kc = key_ref[:, cs]
        tie = jnp.where(lane_pos + cidx * LANE < pos, 0.0, NEG_BIG)
        sel = jnp.where(kc > thr, 0.0, jnp.where(kc == thr, tie, NEG_BIG))
        bias_ref[:, cs] = jnp.where(kc == INT_MIN, NEG_BIG, sel)

    scale = np.float32(ATT_HEADDIM ** -0.5)
    for kv in range(ATT_KV_HEADS):
        hs = slice(kv * ATT_HEADDIM, (kv + 1) * ATT_HEADDIM)
        krh = kr_ref[:, hs]
        vh = vb_ref[:, hs]
        heads = [slice((kv * ATT_GRP + gq) * ATT_HEADDIM, (kv * ATT_GRP + gq + 1) * ATT_HEADDIM)
                 for gq in range(ATT_GRP)]
        qg = jnp.concatenate([(_rope128(q_ref[:, qs], cq_ref[...], sq_ref[...]) * scale).astype(BF16)
                              for qs in heads], axis=0)
        s = lax.dot_general(qg, krh, nt, preferred_element_type=F32)
        es, dens = [], []
        for gq in range(ATT_GRP):
            sg = s[gq * Q_BLOCK:(gq + 1) * Q_BLOCK, :] + bias_ref[...]
            e = jnp.exp(sg - jnp.max(sg, axis=-1, keepdims=True))
            dens.append(jnp.sum(e, axis=-1, keepdims=True))
            es.append(e.astype(BF16))
        o = jnp.dot(jnp.concatenate(es, axis=0), vh, preferred_element_type=F32)
        for gq, qs in enumerate(heads):
            o_ref[:, qs] = (o[gq * Q_BLOCK:(gq + 1) * Q_BLOCK, :] / dens[gq]).astype(o_ref.dtype)


DSA_BUCKETS = 8


def _dsa_mixer(seg_q, seg_s, tabs, bsz, seq):
    nqb = seq // Q_BLOCK
    topk = min(IDX_TOPK, seq // 4)
    cos128, sin128, cos64, sin64 = tabs
    kb, vb_, qib = 1024 // 256, 1280 // 256, 1536 // 512
    nbk = DSA_BUCKETS if nqb % DSA_BUCKETS == 0 else 1
    qpb = nqb // nbk
    width = ATT_HEADS * ATT_HEADDIM
    kvw = ATT_KV_HEADS * ATT_HEADDIM
    out = None
    for u in range(nbk):
        q0 = u * qpb
        klen = (u + 1) * qpb * Q_BLOCK
        qtab = pl.BlockSpec((Q_BLOCK, LANE), lambda b, i, q0=q0: (q0 + i, 0))
        ktab = pl.BlockSpec((klen, LANE), lambda b, i: (0, 0))
        qrow = lambda blk, q0=q0: (lambda b, i: (b, q0 + i, blk))
        in_specs = [pl.BlockSpec((None, Q_BLOCK, width), qrow(0)),
                    pl.BlockSpec((None, klen, kvw), lambda b, i: (b, 0, kb)),
                    pl.BlockSpec((None, klen, kvw), lambda b, i: (b, 0, vb_)),
                    pl.BlockSpec((None, Q_BLOCK, IDX_HEADS * IDX_HEADDIM), qrow(qib)),
                    pl.BlockSpec((None, klen, LANE), lambda b, i: (b, 0, 0)),
                    pl.BlockSpec((None, Q_BLOCK, LANE), qrow(1)),
                    qtab, qtab, ktab, ktab, qtab, qtab, ktab, ktab]
        args = [seg_q, seg_q, seg_q, seg_q, seg_s, seg_s,
                cos128, sin128, cos128, sin128, cos64, sin64, cos64, sin64]
        aliases = {}
        if out is not None:
            in_specs.append(pl.BlockSpec(memory_space=pl.ANY))
            args.append(out)
            aliases = {len(args) - 1: 0}
        out = pl.pallas_call(
            functools.partial(_dsa_kernel, klen=klen, q0=q0, topk=topk),
            grid=(bsz, qpb),
            in_specs=in_specs,
            out_specs=pl.BlockSpec((None, Q_BLOCK, width), qrow(0)),
            out_shape=jax.ShapeDtypeStruct((bsz, seq, width), BF16),
            scratch_shapes=[pltpu.VMEM((klen, kvw), BF16),
                            pltpu.VMEM((klen, kvw), BF16),
                            pltpu.VMEM((klen, 2 * LANE), BF16),
                            pltpu.VMEM((Q_BLOCK, klen), jnp.int32),
                            pltpu.VMEM((Q_BLOCK, klen), F32),
                            pltpu.VMEM((Q_BLOCK, LANE), jnp.int32)],
            input_output_aliases=aliases,
            compiler_params=_cparams(("parallel", "arbitrary")),
            name="dsa_mixer",
        )(*args)
    return out


def _rope_tables(seq):
    pos = jnp.arange(seq, dtype=F32)[:, None]

    def tab(half, reps):
        inv = ROPE_THETA ** (-jnp.arange(half, dtype=F32) / half)
        ang = pos * inv[None, :]
        cos, sin = jnp.cos(ang), jnp.sin(ang)
        return (jnp.tile(jnp.concatenate([cos, cos], axis=1), (1, reps)),
                jnp.tile(jnp.concatenate([-sin, sin], axis=1), (1, reps)))

    cos128, sin128 = tab(ATT_HEADDIM // 2, 1)
    cos64, sin64 = tab(IDX_HEADDIM // 2, 2)
    return cos128, sin128, cos64, sin64


MERGE_TILE = 256


def _merge_kernel(h_ref, wg_ref, ya_ref, yb_ref, yc_ref, yd_ref, p_ref, *rest, nside):
    h = h_ref[...]
    d = h.shape[1]
    acc = None
    row = 0
    for i, y_ref in enumerate((ya_ref, yb_ref, yc_ref, yd_ref)):
        width = y_ref.shape[1]
        gate = _sigmoid(jnp.dot(h, wg_ref[i * d:(i + 1) * d, :], preferred_element_type=F32))
        term = gate * jnp.dot(y_ref[...], p_ref[row:row + width, :], preferred_element_type=F32)
        acc = term if acc is None else acc + term
        row += width
    rest[nside][...] = acc.astype(BF16)
    for k in range(nside):
        _cast_tiles(rest[k], rest[nside + 1 + k])


def _gated_merge(h, ys, w_gate_t, w_branch, l, *, tm=1024, sides=()):
    m, d = h.shape
    tm, tn = min(tm, m), MERGE_TILE
    gm, gn = m // tm, d // tn
    s_in, s_out, s_shapes, s_args = _side_cast_specs(sides, gm, gn)
    resident = lambda width: pl.BlockSpec((tm, width), lambda i, j: (i, 0), pipeline_mode=pl.Buffered(1))
    return pl.pallas_call(
        functools.partial(_merge_kernel, nside=len(sides)),
        grid=(gm, gn),
        in_specs=[resident(d),
                  pl.BlockSpec((None, len(ys) * d, tn), lambda i, j: (j, 0, 0))]
                 + [resident(y.shape[1]) for y in ys]
                 + [pl.BlockSpec((None, w_branch.shape[1], tn), lambda i, j: (l, 0, j))] + s_in,
        out_specs=[pl.BlockSpec((tm, tn), lambda i, j: (i, j))] + s_out,
        out_shape=[jax.ShapeDtypeStruct((m, d), BF16)] + s_shapes,
        compiler_params=_cparams(("parallel", "arbitrary")),
        name="gated_merge",
    )(h, w_gate_t, *ys, w_branch, *s_args)


def _cross_kernel(x_ref, gc_ref, wq_ref, kv_ref, wo_ref, gn_ref, o_ref, h_ref):
    nt = (((1,), (1,)), ((), ()))
    scale = np.float32(MEM_HEADDIM ** -0.5)
    hw = MEM_HEADS * MEM_HEADDIM
    x = x_ref[...]
    ms = jnp.mean(x * x, axis=-1, keepdims=True)
    xn = (x * lax.rsqrt(ms + EPS) * gc_ref[...]).astype(BF16)
    q = jnp.dot(xn, wq_ref[...].astype(BF16), preferred_element_type=F32).astype(BF16)
    heads = []
    for h in range(MEM_HEADS):
        hs = slice(h * MEM_HEADDIM, (h + 1) * MEM_HEADDIM)
        s = lax.dot_general(q[:, hs], kv_ref[:, hs], nt, preferred_element_type=F32) * scale
        e = jnp.exp(s - jnp.max(s, axis=-1, keepdims=True))
        den = jnp.sum(e, axis=-1, keepdims=True)
        vs = slice(hw + h * MEM_HEADDIM, hw + (h + 1) * MEM_HEADDIM)
        heads.append((jnp.dot(e.astype(BF16), kv_ref[:, vs], preferred_element_type=F32) / den).astype(BF16))
    att = jnp.concatenate(heads, axis=1)
    y = x + jnp.dot(att, wo_ref[...].astype(BF16), preferred_element_type=F32)
    o_ref[...] = y
    ms2 = jnp.mean(y * y, axis=-1, keepdims=True)
    h_ref[...] = (y * lax.rsqrt(ms2 + EPS) * gn_ref[...]).astype(h_ref.dtype)


def _cross_block(x, g_cross, w_q, kv, w_o, g_next, l, seq, mem_len, *, tm=256):
    m, d = x.shape
    hw = MEM_HEADS * MEM_HEADDIM
    tm = min(tm, seq)
    per_seq = seq // tm
    row = pl.BlockSpec((tm, d), lambda i: (i, 0))
    vec = pl.BlockSpec((1, d), lambda i: (0, 0))
    once = pl.Buffered(1)
    return pl.pallas_call(
        _cross_kernel,
        grid=(m // tm,),
        in_specs=[row, vec,
                  pl.BlockSpec((None, d, hw), lambda i: (l, 0, 0), pipeline_mode=once),
                  pl.BlockSpec((mem_len, 2 * hw), lambda i: (i // per_seq, 0)),
                  pl.BlockSpec((None, hw, d), lambda i: (l, 0, 0), pipeline_mode=once),
                  vec],
        out_specs=[row, row],
        out_shape=[jax.ShapeDtypeStruct((m, d), F32), jax.ShapeDtypeStruct((m, d), BF16)],
        compiler_params=_cparams(("parallel",)),
        name="mem_cross_block",
    )(x, g_cross.reshape(1, d), w_q, kv, w_o, g_next.reshape(1, d))


def kernel(x, mem, g_ffn1, w_ffn1_in, w_ffn1_out, g_mix, w_in, pool_w, pool_scale, sg_ln_g, sg_ln_b, sg_w, sg_b, ssm_conv_w, ssm_conv_b, ssm_a_log, ssm_dt_bias, ssm_d, ssm_norm_g, w_branch, w_gate, w_out, g_mem, g_cross, w_mem_q, w_mem_kv, w_mem_o, g_ffn2, w_ffn2_in, w_ffn2_out, g_final):
    bsz, seq, d = x.shape
    mem_len = mem.shape[1]
    depth = w_in.shape[0]
    m = bsz * seq
    bf = lambda a: a.astype(BF16)

    w_in_t = jnp.swapaxes(w_in, 1, 2)
    w_q = w_in_t[:, C_Q:C_KI]
    w_ki = w_in_t[:, C_KI:C_WI]
    w_s = jnp.concatenate(
        [w_ki, w_ki, w_in_t[:, C_DT:C_Q], w_in_t[:, C_WI:],
         jnp.zeros((depth, LANE - SSM_HEADS - IDX_HEADS, d), F32)], axis=1)
    w_gate_rows = w_gate.reshape(depth, -1, d)
    wb = bf(w_branch)
    pw = bf(pool_w)

    expand = lambda v: jnp.repeat(v, SSM_HEADDIM, axis=-1).reshape(depth, 1, SSM_INNER)
    a_exp = expand(-jnp.exp(ssm_a_log))
    d_exp = expand(ssm_d)
    dtb_pad = jnp.pad(ssm_dt_bias, ((0, 0), (0, LANE - SSM_HEADS))).reshape(depth, 1, LANE)
    e_mat = _head_expand_matrix()
    bias_tile = jnp.repeat(jnp.swapaxes(sg_b, 1, 2), SG_GW, axis=2)
    tabs = _rope_tables(seq)

    x2 = x.reshape(m, d)
    mem_n = _rmsnorm(mem.reshape(bsz * mem_len, d), g_mem, BF16)

    w1_in, l1_in = w_ffn1_in, 0
    for l in range(depth):
        x2, wg_t, wo_t = _ffn(x2, _rmsnorm(x2, g_ffn1[l], BF16), w1_in, l1_in, w_ffn1_out, l,
                              sides=[(w_gate_rows, l, MERGE_TILE), (w_out, l, 512)])

        h = _rmsnorm(x2, g_mix[l], BF16)
        seg_a = _matmul_nt(h, w_in_t, l, SEG_A, out_dtype=BF16)
        seg_q = _matmul_nt(h, w_q, l, SEG_Q, out_dtype=F32)
        seg_s = _matmul_nt(h, w_s, l, SEG_S, tn=256, out_dtype=F32)
        y_a = _pool_mixer(seg_a, pw, pool_scale, l, bsz, seq)
        y_b = _sg_mixer(seg_a, sg_ln_g, sg_ln_b, sg_w, bias_tile, l, m)
        y_c = _ssd_mixer(seg_a, seg_s, ssm_conv_w, ssm_conv_b, dtb_pad, a_exp, d_exp,
                         ssm_norm_g, e_mat, l, bsz, seq)
        y_d = _dsa_mixer(seg_q.reshape(bsz, seq, SEG_Q), seg_s.reshape(bsz, seq, SEG_S), tabs, bsz, seq)
        merged, w2_in_t = _gated_merge(h, (y_a, y_b, y_c, y_d.reshape(m, -1)), wg_t, wb, l,
                                       sides=[(w_ffn2_in, l, FFN_IN_TILE_BF16)])
        x2, = _matmul_res(merged, wo_t, None, x2, 1.0)

        kv = _matmul(mem_n, w_mem_kv, (l,), 2 * MEM_HEADS * MEM_HEADDIM, out_dtype=BF16)
        x2, h2 = _cross_block(x2, g_cross[l], w_mem_q, kv, w_mem_o, g_ffn2[l], l, seq, mem_len)

        nxt = [(w_ffn1_in, l + 1, FFN_IN_TILE_BF16)] if l + 1 < depth else []
        x2, *cast = _ffn(x2, h2, w2_in_t, None, w_ffn2_out, l, sides=nxt)
        if cast:
            w1_in, l1_in = cast[0], None

    return _rmsnorm(x2, g_final, F32).reshape(bsz, seq, d)
```

```python
import functools

import jax
import jax.numpy as jnp
import numpy as np
from jax import lax
from jax.experimental import pallas as pl
from jax.experimental.pallas import tpu as pltpu

F32 = jnp.float32
BF16 = jnp.bfloat16

CHUNK = 64
EPS = 1e-6
ROPE_THETA = 10000.0

POOL_WINDOWS = (2, 4, 8, 16)
POOL_GROUPS = 4
POOL_WIDTH = 2048
POOL_GW = POOL_WIDTH // POOL_GROUPS
POOL_PAD = 16

SG_WIDTH = 1024
SG_BLOCK = 128
SG_GROUPS = 4
SG_GW = SG_WIDTH // SG_GROUPS

SSM_HEADS = 16
SSM_HEADDIM = 64
SSM_INNER = SSM_HEADS * SSM_HEADDIM
SSM_GROUPS = 4
SSM_STATE = 128
SSM_CONV = 4
SSM_HG = SSM_HEADS // SSM_GROUPS
SSM_GW = SSM_INNER // SSM_GROUPS

ATT_HEADS = 8
ATT_KV_HEADS = 2
ATT_HEADDIM = 128
ATT_GRP = ATT_HEADS // ATT_KV_HEADS
IDX_HEADS = 8
IDX_HEADDIM = 64
IDX_TOPK = 256
Q_BLOCK = 128

MEM_HEADS = 4
MEM_HEADDIM = 128

C_POOL, C_U, C_V, C_Z, C_XBC = 0, 2048, 3072, 4096, 5120
C_DT, C_Q, C_K, C_VAL, C_QI, C_KI, C_WI = 7168, 7184, 8208, 8464, 8720, 9232, 9296
SEG_A = 7168
SEG_Q = 2048
SEG_S = 256
S_WI_LANE = 16

LANE = 128
VMEM_LIMIT = 56 * 1024 * 1024
NEG_BIG = -1e30
INT_MIN = -2147483648


def _cparams(sem):
    return pltpu.CompilerParams(dimension_semantics=sem, vmem_limit_bytes=VMEM_LIMIT)


def _sigmoid(x):
    return 1.0 / (1.0 + jnp.exp(-x))


def _silu(x):
    return x * _sigmoid(x)


def _gelu(x):
    return 0.5 * x * (1.0 + lax.erf(x * np.float32(1.0 / np.sqrt(2.0))))


def _softplus(x):
    return jnp.maximum(x, 0.0) + jnp.log1p(jnp.exp(-jnp.abs(x)))


def _rmsnorm_kernel(x_ref, g_ref, o_ref):
    x = x_ref[...]
    ms = jnp.mean(x * x, axis=-1, keepdims=True)
    o_ref[...] = (x * lax.rsqrt(ms + EPS) * g_ref[...]).astype(o_ref.dtype)


def _rmsnorm(x, g, out_dtype):
    m, d = x.shape
    tm = min(512, m)
    return pl.pallas_call(
        _rmsnorm_kernel,
        grid=(m // tm,),
        in_specs=[pl.BlockSpec((tm, d), lambda i: (i, 0)),
                  pl.BlockSpec((1, d), lambda i: (0, 0))],
        out_specs=pl.BlockSpec((tm, d), lambda i: (i, 0)),
        out_shape=jax.ShapeDtypeStruct((m, d), out_dtype),
        compiler_params=_cparams(("parallel",)),
        name="rmsnorm",
    )(x, g.reshape(1, d))


def _mm_kernel(x_ref, w_ref, o_ref):
    o_ref[...] = jnp.dot(x_ref[...], w_ref[...].astype(BF16), preferred_element_type=F32).astype(o_ref.dtype)


def _cast_tiles(src_ref, dst_ref):
    tc = dst_ref.shape[2]
    for t in range(dst_ref.shape[0]):
        dst_ref[t] = src_ref[:, t * tc:(t + 1) * tc].astype(BF16)


def _mm_res_kernel(x_ref, w_ref, r_ref, *rest, alpha, nside):
    o_ref = rest[nside]
    acc = jnp.dot(x_ref[...], w_ref[...].astype(BF16), preferred_element_type=F32)
    o_ref[...] = r_ref[...] + alpha * acc
    for k in range(nside):
        _cast_tiles(rest[k], rest[nside + 1 + k])


def _side_cast_specs(sides, gm, gn):
    in_specs, out_specs, out_shapes, args = [], [], [], []
    for src, l, tc in sides:
        _, rows, cols = src.shape
        rps = rows // (gm * gn)
        assert rps * gm * gn == rows and rps % 16 == 0 and cols % tc == 0, (src.shape, gm, gn)
        in_specs.append(pl.BlockSpec((None, rps, cols), lambda i, j, l=l: (l, i * gn + j, 0)))
        out_specs.append(pl.BlockSpec((cols // tc, rps, tc), lambda i, j: (0, i * gn + j, 0)))
        out_shapes.append(jax.ShapeDtypeStruct((cols // tc, rows, tc), BF16))
        args.append(src)
    return in_specs, out_specs, out_shapes, args


def _w_spec(widx, kdim, tn, col_blk0=0):
    if widx is None:
        return pl.BlockSpec((None, kdim, tn), lambda i, j: (j, 0, 0))
    lead = (None,) * len(widx)
    return pl.BlockSpec(lead + (kdim, tn), lambda i, j: (*widx, 0, j + col_blk0))


def _matmul(x, w, widx, n, *, col0=0, tm=1024, tn=512, out_dtype=BF16):
    m, kdim = x.shape
    tm, tn = min(tm, m), min(tn, n)
    return pl.pallas_call(
        _mm_kernel,
        grid=(m // tm, n // tn),
        in_specs=[pl.BlockSpec((tm, kdim), lambda i, j: (i, 0)),
                  _w_spec(widx, kdim, tn, col0 // tn)],
        out_specs=pl.BlockSpec((tm, tn), lambda i, j: (i, j)),
        out_shape=jax.ShapeDtypeStruct((m, n), out_dtype),
        compiler_params=_cparams(("parallel", "arbitrary")),
        name="matmul",
    )(x, w)


def _mm_nt_kernel(x_ref, wt_ref, o_ref):
    nt = (((1,), (1,)), ((), ()))
    o_ref[...] = lax.dot_general(x_ref[...], wt_ref[...].astype(BF16), nt,
                                 preferred_element_type=F32).astype(o_ref.dtype)


def _matmul_nt(x, wt, l, n, *, tm=1024, tn=512, out_dtype=BF16):
    m, kdim = x.shape
    tm, tn = min(tm, m), min(tn, n)
    return pl.pallas_call(
        _mm_nt_kernel,
        grid=(m // tm, n // tn),
        in_specs=[pl.BlockSpec((tm, kdim), lambda i, j: (i, 0)),
                  pl.BlockSpec((None, tn, kdim), lambda i, j: (l, j, 0))],
        out_specs=pl.BlockSpec((tm, tn), lambda i, j: (i, j)),
        out_shape=jax.ShapeDtypeStruct((m, n), out_dtype),
        compiler_params=_cparams(("parallel", "arbitrary")),
        name="matmul_nt",
    )(x, wt)


def _x_spec(tm, kdim):
    return pl.BlockSpec((tm, kdim), lambda i, j: (i, 0))


def _matmul_res(x, w, widx, res, alpha, *, tm=1024, tn=512, sides=()):
    m, kdim = x.shape
    n = res.shape[1]
    tm, tn = min(tm, m), min(tn, n)
    gm, gn = m // tm, n // tn
    s_in, s_out, s_shapes, s_args = _side_cast_specs(sides, gm, gn)
    tile = pl.BlockSpec((tm, tn), lambda i, j: (i, j))
    return pl.pallas_call(
        functools.partial(_mm_res_kernel, alpha=alpha, nside=len(sides)),
        grid=(gm, gn),
        in_specs=[_x_spec(tm, kdim), _w_spec(widx, kdim, tn), tile] + s_in,
        out_specs=[tile] + s_out,
        out_shape=[jax.ShapeDtypeStruct((m, n), F32)] + s_shapes,
        compiler_params=_cparams(("parallel", "arbitrary")),
        name="matmul_res",
    )(x, w, res, *s_args)


def _swiglu_kernel(x_ref, wg_ref, wu_ref, *rest, nside):
    o_ref, w_ref = rest[nside], rest[-1]
    tn = wg_ref.shape[1]
    w_ref[:, 0:tn] = wg_ref[...].astype(BF16)
    w_ref[:, tn:2 * tn] = wu_ref[...].astype(BF16)
    gu = jnp.dot(x_ref[...], w_ref[...], preferred_element_type=F32)
    o_ref[...] = (_silu(gu[:, 0:tn]) * gu[:, tn:2 * tn]).astype(o_ref.dtype)
    for k in range(nside):
        _cast_tiles(rest[k], rest[nside + 1 + k])


def _swiglu_bf16_kernel(x_ref, wg_ref, wu_ref, *rest, nside):
    x = x_ref[...]
    g = jnp.dot(x, wg_ref[...], preferred_element_type=F32)
    u = jnp.dot(x, wu_ref[...], preferred_element_type=F32)
    rest[nside][...] = (_silu(g) * u).astype(BF16)
    for k in range(nside):
        _cast_tiles(rest[k], rest[nside + 1 + k])


FFN_IN_TILE = 256
FFN_IN_TILE_BF16 = 512
FFN_OUT_TILE = 256


def _swiglu_in(x, w, l, *, tm=1024, sides=()):
    m, kdim = x.shape
    tm = min(tm, m)
    if l is None:
        tn = w.shape[2]
        f = w.shape[0] * tn // 2
        nb = f // tn
        w_specs = [pl.BlockSpec((None, kdim, tn), lambda i, j: (j, 0, 0)),
                   pl.BlockSpec((None, kdim, tn), lambda i, j: (j + nb, 0, 0))]
        body, scratch = _swiglu_bf16_kernel, []
    else:
        tn = FFN_IN_TILE
        f = w.shape[-1] // 2
        nb = f // tn
        w_specs = [pl.BlockSpec((None, kdim, tn), lambda i, j: (l, 0, j)),
                   pl.BlockSpec((None, kdim, tn), lambda i, j: (l, 0, j + nb))]
        body, scratch = _swiglu_kernel, [pltpu.VMEM((kdim, 2 * tn), BF16)]
    gm = m // tm
    s_in, s_out, s_shapes, s_args = _side_cast_specs(sides, gm, nb)
    return pl.pallas_call(
        functools.partial(body, nside=len(sides)),
        grid=(gm, nb),
        in_specs=[_x_spec(tm, kdim)] + w_specs + s_in,
        out_specs=[pl.BlockSpec((tm, tn), lambda i, j: (i, j))] + s_out,
        out_shape=[jax.ShapeDtypeStruct((m, f), BF16)] + s_shapes,
        scratch_shapes=scratch,
        compiler_params=_cparams(("parallel", "arbitrary")),
        name="swiglu_in",
    )(x, w, w, *s_args)


def _ffn(x, h, w_in, l_in, w_out, l, *, sides=()):
    act, w_out_t = _swiglu_in(h, w_in, l_in, sides=[(w_out, l, FFN_OUT_TILE)])
    return _matmul_res(act, w_out_t, None, x, 0.5, tm=1024, tn=FFN_OUT_TILE, sides=sides)


def _pool_kernel(a_ref, w_ref, s_ref, o_ref, pad_ref, *, seq, rows):
    g = pl.program_id(1)
    pad_ref[0:POOL_PAD, :] = jnp.zeros((POOL_PAD, POOL_GW), F32)
    pad_ref[POOL_PAD:POOL_PAD + seq, :] = a_ref[...].astype(F32)
    w = w_ref[...]
    scale = s_ref[...]
    for gi, win in enumerate(POOL_WINDOWS):

        @pl.when(g == gi)
        def _(win=win):
            for c in range(seq // rows):
                r0 = POOL_PAD + c * rows
                cur = pad_ref[r0:r0 + rows, :]
                tot = cur
                for k in range(1, win):
                    tot = tot + pad_ref[r0 - k:r0 - k + rows, :]
                t1 = lax.broadcasted_iota(jnp.int32, (rows, POOL_GW), 0) + (c * rows + 1)
                cnt = jnp.minimum(t1, win).astype(F32)
                mixed = (tot / cnt - cur).astype(BF16)
                y = jnp.dot(mixed, w, preferred_element_type=F32) * scale
                o_ref[c * rows:(c + 1) * rows, :] = y.astype(o_ref.dtype)


def _pool_mixer(seg_a, pool_w, pool_scale, l, bsz, seq):
    rows = min(256, seq)
    return pl.pallas_call(
        functools.partial(_pool_kernel, seq=seq, rows=rows),
        grid=(bsz, POOL_GROUPS),
        in_specs=[pl.BlockSpec((seq, POOL_GW), lambda b, g: (b, g)),
                  pl.BlockSpec((None, None, POOL_GW, POOL_GW), lambda b, g: (l, g, 0, 0)),
                  pl.BlockSpec((None, None, 1, POOL_GW), lambda b, g: (l, g, 0, 0))],
        out_specs=pl.BlockSpec((seq, POOL_GW), lambda b, g: (b, g)),
        out_shape=jax.ShapeDtypeStruct((bsz * seq, POOL_WIDTH), BF16),
        scratch_shapes=[pltpu.VMEM((POOL_PAD + seq, POOL_GW), F32)],
        compiler_params=_cparams(("parallel", "arbitrary")),
        name="pool_mixer",
    )(seg_a, pool_w, pool_scale.reshape(pool_scale.shape[0], POOL_GROUPS, 1, POOL_GW))


def _sg_kernel(u_ref, v_ref, g_ref, b_ref, w_ref, bias_ref, o_ref, *, nblk):
    ri = lax.broadcasted_iota(jnp.int32, (SG_BLOCK, SG_BLOCK), 0) // CHUNK
    ci = lax.broadcasted_iota(jnp.int32, (SG_BLOCK, SG_BLOCK), 1) // CHUNK
    causal = ri >= ci
    wm = [jnp.where(causal, w_ref[gi], 0.0).astype(BF16) for gi in range(SG_GROUPS)]
    bias = bias_ref[...]
    for n in range(nblk):
        rs = slice(n * SG_BLOCK, (n + 1) * SG_BLOCK)
        v = _gelu(v_ref[rs, :].astype(F32))
        mu = jnp.mean(v, axis=-1, keepdims=True)
        vc = v - mu
        var = jnp.mean(vc * vc, axis=-1, keepdims=True)
        vn = (vc * lax.rsqrt(var + EPS) * g_ref[...] + b_ref[...]).astype(BF16)
        u = _gelu(u_ref[rs, :].astype(F32))
        for gi in range(SG_GROUPS):
            cs = slice(gi * SG_GW, (gi + 1) * SG_GW)
            sv = jnp.dot(wm[gi], vn[:, cs], preferred_element_type=F32) + bias[:, cs]
            o_ref[rs, cs] = (u[:, cs] * sv).astype(o_ref.dtype)


def _sg_mixer(seg_a, ln_g, ln_b, sg_w, bias_tile, l, m):
    tb = min(512, m)
    ub, vb = C_U // SG_WIDTH, C_V // SG_WIDTH
    return pl.pallas_call(
        functools.partial(_sg_kernel, nblk=tb // SG_BLOCK),
        grid=(m // tb,),
        in_specs=[pl.BlockSpec((tb, SG_WIDTH), lambda i: (i, ub)),
                  pl.BlockSpec((tb, SG_WIDTH), lambda i: (i, vb)),
                  pl.BlockSpec((None, 1, SG_WIDTH), lambda i: (l, 0, 0)),
                  pl.BlockSpec((None, 1, SG_WIDTH), lambda i: (l, 0, 0)),
                  pl.BlockSpec((None, SG_GROUPS, SG_BLOCK, SG_BLOCK), lambda i: (l, 0, 0, 0)),
                  pl.BlockSpec((None, SG_BLOCK, SG_WIDTH), lambda i: (l, 0, 0))],
        out_specs=pl.BlockSpec((tb, SG_WIDTH), lambda i: (i, 0)),
        out_shape=jax.ShapeDtypeStruct((m, SG_WIDTH), BF16),
        compiler_params=_cparams(("parallel",)),
        name="sg_mixer",
    )(seg_a, seg_a, ln_g.reshape(-1, 1, SG_WIDTH), ln_b.reshape(-1, 1, SG_WIDTH), sg_w, bias_tile)


def _ssd_kernel(z_ref, xc_ref, xp_ref, bc_ref, bp_ref, dt_ref,
                cwx_ref, cwb_ref, cbx_ref, cbb_ref, dtb_ref, aexp_ref, dexp_ref, ng_ref, e_ref,
                o_ref, st_ref):
    c = pl.program_id(1)

    @pl.when(c == 0)
    def _():
        st_ref[...] = jnp.zeros(st_ref.shape, F32)

    has_prev = c > 0
    srow = lax.broadcasted_iota(jnp.int32, (3 * CHUNK, 2 * CHUNK), 0)
    scol = lax.broadcasted_iota(jnp.int32, (3 * CHUNK, 2 * CHUNK), 1)
    shift = jnp.where(scol == CHUNK + (srow % CHUNK) - (3 - srow // CHUNK), 1.0, 0.0).astype(BF16)

    def conv(cur_ref, prev_ref, w_ref, b_ref):
        cur = cur_ref[...]
        prev = jnp.where(has_prev, prev_ref[...], jnp.zeros_like(cur))
        both = jnp.concatenate([prev, cur], axis=0)
        sh = jnp.dot(shift, both, preferred_element_type=F32)
        w = w_ref[...]
        acc = cur.astype(F32) * w[3:4, :] + b_ref[...]
        for k in range(SSM_CONV - 1):
            acc = acc + sh[k * CHUNK:(k + 1) * CHUNK, :] * w[k:k + 1, :]
        return _silu(acc)

    xs = conv(xc_ref, xp_ref, cwx_ref, cbx_ref)
    bcv = conv(bc_ref, bp_ref, cwb_ref, cbb_ref)
    gn = SSM_GROUPS * SSM_STATE
    bm = bcv[:, :gn].astype(BF16)
    cm = bcv[:, gn:].astype(BF16)

    def split3(v):
        p1 = v.astype(BF16)
        r1 = v - p1.astype(F32)
        p2 = r1.astype(BF16)
        return p1, p2, (r1 - p2.astype(F32)).astype(BF16)

    def rows_times(m01, v):
        m3 = jnp.concatenate([m01.astype(BF16)] * 3, axis=1)
        return jnp.dot(m3, jnp.concatenate(split3(v), axis=0), preferred_element_type=F32)

    lane = lax.broadcasted_iota(jnp.int32, (CHUNK, LANE), 1)
    dt = jnp.where(lane < SSM_HEADS, _softplus(dt_ref[...] + dtb_ref[...]), 0.0)
    d1, d2, d3 = split3(dt)
    dt3 = (d1.astype(F32) + pltpu.roll(d2.astype(F32), SSM_HEADS, 1)
           + pltpu.roll(d3.astype(F32), 2 * SSM_HEADS, 1)).astype(BF16)
    dt_e = jnp.dot(dt3, e_ref[...], preferred_element_type=F32)
    a_e = dt_e * aexp_ref[...]
    r64 = lax.broadcasted_iota(jnp.int32, (CHUNK, CHUNK), 0)
    c64 = lax.broadcasted_iota(jnp.int32, (CHUNK, CHUNK), 1)
    a_cs = rows_times(jnp.where(c64 <= r64, 1.0, 0.0), a_e)
    rl = lax.broadcasted_iota(jnp.int32, (CHUNK, SSM_INNER), 0)
    cl = lax.broadcasted_iota(jnp.int32, (CHUNK, SSM_INNER), 1) % SSM_HEADDIM
    diag = jnp.where(rl == cl, a_cs, 0.0)
    a_row = rows_times(jnp.ones((CHUNK, CHUNK), F32), diag)
    decay = jnp.exp(jnp.where(rl >= cl, a_cs - a_row, NEG_BIG))
    a_last = a_cs[CHUNK - 1:CHUNK, :]
    xd = xs * dt_e
    xe = (xd * jnp.exp(a_last - a_cs)).astype(BF16)
    xdb = xd.astype(BF16)
    ea = jnp.exp(a_cs)
    cdec = jnp.exp(a_last)

    br = lax.broadcasted_iota(jnp.int32, (SSM_GW, SSM_GW), 0) // SSM_HEADDIM
    bc_ = lax.broadcasted_iota(jnp.int32, (SSM_GW, SSM_GW), 1) // SSM_HEADDIM
    blockdiag = br == bc_
    nt = (((1,), (1,)), ((), ()))
    tn = (((0,), (0,)), ((), ()))
    ys = []
    for g in range(SSM_GROUPS):
        ns = slice(g * SSM_STATE, (g + 1) * SSM_STATE)
        ls = slice(g * SSM_GW, (g + 1) * SSM_GW)
        cg, bg = cm[:, ns], bm[:, ns]
        b_t = jnp.concatenate([bg] * SSM_HG, axis=0)
        cb = lax.dot_general(cg, b_t, nt, preferred_element_type=F32)
        mg = (cb * decay[:, ls]).astype(BF16)
        xg = xdb[:, ls]
        bd = jnp.where(blockdiag, jnp.concatenate([xg] * SSM_HG, axis=0), jnp.zeros((), BF16))
        y_diag = jnp.dot(mg, bd, preferred_element_type=F32)
        st = st_ref[g]
        y_off = jnp.dot(cg, st.astype(BF16), preferred_element_type=F32) * ea[:, ls]
        upd = lax.dot_general(bg, xe[:, ls], tn, preferred_element_type=F32)
        st_ref[g] = st * cdec[:, ls] + upd
        ys.append(y_diag + y_off)
    y = jnp.concatenate(ys, axis=1) + xs * dexp_ref[...]
    y = y * _silu(z_ref[...].astype(F32))
    outs = []
    for g in range(SSM_GROUPS):
        yg = y[:, g * SSM_GW:(g + 1) * SSM_GW]
        ms = jnp.mean(yg * yg, axis=-1, keepdims=True)
        outs.append(yg * lax.rsqrt(ms + EPS))
    o_ref[...] = (jnp.concatenate(outs, axis=1) * ng_ref[...]).astype(o_ref.dtype)


def _head_expand_matrix():
    r = jnp.arange(LANE)[:, None]
    c = jnp.arange(SSM_INNER)[None, :]
    return ((r % SSM_HEADS == c // SSM_HEADDIM) & (r < 3 * SSM_HEADS)).astype(BF16)


def _ssd_mixer(seg_a, seg_s, conv_w, conv_b, dtb_pad, a_exp, d_exp, norm_g, e_mat, l, bsz, seq):
    nc = seq // CHUNK
    zb, xb, bb = C_Z // SSM_INNER, C_XBC // SSM_INNER, C_XBC // SSM_INNER + 1

    def row(b, c):
        return b * nc + c

    def prow(b, c):
        return b * nc + jnp.maximum(c - 1, 0)

    vec = lambda blk: pl.BlockSpec((None, 1, SSM_INNER), lambda b, c: (l, 0, blk))
    return pl.pallas_call(
        _ssd_kernel,
        grid=(bsz, nc),
        in_specs=[pl.BlockSpec((CHUNK, SSM_INNER), lambda b, c: (row(b, c), zb)),
                  pl.BlockSpec((CHUNK, SSM_INNER), lambda b, c: (row(b, c), xb)),
                  pl.BlockSpec((CHUNK, SSM_INNER), lambda b, c: (prow(b, c), xb)),
                  pl.BlockSpec((CHUNK, SSM_INNER), lambda b, c: (row(b, c), bb)),
                  pl.BlockSpec((CHUNK, SSM_INNER), lambda b, c: (prow(b, c), bb)),
                  pl.BlockSpec((CHUNK, LANE), lambda b, c: (row(b, c), 1)),
                  pl.BlockSpec((None, SSM_CONV, SSM_INNER), lambda b, c: (l, 0, 0)),
                  pl.BlockSpec((None, SSM_CONV, SSM_INNER), lambda b, c: (l, 0, 1)),
                  vec(0), vec(1),
                  pl.BlockSpec((None, 1, LANE), lambda b, c: (l, 0, 0)),
                  vec(0), vec(0), vec(0),
                  pl.BlockSpec((LANE, SSM_INNER), lambda b, c: (0, 0))],
        out_specs=pl.BlockSpec((CHUNK, SSM_INNER), lambda b, c: (row(b, c), 0)),
        out_shape=jax.ShapeDtypeStruct((bsz * seq, SSM_INNER), BF16),
        scratch_shapes=[pltpu.VMEM((SSM_GROUPS, SSM_STATE, SSM_GW), F32)],
        compiler_params=_cparams(("parallel", "arbitrary")),
        name="ssd_mixer",
    )(seg_a, seg_a, seg_a, seg_a, seg_a, seg_s, conv_w, conv_w,
      conv_b.reshape(-1, 1, 2 * SSM_INNER), conv_b.reshape(-1, 1, 2 * SSM_INNER),
      dtb_pad, a_exp, d_exp, norm_g.reshape(-1, 1, SSM_INNER), e_mat)


def _rope128(x, cos, sin_signed):
    return x * cos + pltpu.roll(x, ATT_HEADDIM // 2, 1) * sin_signed


def _rope64(x, cos, sin_signed):
    lane = lax.broadcasted_iota(jnp.int32, x.shape, 1)
    low = (lane % IDX_HEADDIM) < IDX_HEADDIM // 2
    rot = jnp.where(low, pltpu.roll(x, LANE - IDX_HEADDIM // 2, 1), pltpu.roll(x, IDX_HEADDIM // 2, 1))
    return x * cos + rot * sin_signed


def _dsa_kernel(q_ref, k_ref, v_ref, qi_ref, ki_ref, wi_ref,
                cq_ref, sq_ref, ck_ref, sk_ref, ciq_ref, siq_ref, cik_ref, sik_ref, *rest,
                klen, q0, topk):
    o_ref, kr_ref, vb_ref, kir_ref, key_ref, bias_ref, pos_ref = rest[-7:]
    i = pl.program_id(1)
    nt = (((1,), (1,)), ((), ()))
    nlc = klen // LANE

    @pl.when(i == 0)
    def _():
        for kv in range(ATT_KV_HEADS):
            hs = slice(kv * ATT_HEADDIM, (kv + 1) * ATT_HEADDIM)
            kr_ref[:, hs] = _rope128(k_ref[:, hs], ck_ref[...], sk_ref[...]).astype(BF16)
        vb_ref[...] = v_ref[...].astype(BF16)
        kx = _rope64(ki_ref[...], cik_ref[...], sik_ref[...])
        kx_hi = kx.astype(BF16)
        kir_ref[:, 0:LANE] = kx_hi
        kir_ref[:, LANE:2 * LANE] = (kx - kx_hi.astype(F32)).astype(BF16)

    lane_q = lax.broadcasted_iota(jnp.int32, (Q_BLOCK, LANE), 1)
    kir = kir_ref[...]
    wi = wi_ref[...] * np.float32(IDX_HEADS ** -0.5)
    iscore = jnp.zeros((Q_BLOCK, klen), F32)
    for quad in range(IDX_HEADS // 4):
        parts = []
        for pair in range(2 * quad, 2 * quad + 2):
            ps = slice(pair * LANE, (pair + 1) * LANE)
            qp = _rope64(qi_ref[:, ps], ciq_ref[...], siq_ref[...])
            q_hi = qp.astype(BF16).astype(F32)
            q_lo_swapped = pltpu.roll(qp - q_hi, IDX_HEADDIM, 1)
            for sub in range(2):
                own = (lane_q // IDX_HEADDIM) == sub
                parts.append(jnp.concatenate([jnp.where(own, q_hi, q_lo_swapped), jnp.where(own, q_hi, 0.0)],
                                             axis=1).astype(BF16))
        logits = lax.dot_general(jnp.concatenate(parts, axis=0), kir, nt,
                                 preferred_element_type=F32)
        for hh in range(4):
            h = 4 * quad + hh
            wcol = wi[:, S_WI_LANE + h:S_WI_LANE + h + 1]
            iscore = iscore + jnp.maximum(logits[hh * Q_BLOCK:(hh + 1) * Q_BLOCK, :], 0.0) * wcol

    qchunk = (lax.broadcasted_iota(jnp.int32, (Q_BLOCK, klen), 0) + (q0 + i) * Q_BLOCK) // CHUNK
    kchunk = lax.broadcasted_iota(jnp.int32, (Q_BLOCK, klen), 1) // CHUNK
    iscore = jnp.where(iscore == 0.0, 0.0, iscore)
    bits = pltpu.bitcast(iscore, jnp.int32)
    key = jnp.where(bits < 0, bits ^ jnp.int32(0x7FFFFFFF), bits)
    key = jnp.maximum(key, jnp.int32(INT_MIN + 1))
    key_ref[...] = jnp.where(kchunk <= qchunk, key, jnp.int32(INT_MIN))

    def row_count(pred, rows=slice(0, Q_BLOCK)):
        nrow = rows.stop - rows.start
        acc = jnp.zeros((nrow, LANE), F32)
        for cidx in range(nlc):
            acc = acc + jnp.where(pred(key_ref[rows, cidx * LANE:(cidx + 1) * LANE], cidx), 1.0, 0.0)
        return jnp.broadcast_to(jnp.sum(acc, axis=-1, keepdims=True), (nrow, LANE))

    halves = (slice(0, Q_BLOCK // 2), slice(Q_BLOCK // 2, Q_BLOCK))

    def thr_step(it, t_us):
        bit = jnp.left_shift(jnp.int32(1), 31 - it)
        out = []
        for rows, t_u in zip(halves, t_us):
            cand_u = t_u | bit
            cand = cand_u ^ jnp.int32(INT_MIN)
            out.append(jnp.where(row_count(lambda kc, _: kc >= cand, rows) >= topk, cand_u, t_u))
        return tuple(out)

    t_us = lax.fori_loop(0, 32, thr_step, tuple(jnp.zeros((Q_BLOCK // 2, LANE), jnp.int32) for _ in halves),
                         unroll=8)
    thr = jnp.concatenate(t_us, axis=0) ^ jnp.int32(INT_MIN)
    cnt_ge = row_count(lambda kc, _: kc >= thr)
    cnt_gt = row_count(lambda kc, _: kc > thr)
    need = topk - cnt_gt
    excess = jnp.where(thr > INT_MIN, cnt_ge - topk, 0.0)
    pos_ref[...] = jnp.full((Q_BLOCK, LANE), klen, jnp.int32)
    lane_pos = lax.broadcasted_iota(jnp.int32, (Q_BLOCK, LANE), 1)

    @pl.when(jnp.max(excess) > 0.0)
    def _():
        nbits = int(klen - 1).bit_length()
        never = jnp.int32(1 << 30)

        def pos_step(it, bound):
            cand = bound | jnp.left_shift(jnp.int32(1), nbits - 1 - it)
            ties = row_count(lambda kc, cidx: jnp.where(kc == thr, lane_pos + cidx * LANE, never) < cand)
            return jnp.where(ties < need, cand, bound)

        bound = lax.fori_loop(0, nbits, pos_step, jnp.zeros((Q_BLOCK, LANE), jnp.int32))
        pos_ref[...] = bound + 1

    pos = pos_ref[...]
    for cidx in range(nlc):
        cs = slice(cidx * LANE, (cidx + 1) * LANE)
        kc = key_ref[:, cs]
        tie = jnp.where(lane_pos + cidx * LANE < pos, 0.0, NEG_BIG)
        sel = jnp.where(kc > thr, 0.0, jnp.where(kc == thr, tie, NEG_BIG))
        bias_ref[:, cs] = jnp.where(kc == INT_MIN, NEG_BIG, sel)

    scale = np.float32(ATT_HEADDIM ** -0.5)
    for kv in range(ATT_KV_HEADS):
        hs = slice(kv * ATT_HEADDIM, (kv + 1) * ATT_HEADDIM)
        krh = kr_ref[:, hs]
        vh = vb_ref[:, hs]
        heads = [slice((kv * ATT_GRP + gq) * ATT_HEADDIM, (kv * ATT_GRP + gq + 1) * ATT_HEADDIM)
                 for gq in range(ATT_GRP)]
        qg = jnp.concatenate([(_rope128(q_ref[:, qs], cq_ref[...], sq_ref[...]) * scale).astype(BF16)
                              for qs in heads], axis=0)
        s = lax.dot_general(qg, krh, nt, preferred_element_type=F32)
        es, dens = [], []
        for gq in range(ATT_GRP):
            sg = s[gq * Q_BLOCK:(gq + 1) * Q_BLOCK, :] + bias_ref[...]
            e = jnp.exp(sg - jnp.max(sg, axis=-1, keepdims=True))
            dens.append(jnp.sum(e, axis=-1, keepdims=True))
            es.append(e.astype(BF16))
        o = jnp.dot(jnp.concatenate(es, axis=0), vh, preferred_element_type=F32)
        for gq, qs in enumerate(heads):
            o_ref[:, qs] = (o[gq * Q_BLOCK:(gq + 1) * Q_BLOCK, :] / dens[gq]).astype(o_ref.dtype)


DSA_BUCKETS = 8


def _dsa_mixer(seg_q, seg_s, tabs, bsz, seq):
    nqb = seq // Q_BLOCK
    topk = min(IDX_TOPK, seq // 4)
    cos128, sin128, cos64, sin64 = tabs
    kb, vb_, qib = 1024 // 256, 1280 // 256, 1536 // 512
    nbk = DSA_BUCKETS if nqb % DSA_BUCKETS == 0 else 1
    qpb = nqb // nbk
    width = ATT_HEADS * ATT_HEADDIM
    kvw = ATT_KV_HEADS * ATT_HEADDIM
    out = None
    for u in range(nbk):
        q0 = u * qpb
        klen = (u + 1) * qpb * Q_BLOCK
        qtab = pl.BlockSpec((Q_BLOCK, LANE), lambda b, i, q0=q0: (q0 + i, 0))
        ktab = pl.BlockSpec((klen, LANE), lambda b, i: (0, 0))
        qrow = lambda blk, q0=q0: (lambda b, i: (b, q0 + i, blk))
        in_specs = [pl.BlockSpec((None, Q_BLOCK, width), qrow(0)),
                    pl.BlockSpec((None, klen, kvw), lambda b, i: (b, 0, kb)),
                    pl.BlockSpec((None, klen, kvw), lambda b, i: (b, 0, vb_)),
                    pl.BlockSpec((None, Q_BLOCK, IDX_HEADS * IDX_HEADDIM), qrow(qib)),
                    pl.BlockSpec((None, klen, LANE), lambda b, i: (b, 0, 0)),
                    pl.BlockSpec((None, Q_BLOCK, LANE), qrow(1)),
                    qtab, qtab, ktab, ktab, qtab, qtab, ktab, ktab]
        args = [seg_q, seg_q, seg_q, seg_q, seg_s, seg_s,
                cos128, sin128, cos128, sin128, cos64, sin64, cos64, sin64]
        aliases = {}
        if out is not None:
            in_specs.append(pl.BlockSpec(memory_space=pl.ANY))
            args.append(out)
            aliases = {len(args) - 1: 0}
        out = pl.pallas_call(
            functools.partial(_dsa_kernel, klen=klen, q0=q0, topk=topk),
            grid=(bsz, qpb),
            in_specs=in_specs,
            out_specs=pl.BlockSpec((None, Q_BLOCK, width), qrow(0)),
            out_shape=jax.ShapeDtypeStruct((bsz, seq, width), BF16),
            scratch_shapes=[pltpu.VMEM((klen, kvw), BF16),
                            pltpu.VMEM((klen, kvw), BF16),
                            pltpu.VMEM((klen, 2 * LANE), BF16),
                            pltpu.VMEM((Q_BLOCK, klen), jnp.int32),
                            pltpu.VMEM((Q_BLOCK, klen), F32),
                            pltpu.VMEM((Q_BLOCK, LANE), jnp.int32)],
            input_output_aliases=aliases,
            compiler_params=_cparams(("parallel", "arbitrary")),
            name="dsa_mixer",
        )(*args)
    return out


def _rope_tables(seq):
    pos = jnp.arange(seq, dtype=F32)[:, None]

    def tab(half, reps):
        inv = ROPE_THETA ** (-jnp.arange(half, dtype=F32) / half)
        ang = pos * inv[None, :]
        cos, sin = jnp.cos(ang), jnp.sin(ang)
        return (jnp.tile(jnp.concatenate([cos, cos], axis=1), (1, reps)),
                jnp.tile(jnp.concatenate([-sin, sin], axis=1), (1, reps)))

    cos128, sin128 = tab(ATT_HEADDIM // 2, 1)
    cos64, sin64 = tab(IDX_HEADDIM // 2, 2)
    return cos128, sin128, cos64, sin64


MERGE_TILE = 256
OUT_PROJ_TILE = 1024


def _merge_kernel(h_ref, wg_ref, ya_ref, yb_ref, yc_ref, yd_ref, p_ref, *rest, nside):
    h = h_ref[...]
    d = h.shape[1]
    acc = None
    row = 0
    for i, y_ref in enumerate((ya_ref, yb_ref, yc_ref, yd_ref)):
        width = y_ref.shape[1]
        gate = _sigmoid(jnp.dot(h, wg_ref[i * d:(i + 1) * d, :], preferred_element_type=F32))
        term = gate * jnp.dot(y_ref[...], p_ref[row:row + width, :], preferred_element_type=F32)
        acc = term if acc is None else acc + term
        row += width
    rest[nside][...] = acc.astype(BF16)
    for k in range(nside):
        _cast_tiles(rest[k], rest[nside + 1 + k])


def _gated_merge(h, ys, w_gate_t, w_branch, l, *, tm=1024, sides=()):
    m, d = h.shape
    tm, tn = min(tm, m), MERGE_TILE
    gm, gn = m // tm, d // tn
    s_in, s_out, s_shapes, s_args = _side_cast_specs(sides, gm, gn)
    resident = lambda width: pl.BlockSpec((tm, width), lambda i, j: (i, 0), pipeline_mode=pl.Buffered(1))
    return pl.pallas_call(
        functools.partial(_merge_kernel, nside=len(sides)),
        grid=(gm, gn),
        in_specs=[resident(d),
                  pl.BlockSpec((None, len(ys) * d, tn), lambda i, j: (j, 0, 0))]
                 + [resident(y.shape[1]) for y in ys]
                 + [pl.BlockSpec((None, w_branch.shape[1], tn), lambda i, j: (l, 0, j))] + s_in,
        out_specs=[pl.BlockSpec((tm, tn), lambda i, j: (i, j))] + s_out,
        out_shape=[jax.ShapeDtypeStruct((m, d), BF16)] + s_shapes,
        compiler_params=_cparams(("parallel", "arbitrary")),
        name="gated_merge",
    )(h, w_gate_t, *ys, w_branch, *s_args)


def _cross_kernel(x_ref, gc_ref, wq_ref, kv_ref, wo_ref, gn_ref, o_ref, h_ref):
    nt = (((1,), (1,)), ((), ()))
    scale = np.float32(MEM_HEADDIM ** -0.5)
    hw = MEM_HEADS * MEM_HEADDIM
    x = x_ref[...]
    ms = jnp.mean(x * x, axis=-1, keepdims=True)
    xn = (x * lax.rsqrt(ms + EPS) * gc_ref[...]).astype(BF16)
    q = jnp.dot(xn, wq_ref[...].astype(BF16), preferred_element_type=F32).astype(BF16)
    heads = []
    for h in range(MEM_HEADS):
        hs = slice(h * MEM_HEADDIM, (h + 1) * MEM_HEADDIM)
        s = lax.dot_general(q[:, hs], kv_ref[:, hs], nt, preferred_element_type=F32) * scale
        e = jnp.exp(s - jnp.max(s, axis=-1, keepdims=True))
        den = jnp.sum(e, axis=-1, keepdims=True)
        vs = slice(hw + h * MEM_HEADDIM, hw + (h + 1) * MEM_HEADDIM)
        heads.append((jnp.dot(e.astype(BF16), kv_ref[:, vs], preferred_element_type=F32) / den).astype(BF16))
    att = jnp.concatenate(heads, axis=1)
    y = x + jnp.dot(att, wo_ref[...].astype(BF16), preferred_element_type=F32)
    o_ref[...] = y
    ms2 = jnp.mean(y * y, axis=-1, keepdims=True)
    h_ref[...] = (y * lax.rsqrt(ms2 + EPS) * gn_ref[...]).astype(h_ref.dtype)


def _cross_block(x, g_cross, w_q, kv, w_o, g_next, l, seq, mem_len, *, tm=256):
    m, d = x.shape
    hw = MEM_HEADS * MEM_HEADDIM
    tm = min(tm, seq)
    per_seq = seq // tm
    row = pl.BlockSpec((tm, d), lambda i: (i, 0))
    vec = pl.BlockSpec((1, d), lambda i: (0, 0))
    once = pl.Buffered(1)
    return pl.pallas_call(
        _cross_kernel,
        grid=(m // tm,),
        in_specs=[row, vec,
                  pl.BlockSpec((None, d, hw), lambda i: (l, 0, 0), pipeline_mode=once),
                  pl.BlockSpec((mem_len, 2 * hw), lambda i: (i // per_seq, 0)),
                  pl.BlockSpec((None, hw, d), lambda i: (l, 0, 0), pipeline_mode=once),
                  vec],
        out_specs=[row, row],
        out_shape=[jax.ShapeDtypeStruct((m, d), F32), jax.ShapeDtypeStruct((m, d), BF16)],
        compiler_params=_cparams(("parallel",)),
        name="mem_cross_block",
    )(x, g_cross.reshape(1, d), w_q, kv, w_o, g_next.reshape(1, d))


def kernel(x, mem, g_ffn1, w_ffn1_in, w_ffn1_out, g_mix, w_in, pool_w, pool_scale, sg_ln_g, sg_ln_b, sg_w, sg_b, ssm_conv_w, ssm_conv_b, ssm_a_log, ssm_dt_bias, ssm_d, ssm_norm_g, w_branch, w_gate, w_out, g_mem, g_cross, w_mem_q, w_mem_kv, w_mem_o, g_ffn2, w_ffn2_in, w_ffn2_out, g_final):
    bsz, seq, d = x.shape
    mem_len = mem.shape[1]
    depth = w_in.shape[0]
    m = bsz * seq
    bf = lambda a: a.astype(BF16)

    w_in_t = jnp.swapaxes(w_in, 1, 2)
    w_q = w_in_t[:, C_Q:C_KI]
    w_ki = w_in_t[:, C_KI:C_WI]
    w_s = jnp.concatenate(
        [w_ki, w_ki, w_in_t[:, C_DT:C_Q], w_in_t[:, C_WI:],
         jnp.zeros((depth, LANE - SSM_HEADS - IDX_HEADS, d), F32)], axis=1)
    w_gate_rows = w_gate.reshape(depth, -1, d)
    wb = bf(w_branch)
    pw = bf(pool_w)

    expand = lambda v: jnp.repeat(v, SSM_HEADDIM, axis=-1).reshape(depth, 1, SSM_INNER)
    a_exp = expand(-jnp.exp(ssm_a_log))
    d_exp = expand(ssm_d)
    dtb_pad = jnp.pad(ssm_dt_bias, ((0, 0), (0, LANE - SSM_HEADS))).reshape(depth, 1, LANE)
    e_mat = _head_expand_matrix()
    bias_tile = jnp.repeat(jnp.swapaxes(sg_b, 1, 2), SG_GW, axis=2)
    tabs = _rope_tables(seq)

    x2 = x.reshape(m, d)
    mem_n = _rmsnorm(mem.reshape(bsz * mem_len, d), g_mem, BF16)

    w1_in, l1_in = w_ffn1_in, 0
    for l in range(depth):
        x2, wg_t, wo_t = _ffn(x2, _rmsnorm(x2, g_ffn1[l], BF16), w1_in, l1_in, w_ffn1_out, l,
                              sides=[(w_gate_rows, l, MERGE_TILE), (w_out, l, OUT_PROJ_TILE)])

        h = _rmsnorm(x2, g_mix[l], BF16)
        seg_a = _matmul_nt(h, w_in_t, l, SEG_A, out_dtype=BF16)
        seg_q = _matmul_nt(h, w_q, l, SEG_Q, out_dtype=F32)
        seg_s = _matmul_nt(h, w_s, l, SEG_S, tn=256, out_dtype=F32)
        y_a = _pool_mixer(seg_a, pw, pool_scale, l, bsz, seq)
        y_b = _sg_mixer(seg_a, sg_ln_g, sg_ln_b, sg_w, bias_tile, l, m)
        y_c = _ssd_mixer(seg_a, seg_s, ssm_conv_w, ssm_conv_b, dtb_pad, a_exp, d_exp,
                         ssm_norm_g, e_mat, l, bsz, seq)
        y_d = _dsa_mixer(seg_q.reshape(bsz, seq, SEG_Q), seg_s.reshape(bsz, seq, SEG_S), tabs, bsz, seq)
        merged, w2_in_t = _gated_merge(h, (y_a, y_b, y_c, y_d.reshape(m, -1)), wg_t, wb, l,
                                       sides=[(w_ffn2_in, l, FFN_IN_TILE_BF16)])
        x2, = _matmul_res(merged, wo_t, None, x2, 1.0, tn=OUT_PROJ_TILE)

        kv = _matmul(mem_n, w_mem_kv, (l,), 2 * MEM_HEADS * MEM_HEADDIM, out_dtype=BF16)
        x2, h2 = _cross_block(x2, g_cross[l], w_mem_q, kv, w_mem_o, g_ffn2[l], l, seq, mem_len)

        nxt = [(w_ffn1_in, l + 1, FFN_IN_TILE_BF16)] if l + 1 < depth else []
        x2, *cast = _ffn(x2, h2, w2_in_t, None, w_ffn2_out, l, sides=nxt)
        if cast:
            w1_in, l1_in = cast[0], None

    return _rmsnorm(x2, g_final, F32).reshape(bsz, seq, d)
```

```python
import functools

import jax
import jax.numpy as jnp
import numpy as np
from jax import lax
from jax.experimental import pallas as pl
from jax.experimental.pallas import tpu as pltpu

F32 = jnp.float32
BF16 = jnp.bfloat16

CHUNK = 64
EPS = 1e-6
ROPE_THETA = 10000.0

POOL_WINDOWS = (2, 4, 8, 16)
POOL_GROUPS = 4
POOL_WIDTH = 2048
POOL_GW = POOL_WIDTH // POOL_GROUPS
POOL_PAD = 16

SG_WIDTH = 1024
SG_BLOCK = 128
SG_GROUPS = 4
SG_GW = SG_WIDTH // SG_GROUPS

SSM_HEADS = 16
SSM_HEADDIM = 64
SSM_INNER = SSM_HEADS * SSM_HEADDIM
SSM_GROUPS = 4
SSM_STATE = 128
SSM_CONV = 4
SSM_HG = SSM_HEADS // SSM_GROUPS
SSM_GW = SSM_INNER // SSM_GROUPS

ATT_HEADS = 8
ATT_KV_HEADS = 2
ATT_HEADDIM = 128
ATT_GRP = ATT_HEADS // ATT_KV_HEADS
IDX_HEADS = 8
IDX_HEADDIM = 64
IDX_TOPK = 256
Q_BLOCK = 128

MEM_HEADS = 4
MEM_HEADDIM = 128

C_POOL, C_U, C_V, C_Z, C_XBC = 0, 2048, 3072, 4096, 5120
C_DT, C_Q, C_K, C_VAL, C_QI, C_KI, C_WI = 7168, 7184, 8208, 8464, 8720, 9232, 9296
SEG_A = 7168
SEG_Q = 2048
SEG_S = 256
S_WI_LANE = 16

LANE = 128
VMEM_LIMIT = 56 * 1024 * 1024
NEG_BIG = -1e30
INT_MIN = -2147483648


def _cparams(sem):
    return pltpu.CompilerParams(dimension_semantics=sem, vmem_limit_bytes=VMEM_LIMIT)


def _sigmoid(x):
    return 1.0 / (1.0 + jnp.exp(-x))


def _silu(x):
    return x * _sigmoid(x)


def _gelu(x):
    return 0.5 * x * (1.0 + lax.erf(x * np.float32(1.0 / np.sqrt(2.0))))


def _softplus(x):
    return jnp.maximum(x, 0.0) + jnp.log1p(jnp.exp(-jnp.abs(x)))


def _rmsnorm_kernel(x_ref, g_ref, o_ref):
    x = x_ref[...]
    ms = jnp.mean(x * x, axis=-1, keepdims=True)
    o_ref[...] = (x * lax.rsqrt(ms + EPS) * g_ref[...]).astype(o_ref.dtype)


def _rmsnorm(x, g, out_dtype):
    m, d = x.shape
    tm = min(512, m)
    return pl.pallas_call(
        _rmsnorm_kernel,
        grid=(m // tm,),
        in_specs=[pl.BlockSpec((tm, d), lambda i: (i, 0)),
                  pl.BlockSpec((1, d), lambda i: (0, 0))],
        out_specs=pl.BlockSpec((tm, d), lambda i: (i, 0)),
        out_shape=jax.ShapeDtypeStruct((m, d), out_dtype),
        compiler_params=_cparams(("parallel",)),
        name="rmsnorm",
    )(x, g.reshape(1, d))


def _mm_kernel(x_ref, w_ref, o_ref):
    o_ref[...] = jnp.dot(x_ref[...], w_ref[...].astype(BF16), preferred_element_type=F32).astype(o_ref.dtype)


def _cast_tiles(src_ref, dst_ref):
    tc = dst_ref.shape[2]
    for t in range(dst_ref.shape[0]):
        dst_ref[t] = src_ref[:, t * tc:(t + 1) * tc].astype(BF16)


def _mm_res_kernel(x_ref, w_ref, r_ref, *rest, alpha, nside):
    o_ref = rest[nside]
    acc = jnp.dot(x_ref[...], w_ref[...].astype(BF16), preferred_element_type=F32)
    o_ref[...] = r_ref[...] + alpha * acc
    for k in range(nside):
        _cast_tiles(rest[k], rest[nside + 1 + k])


def _side_cast_specs(sides, gm, gn):
    in_specs, out_specs, out_shapes, args = [], [], [], []
    for src, l, tc in sides:
        _, rows, cols = src.shape
        rps = rows // (gm * gn)
        assert rps * gm * gn == rows and rps % 16 == 0 and cols % tc == 0, (src.shape, gm, gn)
        in_specs.append(pl.BlockSpec((None, rps, cols), lambda i, j, l=l: (l, i * gn + j, 0)))
        out_specs.append(pl.BlockSpec((cols // tc, rps, tc), lambda i, j: (0, i * gn + j, 0)))
        out_shapes.append(jax.ShapeDtypeStruct((cols // tc, rows, tc), BF16))
        args.append(src)
    return in_specs, out_specs, out_shapes, args


def _w_spec(widx, kdim, tn, col_blk0=0):
    if widx is None:
        return pl.BlockSpec((None, kdim, tn), lambda i, j: (j, 0, 0))
    lead = (None,) * len(widx)
    return pl.BlockSpec(lead + (kdim, tn), lambda i, j: (*widx, 0, j + col_blk0))


def _matmul(x, w, widx, n, *, col0=0, tm=1024, tn=512, out_dtype=BF16):
    m, kdim = x.shape
    tm, tn = min(tm, m), min(tn, n)
    return pl.pallas_call(
        _mm_kernel,
        grid=(m // tm, n // tn),
        in_specs=[pl.BlockSpec((tm, kdim), lambda i, j: (i, 0)),
                  _w_spec(widx, kdim, tn, col0 // tn)],
        out_specs=pl.BlockSpec((tm, tn), lambda i, j: (i, j)),
        out_shape=jax.ShapeDtypeStruct((m, n), out_dtype),
        compiler_params=_cparams(("parallel", "arbitrary")),
        name="matmul",
    )(x, w)


def _mm_nt_kernel(x_ref, wt_ref, o_ref):
    nt = (((1,), (1,)), ((), ()))
    o_ref[...] = lax.dot_general(x_ref[...], wt_ref[...].astype(BF16), nt,
                                 preferred_element_type=F32).astype(o_ref.dtype)


def _matmul_nt(x, wt, l, n, *, tm=1024, tn=512, out_dtype=BF16):
    m, kdim = x.shape
    tm, tn = min(tm, m), min(tn, n)
    return pl.pallas_call(
        _mm_nt_kernel,
        grid=(m // tm, n // tn),
        in_specs=[pl.BlockSpec((tm, kdim), lambda i, j: (i, 0)),
                  pl.BlockSpec((None, tn, kdim), lambda i, j: (l, j, 0))],
        out_specs=pl.BlockSpec((tm, tn), lambda i, j: (i, j)),
        out_shape=jax.ShapeDtypeStruct((m, n), out_dtype),
        compiler_params=_cparams(("parallel", "arbitrary")),
        name="matmul_nt",
    )(x, wt)


def _x_spec(tm, kdim):
    return pl.BlockSpec((tm, kdim), lambda i, j: (i, 0))


def _matmul_res(x, w, widx, res, alpha, *, tm=1024, tn=512, sides=()):
    m, kdim = x.shape
    n = res.shape[1]
    tm, tn = min(tm, m), min(tn, n)
    gm, gn = m // tm, n // tn
    s_in, s_out, s_shapes, s_args = _side_cast_specs(sides, gm, gn)
    tile = pl.BlockSpec((tm, tn), lambda i, j: (i, j))
    return pl.pallas_call(
        functools.partial(_mm_res_kernel, alpha=alpha, nside=len(sides)),
        grid=(gm, gn),
        in_specs=[_x_spec(tm, kdim), _w_spec(widx, kdim, tn), tile] + s_in,
        out_specs=[tile] + s_out,
        out_shape=[jax.ShapeDtypeStruct((m, n), F32)] + s_shapes,
        compiler_params=_cparams(("parallel", "arbitrary")),
        name="matmul_res",
    )(x, w, res, *s_args)


def _swiglu_kernel(x_ref, wg_ref, wu_ref, *rest, nside):
    o_ref, w_ref = rest[nside], rest[-1]
    tn = wg_ref.shape[1]
    w_ref[:, 0:tn] = wg_ref[...].astype(BF16)
    w_ref[:, tn:2 * tn] = wu_ref[...].astype(BF16)
    gu = jnp.dot(x_ref[...], w_ref[...], preferred_element_type=F32)
    o_ref[...] = (_silu(gu[:, 0:tn]) * gu[:, tn:2 * tn]).astype(o_ref.dtype)
    for k in range(nside):
        _cast_tiles(rest[k], rest[nside + 1 + k])


def _swiglu_bf16_kernel(x_ref, wg_ref, wu_ref, *rest, nside):
    x = x_ref[...]
    g = jnp.dot(x, wg_ref[...], preferred_element_type=F32)
    u = jnp.dot(x, wu_ref[...], preferred_element_type=F32)
    rest[nside][...] = (_silu(g) * u).astype(BF16)
    for k in range(nside):
        _cast_tiles(rest[k], rest[nside + 1 + k])


FFN_IN_TILE = 256
FFN_IN_TILE_BF16 = 512
FFN_OUT_TILE = 256


def _swiglu_in(x, w, l, *, tm=1024, sides=()):
    m, kdim = x.shape
    tm = min(tm, m)
    if l is None:
        tn = w.shape[2]
        f = w.shape[0] * tn // 2
        nb = f // tn
        w_specs = [pl.BlockSpec((None, kdim, tn), lambda i, j: (j, 0, 0)),
                   pl.BlockSpec((None, kdim, tn), lambda i, j: (j + nb, 0, 0))]
        body, scratch = _swiglu_bf16_kernel, []
    else:
        tn = FFN_IN_TILE
        f = w.shape[-1] // 2
        nb = f // tn
        w_specs = [pl.BlockSpec((None, kdim, tn), lambda i, j: (l, 0, j)),
                   pl.BlockSpec((None, kdim, tn), lambda i, j: (l, 0, j + nb))]
        body, scratch = _swiglu_kernel, [pltpu.VMEM((kdim, 2 * tn), BF16)]
    gm = m // tm
    s_in, s_out, s_shapes, s_args = _side_cast_specs(sides, gm, nb)
    return pl.pallas_call(
        functools.partial(body, nside=len(sides)),
        grid=(gm, nb),
        in_specs=[_x_spec(tm, kdim)] + w_specs + s_in,
        out_specs=[pl.BlockSpec((tm, tn), lambda i, j: (i, j))] + s_out,
        out_shape=[jax.ShapeDtypeStruct((m, f), BF16)] + s_shapes,
        scratch_shapes=scratch,
        compiler_params=_cparams(("parallel", "arbitrary")),
        name="swiglu_in",
    )(x, w, w, *s_args)


def _ffn(x, h, w_in, l_in, w_out, l, *, sides=()):
    act, w_out_t = _swiglu_in(h, w_in, l_in, sides=[(w_out, l, FFN_OUT_TILE)])
    return _matmul_res(act, w_out_t, None, x, 0.5, tm=1024, tn=FFN_OUT_TILE, sides=sides)


def _pool_kernel(a_ref, w_ref, s_ref, o_ref, pad_ref, *, seq, rows):
    g = pl.program_id(1)
    pad_ref[0:POOL_PAD, :] = jnp.zeros((POOL_PAD, POOL_GW), F32)
    pad_ref[POOL_PAD:POOL_PAD + seq, :] = a_ref[...].astype(F32)
    w = w_ref[...]
    scale = s_ref[...]
    for gi, win in enumerate(POOL_WINDOWS):

        @pl.when(g == gi)
        def _(win=win):
            for c in range(seq // rows):
                r0 = POOL_PAD + c * rows
                cur = pad_ref[r0:r0 + rows, :]
                tot = cur
                for k in range(1, win):
                    tot = tot + pad_ref[r0 - k:r0 - k + rows, :]
                t1 = lax.broadcasted_iota(jnp.int32, (rows, POOL_GW), 0) + (c * rows + 1)
                cnt = jnp.minimum(t1, win).astype(F32)
                mixed = (tot / cnt - cur).astype(BF16)
                y = jnp.dot(mixed, w, preferred_element_type=F32) * scale
                o_ref[c * rows:(c + 1) * rows, :] = y.astype(o_ref.dtype)


def _pool_mixer(seg_a, pool_w, pool_scale, l, bsz, seq):
    rows = min(256, seq)
    return pl.pallas_call(
        functools.partial(_pool_kernel, seq=seq, rows=rows),
        grid=(bsz, POOL_GROUPS),
        in_specs=[pl.BlockSpec((seq, POOL_GW), lambda b, g: (b, g)),
                  pl.BlockSpec((None, None, POOL_GW, POOL_GW), lambda b, g: (l, g, 0, 0)),
                  pl.BlockSpec((None, None, 1, POOL_GW), lambda b, g: (l, g, 0, 0))],
        out_specs=pl.BlockSpec((seq, POOL_GW), lambda b, g: (b, g)),
        out_shape=jax.ShapeDtypeStruct((bsz * seq, POOL_WIDTH), BF16),
        scratch_shapes=[pltpu.VMEM((POOL_PAD + seq, POOL_GW), F32)],
        compiler_params=_cparams(("parallel", "arbitrary")),
        name="pool_mixer",
    )(seg_a, pool_w, pool_scale.reshape(pool_scale.shape[0], POOL_GROUPS, 1, POOL_GW))


def _sg_kernel(u_ref, v_ref, g_ref, b_ref, w_ref, bias_ref, o_ref, *, nblk):
    ri = lax.broadcasted_iota(jnp.int32, (SG_BLOCK, SG_BLOCK), 0) // CHUNK
    ci = lax.broadcasted_iota(jnp.int32, (SG_BLOCK, SG_BLOCK), 1) // CHUNK
    causal = ri >= ci
    wm = [jnp.where(causal, w_ref[gi], 0.0).astype(BF16) for gi in range(SG_GROUPS)]
    bias = bias_ref[...]
    for n in range(nblk):
        rs = slice(n * SG_BLOCK, (n + 1) * SG_BLOCK)
        v = _gelu(v_ref[rs, :].astype(F32))
        mu = jnp.mean(v, axis=-1, keepdims=True)
        vc = v - mu
        var = jnp.mean(vc * vc, axis=-1, keepdims=True)
        vn = (vc * lax.rsqrt(var + EPS) * g_ref[...] + b_ref[...]).astype(BF16)
        u = _gelu(u_ref[rs, :].astype(F32))
        for gi in range(SG_GROUPS):
            cs = slice(gi * SG_GW, (gi + 1) * SG_GW)
            sv = jnp.dot(wm[gi], vn[:, cs], preferred_element_type=F32) + bias[:, cs]
            o_ref[rs, cs] = (u[:, cs] * sv).astype(o_ref.dtype)


def _sg_mixer(seg_a, ln_g, ln_b, sg_w, bias_tile, l, m):
    tb = min(512, m)
    ub, vb = C_U // SG_WIDTH, C_V // SG_WIDTH
    return pl.pallas_call(
        functools.partial(_sg_kernel, nblk=tb // SG_BLOCK),
        grid=(m // tb,),
        in_specs=[pl.BlockSpec((tb, SG_WIDTH), lambda i: (i, ub)),
                  pl.BlockSpec((tb, SG_WIDTH), lambda i: (i, vb)),
                  pl.BlockSpec((None, 1, SG_WIDTH), lambda i: (l, 0, 0)),
                  pl.BlockSpec((None, 1, SG_WIDTH), lambda i: (l, 0, 0)),
                  pl.BlockSpec((None, SG_GROUPS, SG_BLOCK, SG_BLOCK), lambda i: (l, 0, 0, 0)),
                  pl.BlockSpec((None, SG_BLOCK, SG_WIDTH), lambda i: (l, 0, 0))],
        out_specs=pl.BlockSpec((tb, SG_WIDTH), lambda i: (i, 0)),
        out_shape=jax.ShapeDtypeStruct((m, SG_WIDTH), BF16),
        compiler_params=_cparams(("parallel",)),
        name="sg_mixer",
    )(seg_a, seg_a, ln_g.reshape(-1, 1, SG_WIDTH), ln_b.reshape(-1, 1, SG_WIDTH), sg_w, bias_tile)


def _ssd_kernel(z_ref, xc_ref, xp_ref, bc_ref, bp_ref, dt_ref,
                cwx_ref, cwb_ref, cbx_ref, cbb_ref, dtb_ref, aexp_ref, dexp_ref, ng_ref, e_ref,
                o_ref, st_ref):
    c = pl.program_id(1)

    @pl.when(c == 0)
    def _():
        st_ref[...] = jnp.zeros(st_ref.shape, F32)

    has_prev = c > 0
    srow = lax.broadcasted_iota(jnp.int32, (3 * CHUNK, 2 * CHUNK), 0)
    scol = lax.broadcasted_iota(jnp.int32, (3 * CHUNK, 2 * CHUNK), 1)
    shift = jnp.where(scol == CHUNK + (srow % CHUNK) - (3 - srow // CHUNK), 1.0, 0.0).astype(BF16)

    def conv(cur_ref, prev_ref, w_ref, b_ref):
        cur = cur_ref[...]
        prev = jnp.where(has_prev, prev_ref[...], jnp.zeros_like(cur))
        both = jnp.concatenate([prev, cur], axis=0)
        sh = jnp.dot(shift, both, preferred_element_type=F32)
        w = w_ref[...]
        acc = cur.astype(F32) * w[3:4, :] + b_ref[...]
        for k in range(SSM_CONV - 1):
            acc = acc + sh[k * CHUNK:(k + 1) * CHUNK, :] * w[k:k + 1, :]
        return _silu(acc)

    xs = conv(xc_ref, xp_ref, cwx_ref, cbx_ref)
    bcv = conv(bc_ref, bp_ref, cwb_ref, cbb_ref)
    gn = SSM_GROUPS * SSM_STATE
    bm = bcv[:, :gn].astype(BF16)
    cm = bcv[:, gn:].astype(BF16)

    def split3(v):
        p1 = v.astype(BF16)
        r1 = v - p1.astype(F32)
        p2 = r1.astype(BF16)
        return p1, p2, (r1 - p2.astype(F32)).astype(BF16)

    def rows_times(m01, v):
        m3 = jnp.concatenate([m01.astype(BF16)] * 3, axis=1)
        return jnp.dot(m3, jnp.concatenate(split3(v), axis=0), preferred_element_type=F32)

    lane = lax.broadcasted_iota(jnp.int32, (CHUNK, LANE), 1)
    dt = jnp.where(lane < SSM_HEADS, _softplus(dt_ref[...] + dtb_ref[...]), 0.0)
    d1, d2, d3 = split3(dt)
    dt3 = (d1.astype(F32) + pltpu.roll(d2.astype(F32), SSM_HEADS, 1)
           + pltpu.roll(d3.astype(F32), 2 * SSM_HEADS, 1)).astype(BF16)
    dt_e = jnp.dot(dt3, e_ref[...], preferred_element_type=F32)
    a_e = dt_e * aexp_ref[...]
    r64 = lax.broadcasted_iota(jnp.int32, (CHUNK, CHUNK), 0)
    c64 = lax.broadcasted_iota(jnp.int32, (CHUNK, CHUNK), 1)
    a_cs = rows_times(jnp.where(c64 <= r64, 1.0, 0.0), a_e)
    rl = lax.broadcasted_iota(jnp.int32, (CHUNK, SSM_INNER), 0)
    cl = lax.broadcasted_iota(jnp.int32, (CHUNK, SSM_INNER), 1) % SSM_HEADDIM
    diag = jnp.where(rl == cl, a_cs, 0.0)
    a_row = rows_times(jnp.ones((CHUNK, CHUNK), F32), diag)
    decay = jnp.exp(jnp.where(rl >= cl, a_cs - a_row, NEG_BIG))
    a_last = a_cs[CHUNK - 1:CHUNK, :]
    xd = xs * dt_e
    xe = (xd * jnp.exp(a_last - a_cs)).astype(BF16)
    xdb = xd.astype(BF16)
    ea = jnp.exp(a_cs)
    cdec = jnp.exp(a_last)

    br = lax.broadcasted_iota(jnp.int32, (SSM_GW, SSM_GW), 0) // SSM_HEADDIM
    bc_ = lax.broadcasted_iota(jnp.int32, (SSM_GW, SSM_GW), 1) // SSM_HEADDIM
    blockdiag = br == bc_
    nt = (((1,), (1,)), ((), ()))
    tn = (((0,), (0,)), ((), ()))
    ys = []
    for g in range(SSM_GROUPS):
        ns = slice(g * SSM_STATE, (g + 1) * SSM_STATE)
        ls = slice(g * SSM_GW, (g + 1) * SSM_GW)
        cg, bg = cm[:, ns], bm[:, ns]
        b_t = jnp.concatenate([bg] * SSM_HG, axis=0)
        cb = lax.dot_general(cg, b_t, nt, preferred_element_type=F32)
        mg = (cb * decay[:, ls]).astype(BF16)
        xg = xdb[:, ls]
        bd = jnp.where(blockdiag, jnp.concatenate([xg] * SSM_HG, axis=0), jnp.zeros((), BF16))
        y_diag = jnp.dot(mg, bd, preferred_element_type=F32)
        st = st_ref[g]
        y_off = jnp.dot(cg, st.astype(BF16), preferred_element_type=F32) * ea[:, ls]
        upd = lax.dot_general(bg, xe[:, ls], tn, preferred_element_type=F32)
        st_ref[g] = st * cdec[:, ls] + upd
        ys.append(y_diag + y_off)
    y = jnp.concatenate(ys, axis=1) + xs * dexp_ref[...]
    y = y * _silu(z_ref[...].astype(F32))
    outs = []
    for g in range(SSM_GROUPS):
        yg = y[:, g * SSM_GW:(g + 1) * SSM_GW]
        ms = jnp.mean(yg * yg, axis=-1, keepdims=True)
        outs.append(yg * lax.rsqrt(ms + EPS))
    o_ref[...] = (jnp.concatenate(outs, axis=1) * ng_ref[...]).astype(o_ref.dtype)


def _head_expand_matrix():
    r = jnp.arange(LANE)[:, None]
    c = jnp.arange(SSM_INNER)[None, :]
    return ((r % SSM_HEADS == c // SSM_HEADDIM) & (r < 3 * SSM_HEADS)).astype(BF16)


def _ssd_mixer(seg_a, seg_s, conv_w, conv_b, dtb_pad, a_exp, d_exp, norm_g, e_mat, l, bsz, seq):
    nc = seq // CHUNK
    zb, xb, bb = C_Z // SSM_INNER, C_XBC // SSM_INNER, C_XBC // SSM_INNER + 1

    def row(b, c):
        return b * nc + c

    def prow(b, c):
        return b * nc + jnp.maximum(c - 1, 0)

    vec = lambda blk: pl.BlockSpec((None, 1, SSM_INNER), lambda b, c: (l, 0, blk))
    return pl.pallas_call(
        _ssd_kernel,
        grid=(bsz, nc),
        in_specs=[pl.BlockSpec((CHUNK, SSM_INNER), lambda b, c: (row(b, c), zb)),
                  pl.BlockSpec((CHUNK, SSM_INNER), lambda b, c: (row(b, c), xb)),
                  pl.BlockSpec((CHUNK, SSM_INNER), lambda b, c: (prow(b, c), xb)),
                  pl.BlockSpec((CHUNK, SSM_INNER), lambda b, c: (row(b, c), bb)),
                  pl.BlockSpec((CHUNK, SSM_INNER), lambda b, c: (prow(b, c), bb)),
                  pl.BlockSpec((CHUNK, LANE), lambda b, c: (row(b, c), 1)),
                  pl.BlockSpec((None, SSM_CONV, SSM_INNER), lambda b, c: (l, 0, 0)),
                  pl.BlockSpec((None, SSM_CONV, SSM_INNER), lambda b, c: (l, 0, 1)),
                  vec(0), vec(1),
                  pl.BlockSpec((None, 1, LANE), lambda b, c: (l, 0, 0)),
                  vec(0), vec(0), vec(0),
                  pl.BlockSpec((LANE, SSM_INNER), lambda b, c: (0, 0))],
        out_specs=pl.BlockSpec((CHUNK, SSM_INNER), lambda b, c: (row(b, c), 0)),
        out_shape=jax.ShapeDtypeStruct((bsz * seq, SSM_INNER), BF16),
        scratch_shapes=[pltpu.VMEM((SSM_GROUPS, SSM_STATE, SSM_GW), F32)],
        compiler_params=_cparams(("parallel", "arbitrary")),
        name="ssd_mixer",
    )(seg_a, seg_a, seg_a, seg_a, seg_a, seg_s, conv_w, conv_w,
      conv_b.reshape(-1, 1, 2 * SSM_INNER), conv_b.reshape(-1, 1, 2 * SSM_INNER),
      dtb_pad, a_exp, d_exp, norm_g.reshape(-1, 1, SSM_INNER), e_mat)


def _rope128(x, cos, sin_signed):
    return x * cos + pltpu.roll(x, ATT_HEADDIM // 2, 1) * sin_signed


def _rope64(x, cos, sin_signed):
    lane = lax.broadcasted_iota(jnp.int32, x.shape, 1)
    low = (lane % IDX_HEADDIM) < IDX_HEADDIM // 2
    rot = jnp.where(low, pltpu.roll(x, LANE - IDX_HEADDIM // 2, 1), pltpu.roll(x, IDX_HEADDIM // 2, 1))
    return x * cos + rot * sin_signed


def _dsa_kernel(q_ref, k_ref, v_ref, qi_ref, ki_ref, wi_ref,
                cq_ref, sq_ref, ck_ref, sk_ref, ciq_ref, siq_ref, cik_ref, sik_ref, *rest,
                klen, q0, topk):
    o_ref, kr_ref, vb_ref, kir_ref, key_ref, bias_ref, pos_ref = rest[-7:]
    i = pl.program_id(1)
    nt = (((1,), (1,)), ((), ()))
    nlc = klen // LANE

    @pl.when(i == 0)
    def _():
        for kv in range(ATT_KV_HEADS):
            hs = slice(kv * ATT_HEADDIM, (kv + 1) * ATT_HEADDIM)
            kr_ref[:, hs] = _rope128(k_ref[:, hs], ck_ref[...], sk_ref[...]).astype(BF16)
        vb_ref[...] = v_ref[...].astype(BF16)
        kx = _rope64(ki_ref[...], cik_ref[...], sik_ref[...])
        kx_hi = kx.astype(BF16)
        kir_ref[:, 0:LANE] = kx_hi
        kir_ref[:, LANE:2 * LANE] = (kx - kx_hi.astype(F32)).astype(BF16)

    lane_q = lax.broadcasted_iota(jnp.int32, (Q_BLOCK, LANE), 1)
    wi = wi_ref[...] * np.float32(IDX_HEADS ** -0.5)
    quads = []
    for quad in range(IDX_HEADS // 4):
        parts = []
        for pair in range(2 * quad, 2 * quad + 2):
            ps = slice(pair * LANE, (pair + 1) * LANE)
            qp = _rope64(qi_ref[:, ps], ciq_ref[...], siq_ref[...])
            q_hi = qp.astype(BF16).astype(F32)
            q_lo_swapped = pltpu.roll(qp - q_hi, IDX_HEADDIM, 1)
            for sub in range(2):
                own = (lane_q // IDX_HEADDIM) == sub
                parts.append(jnp.concatenate([jnp.where(own, q_hi, q_lo_swapped), jnp.where(own, q_hi, 0.0)],
                                             axis=1).astype(BF16))
        quads.append(jnp.concatenate(parts, axis=0))
    wcols = [wi[:, S_WI_LANE + h:S_WI_LANE + h + 1] for h in range(IDX_HEADS)]

    kc = 512 if klen % 512 == 0 else 256
    qchunk = (lax.broadcasted_iota(jnp.int32, (Q_BLOCK, kc), 0) + (q0 + i) * Q_BLOCK) // CHUNK
    kiota = lax.broadcasted_iota(jnp.int32, (Q_BLOCK, kc), 1)
    for c0 in range(0, klen, kc):
        kchunk_rows = kir_ref[c0:c0 + kc, :]
        iscore = jnp.zeros((Q_BLOCK, kc), F32)
        for quad in range(IDX_HEADS // 4):
            logits = lax.dot_general(quads[quad], kchunk_rows, nt, preferred_element_type=F32)
            for hh in range(4):
                iscore = iscore + jnp.maximum(logits[hh * Q_BLOCK:(hh + 1) * Q_BLOCK, :], 0.0) * wcols[4 * quad + hh]
        iscore = jnp.where(iscore == 0.0, 0.0, iscore)
        bits = pltpu.bitcast(iscore, jnp.int32)
        key = jnp.where(bits < 0, bits ^ jnp.int32(0x7FFFFFFF), bits)
        key = jnp.maximum(key, jnp.int32(INT_MIN + 1))
        key_ref[:, c0:c0 + kc] = jnp.where((kiota + c0) // CHUNK <= qchunk, key, jnp.int32(INT_MIN))

    def row_count(pred, rows=slice(0, Q_BLOCK)):
        nrow = rows.stop - rows.start
        acc = jnp.zeros((nrow, LANE), F32)
        for cidx in range(nlc):
            acc = acc + jnp.where(pred(key_ref[rows, cidx * LANE:(cidx + 1) * LANE], cidx), 1.0, 0.0)
        return jnp.broadcast_to(jnp.sum(acc, axis=-1, keepdims=True), (nrow, LANE))

    halves = (slice(0, Q_BLOCK // 2), slice(Q_BLOCK // 2, Q_BLOCK))

    def thr_step(it, t_us):
        bit = jnp.left_shift(jnp.int32(1), 31 - it)
        out = []
        for rows, t_u in zip(halves, t_us):
            cand_u = t_u | bit
            cand = cand_u ^ jnp.int32(INT_MIN)
            out.append(jnp.where(row_count(lambda kc, _: kc >= cand, rows) >= topk, cand_u, t_u))
        return tuple(out)

    t_us = lax.fori_loop(0, 32, thr_step, tuple(jnp.zeros((Q_BLOCK // 2, LANE), jnp.int32) for _ in halves),
                         unroll=8)
    thr = jnp.concatenate(t_us, axis=0) ^ jnp.int32(INT_MIN)
    cnt_ge = row_count(lambda kc, _: kc >= thr)
    cnt_gt = row_count(lambda kc, _: kc > thr)
    need = topk - cnt_gt
    excess = jnp.where(thr > INT_MIN, cnt_ge - topk, 0.0)
    pos_ref[...] = jnp.full((Q_BLOCK, LANE), klen, jnp.int32)
    lane_pos = lax.broadcasted_iota(jnp.int32, (Q_BLOCK, LANE), 1)

    @pl.when(jnp.max(excess) > 0.0)
    def _():
        nbits = int(klen - 1).bit_length()
        never = jnp.int32(1 << 30)

        def pos_step(it, bound):
            cand = bound | jnp.left_shift(jnp.int32(1), nbits - 1 - it)
            ties = row_count(lambda kc, cidx: jnp.where(kc == thr, lane_pos + cidx * LANE, never) < cand)
            return jnp.where(ties < need, cand, bound)

        bound = lax.fori_loop(0, nbits, pos_step, jnp.zeros((Q_BLOCK, LANE), jnp.int32))
        pos_ref[...] = bound + 1

    pos = pos_ref[...]
    for cidx in range(nlc):
        cs = slice(cidx * LANE, (cidx + 1) * LANE)
        kc = key_ref[:, cs]
        tie = jnp.where(lane_pos + cidx * LANE < pos, 0.0, NEG_BIG)
        sel = jnp.where(kc > thr, 0.0, jnp.where(kc == thr, tie, NEG_BIG))
        bias_ref[:, cs] = jnp.where(kc == INT_MIN, NEG_BIG, sel)

    scale = np.float32(ATT_HEADDIM ** -0.5)
    for kv in range(ATT_KV_HEADS):
        hs = slice(kv * ATT_HEADDIM, (kv + 1) * ATT_HEADDIM)
        krh = kr_ref[:, hs]
        vh = vb_ref[:, hs]
        heads = [slice((kv * ATT_GRP + gq) * ATT_HEADDIM, (kv * ATT_GRP + gq + 1) * ATT_HEADDIM)
                 for gq in range(ATT_GRP)]
        qg = jnp.concatenate([(_rope128(q_ref[:, qs], cq_ref[...], sq_ref[...]) * scale).astype(BF16)
                              for qs in heads], axis=0)
        s = lax.dot_general(qg, krh, nt, preferred_element_type=F32)
        es, dens = [], []
        for gq in range(ATT_GRP):
            sg = s[gq * Q_BLOCK:(gq + 1) * Q_BLOCK, :] + bias_ref[...]
            e = jnp.exp(sg - jnp.max(sg, axis=-1, keepdims=True))
            dens.append(jnp.sum(e, axis=-1, keepdims=True))
            es.append(e.astype(BF16))
        o = jnp.dot(jnp.concatenate(es, axis=0), vh, preferred_element_type=F32)
        for gq, qs in enumerate(heads):
            o_ref[:, qs] = (o[gq * Q_BLOCK:(gq + 1) * Q_BLOCK, :] / dens[gq]).astype(o_ref.dtype)


DSA_BUCKETS = 8


def _dsa_mixer(seg_q, seg_s, tabs, bsz, seq):
    nqb = seq // Q_BLOCK
    topk = min(IDX_TOPK, seq // 4)
    cos128, sin128, cos64, sin64 = tabs
    kb, vb_, qib = 1024 // 256, 1280 // 256, 1536 // 512
    nbk = DSA_BUCKETS if nqb % DSA_BUCKETS == 0 else 1
    qpb = nqb // nbk
    width = ATT_HEADS * ATT_HEADDIM
    kvw = ATT_KV_HEADS * ATT_HEADDIM
    out = None
    for u in range(nbk):
        q0 = u * qpb
        klen = (u + 1) * qpb * Q_BLOCK
        qtab = pl.BlockSpec((Q_BLOCK, LANE), lambda b, i, q0=q0: (q0 + i, 0))
        ktab = pl.BlockSpec((klen, LANE), lambda b, i: (0, 0))
        qrow = lambda blk, q0=q0: (lambda b, i: (b, q0 + i, blk))
        in_specs = [pl.BlockSpec((None, Q_BLOCK, width), qrow(0)),
                    pl.BlockSpec((None, klen, kvw), lambda b, i: (b, 0, kb)),
                    pl.BlockSpec((None, klen, kvw), lambda b, i: (b, 0, vb_)),
                    pl.BlockSpec((None, Q_BLOCK, IDX_HEADS * IDX_HEADDIM), qrow(qib)),
                    pl.BlockSpec((None, klen, LANE), lambda b, i: (b, 0, 0)),
                    pl.BlockSpec((None, Q_BLOCK, LANE), qrow(1)),
                    qtab, qtab, ktab, ktab, qtab, qtab, ktab, ktab]
        args = [seg_q, seg_q, seg_q, seg_q, seg_s, seg_s,
                cos128, sin128, cos128, sin128, cos64, sin64, cos64, sin64]
        aliases = {}
        if out is not None:
            in_specs.append(pl.BlockSpec(memory_space=pl.ANY))
            args.append(out)
            aliases = {len(args) - 1: 0}
        out = pl.pallas_call(
            functools.partial(_dsa_kernel, klen=klen, q0=q0, topk=topk),
            grid=(bsz, qpb),
            in_specs=in_specs,
            out_specs=pl.BlockSpec((None, Q_BLOCK, width), qrow(0)),
            out_shape=jax.ShapeDtypeStruct((bsz, seq, width), BF16),
            scratch_shapes=[pltpu.VMEM((klen, kvw), BF16),
                            pltpu.VMEM((klen, kvw), BF16),
                            pltpu.VMEM((klen, 2 * LANE), BF16),
                            pltpu.VMEM((Q_BLOCK, klen), jnp.int32),
                            pltpu.VMEM((Q_BLOCK, klen), F32),
                            pltpu.VMEM((Q_BLOCK, LANE), jnp.int32)],
            input_output_aliases=aliases,
            compiler_params=_cparams(("parallel", "arbitrary")),
            name="dsa_mixer",
        )(*args)
    return out


def _rope_tables(seq):
    pos = jnp.arange(seq, dtype=F32)[:, None]

    def tab(half, reps):
        inv = ROPE_THETA ** (-jnp.arange(half, dtype=F32) / half)
        ang = pos * inv[None, :]
        cos, sin = jnp.cos(ang), jnp.sin(ang)
        return (jnp.tile(jnp.concatenate([cos, cos], axis=1), (1, reps)),
                jnp.tile(jnp.concatenate([-sin, sin], axis=1), (1, reps)))

    cos128, sin128 = tab(ATT_HEADDIM // 2, 1)
    cos64, sin64 = tab(IDX_HEADDIM // 2, 2)
    return cos128, sin128, cos64, sin64


MERGE_TILE = 256
OUT_PROJ_TILE = 1024


def _merge_kernel(h_ref, wg_ref, ya_ref, yb_ref, yc_ref, yd_ref, p_ref, *rest, nside):
    h = h_ref[...]
    d = h.shape[1]
    acc = None
    row = 0
    for i, y_ref in enumerate((ya_ref, yb_ref, yc_ref, yd_ref)):
        width = y_ref.shape[1]
        gate = _sigmoid(jnp.dot(h, wg_ref[i * d:(i + 1) * d, :], preferred_element_type=F32))
        term = gate * jnp.dot(y_ref[...], p_ref[row:row + width, :], preferred_element_type=F32)
        acc = term if acc is None else acc + term
        row += width
    rest[nside][...] = acc.astype(BF16)
    for k in range(nside):
        _cast_tiles(rest[k], rest[nside + 1 + k])


def _gated_merge(h, ys, w_gate_t, w_branch, l, *, tm=1024, sides=()):
    m, d = h.shape
    tm, tn = min(tm, m), MERGE_TILE
    gm, gn = m // tm, d // tn
    s_in, s_out, s_shapes, s_args = _side_cast_specs(sides, gm, gn)
    resident = lambda width: pl.BlockSpec((tm, width), lambda i, j: (i, 0), pipeline_mode=pl.Buffered(1))
    return pl.pallas_call(
        functools.partial(_merge_kernel, nside=len(sides)),
        grid=(gm, gn),
        in_specs=[resident(d),
                  pl.BlockSpec((None, len(ys) * d, tn), lambda i, j: (j, 0, 0))]
                 + [resident(y.shape[1]) for y in ys]
                 + [pl.BlockSpec((None, w_branch.shape[1], tn), lambda i, j: (l, 0, j))] + s_in,
        out_specs=[pl.BlockSpec((tm, tn), lambda i, j: (i, j))] + s_out,
        out_shape=[jax.ShapeDtypeStruct((m, d), BF16)] + s_shapes,
        compiler_params=_cparams(("parallel", "arbitrary")),
        name="gated_merge",
    )(h, w_gate_t, *ys, w_branch, *s_args)


def _cross_kernel(x_ref, gc_ref, wq_ref, kv_ref, wo_ref, gn_ref, o_ref, h_ref):
    nt = (((1,), (1,)), ((), ()))
    scale = np.float32(MEM_HEADDIM ** -0.5)
    hw = MEM_HEADS * MEM_HEADDIM
    x = x_ref[...]
    ms = jnp.mean(x * x, axis=-1, keepdims=True)
    xn = (x * lax.rsqrt(ms + EPS) * gc_ref[...]).astype(BF16)
    q = jnp.dot(xn, wq_ref[...].astype(BF16), preferred_element_type=F32).astype(BF16)
    heads = []
    for h in range(MEM_HEADS):
        hs = slice(h * MEM_HEADDIM, (h + 1) * MEM_HEADDIM)
        s = lax.dot_general(q[:, hs], kv_ref[:, hs], nt, preferred_element_type=F32) * scale
        e = jnp.exp(s - jnp.max(s, axis=-1, keepdims=True))
        den = jnp.sum(e, axis=-1, keepdims=True)
        vs = slice(hw + h * MEM_HEADDIM, hw + (h + 1) * MEM_HEADDIM)
        heads.append((jnp.dot(e.astype(BF16), kv_ref[:, vs], preferred_element_type=F32) / den).astype(BF16))
    att = jnp.concatenate(heads, axis=1)
    y = x + jnp.dot(att, wo_ref[...].astype(BF16), preferred_element_type=F32)
    o_ref[...] = y
    ms2 = jnp.mean(y * y, axis=-1, keepdims=True)
    h_ref[...] = (y * lax.rsqrt(ms2 + EPS) * gn_ref[...]).astype(h_ref.dtype)


def _cross_block(x, g_cross, w_q, kv, w_o, g_next, l, seq, mem_len, *, tm=256):
    m, d = x.shape
    hw = MEM_HEADS * MEM_HEADDIM
    tm = min(tm, seq)
    per_seq = seq // tm
    row = pl.BlockSpec((tm, d), lambda i: (i, 0))
    vec = pl.BlockSpec((1, d), lambda i: (0, 0))
    once = pl.Buffered(1)
    return pl.pallas_call(
        _cross_kernel,
        grid=(m // tm,),
        in_specs=[row, vec,
                  pl.BlockSpec((None, d, hw), lambda i: (l, 0, 0), pipeline_mode=once),
                  pl.BlockSpec((mem_len, 2 * hw), lambda i: (i // per_seq, 0)),
                  pl.BlockSpec((None, hw, d), lambda i: (l, 0, 0), pipeline_mode=once),
                  vec],
        out_specs=[row, row],
        out_shape=[jax.ShapeDtypeStruct((m, d), F32), jax.ShapeDtypeStruct((m, d), BF16)],
        compiler_params=_cparams(("parallel",)),
        name="mem_cross_block",
    )(x, g_cross.reshape(1, d), w_q, kv, w_o, g_next.reshape(1, d))


def kernel(x, mem, g_ffn1, w_ffn1_in, w_ffn1_out, g_mix, w_in, pool_w, pool_scale, sg_ln_g, sg_ln_b, sg_w, sg_b, ssm_conv_w, ssm_conv_b, ssm_a_log, ssm_dt_bias, ssm_d, ssm_norm_g, w_branch, w_gate, w_out, g_mem, g_cross, w_mem_q, w_mem_kv, w_mem_o, g_ffn2, w_ffn2_in, w_ffn2_out, g_final):
    bsz, seq, d = x.shape
    mem_len = mem.shape[1]
    depth = w_in.shape[0]
    m = bsz * seq
    bf = lambda a: a.astype(BF16)

    w_in_t = jnp.swapaxes(w_in, 1, 2)
    w_q = w_in_t[:, C_Q:C_KI]
    w_ki = w_in_t[:, C_KI:C_WI]
    w_s = jnp.concatenate(
        [w_ki, w_ki, w_in_t[:, C_DT:C_Q], w_in_t[:, C_WI:],
         jnp.zeros((depth, LANE - SSM_HEADS - IDX_HEADS, d), F32)], axis=1)
    w_gate_rows = w_gate.reshape(depth, -1, d)
    wb = bf(w_branch)
    pw = bf(pool_w)

    expand = lambda v: jnp.repeat(v, SSM_HEADDIM, axis=-1).reshape(depth, 1, SSM_INNER)
    a_exp = expand(-jnp.exp(ssm_a_log))
    d_exp = expand(ssm_d)
    dtb_pad = jnp.pad(ssm_dt_bias, ((0, 0), (0, LANE - SSM_HEADS))).reshape(depth, 1, LANE)
    e_mat = _head_expand_matrix()
    bias_tile = jnp.repeat(jnp.swapaxes(sg_b, 1, 2), SG_GW, axis=2)
    tabs = _rope_tables(seq)

    x2 = x.reshape(m, d)
    mem_n = _rmsnorm(mem.reshape(bsz * mem_len, d), g_mem, BF16)

    w1_in, l1_in = w_ffn1_in, 0
    for l in range(depth):
        x2, wg_t, wo_t = _ffn(x2, _rmsnorm(x2, g_ffn1[l], BF16), w1_in, l1_in, w_ffn1_out, l,
                              sides=[(w_gate_rows, l, MERGE_TILE), (w_out, l, OUT_PROJ_TILE)])

        h = _rmsnorm(x2, g_mix[l], BF16)
        seg_a = _matmul_nt(h, w_in_t, l, SEG_A, out_dtype=BF16)
        seg_q = _matmul_nt(h, w_q, l, SEG_Q, out_dtype=F32)
        seg_s = _matmul_nt(h, w_s, l, SEG_S, tn=256, out_dtype=F32)
        y_a = _pool_mixer(seg_a, pw, pool_scale, l, bsz, seq)
        y_b = _sg_mixer(seg_a, sg_ln_g, sg_ln_b, sg_w, bias_tile, l, m)
        y_c = _ssd_mixer(seg_a, seg_s, ssm_conv_w, ssm_conv_b, dtb_pad, a_exp, d_exp,
                         ssm_norm_g, e_mat, l, bsz, seq)
        y_d = _dsa_mixer(seg_q.reshape(bsz, seq, SEG_Q), seg_s.reshape(bsz, seq, SEG_S), tabs, bsz, seq)
        merged, w2_in_t = _gated_merge(h, (y_a, y_b, y_c, y_d.reshape(m, -1)), wg_t, wb, l,
                                       sides=[(w_ffn2_in, l, FFN_IN_TILE_BF16)])
        x2, = _matmul_res(merged, wo_t, None, x2, 1.0, tn=OUT_PROJ_TILE)

        kv = _matmul(mem_n, w_mem_kv, (l,), 2 * MEM_HEADS * MEM_HEADDIM, out_dtype=BF16)
        x2, h2 = _cross_block(x2, g_cross[l], w_mem_q, kv, w_mem_o, g_ffn2[l], l, seq, mem_len)

        nxt = [(w_ffn1_in, l + 1, FFN_IN_TILE_BF16)] if l + 1 < depth else []
        x2, *cast = _ffn(x2, h2, w2_in_t, None, w_ffn2_out, l, sides=nxt)
        if cast:
            w1_in, l1_in = cast[0], None

    return _rmsnorm(x2, g_final, F32).reshape(bsz, seq, d)
```
